```python
import math
import jax, jax.numpy as jnp
from jax import lax
import numpy as np

D_MODEL = 1024
BATCH = 32
SEQ = 2048
DEPTH = 2
DEC_BATCH = 32
DEC_SEQ = 32
PAST_LEN = 4096

CHUNK = 64
N_META = 16
N_EVEN = (DEPTH + 1) // 2
N_ODD = DEPTH // 2
HD_A = 64
W_A = D_MODEL // 2
H_A = W_A // HD_A
R_W = 64
R_A = 64
R_G = 128
A_COLS = 3 * W_A + R_W + R_A + R_G
HD_B = 64
W_B = D_MODEL // 2
H_B = W_B // HD_B
B_COLS = 3 * W_B + H_B
N_IN_E = A_COLS + B_COLS
Q_BLOCK = 128
FORGET_BIAS_INIT = 2.0
GC = 16
G_C = D_MODEL // GC
P_C = 64
D_FF = ((8 * D_MODEL // 3 + 127) // 128) * 128
CONV_W = 3
EPS = 1e-6
LN_EPS = 64e-5

kernel_name = 'hybrid_rwkv7_fox_s5_streaming_step'


def rms_norm(x, g):
    xf = x.astype(jnp.float32)
    return xf * lax.rsqrt(jnp.mean(xf * xf, axis=-1, keepdims=True) + EPS) * g.astype(jnp.float32)


def wkv7_scan(r, decay, k, v, kk, a, s0):
    def step(s, inp):
        r_t, w_t, k_t, v_t, kk_t, a_t = inp
        sa = jnp.einsum('bhij,bhj->bhi', s, kk_t)
        s = (s * w_t[:, :, None, :] - sa[..., None] * (kk_t * a_t)[:, :, None, :]
             + v_t[..., None] * k_t[:, :, None, :])
        return s, jnp.einsum('bhij,bhj->bhi', s, r_t)
    xs = tuple(jnp.moveaxis(t.astype(jnp.float32), 1, 0) for t in (r, decay, k, v, kk, a))
    s, ys = lax.scan(step, s0.astype(jnp.float32), xs)
    return s, jnp.moveaxis(ys, 0, 1)


def rwkv7_mix(u, shift_prev, wkv0, mu, w0, w2, a0, a2, g2, k_k, k_a, r_k, ln_w, ln_b):
    b, l = u.shape[:2]
    u_prev = jnp.concatenate([shift_prev[:, None].astype(u.dtype), u[:, :-1]], axis=1)
    um = u + (u_prev - u) * mu
    r, k, v, wd, ad, gd = jnp.split(
        um, [W_A, 2 * W_A, 3 * W_A, 3 * W_A + R_W, 3 * W_A + R_W + R_A], axis=-1)
    w_log = -jax.nn.softplus(-(w0 + jnp.tanh(wd) @ w2)) - 0.5
    decay = jnp.exp(-jnp.exp(w_log.astype(jnp.float32)))
    a = jax.nn.sigmoid(a0 + ad @ a2)
    g = jax.nn.sigmoid(gd) @ g2
    heads = lambda t: t.reshape(b, l, H_A, HD_A)
    kk = heads(k * k_k).astype(jnp.float32)
    kk = kk * lax.rsqrt(jnp.sum(kk * kk, axis=-1, keepdims=True) + 1e-12)
    k = k * (1.0 + (a - 1.0) * k_a)
    r, k, v, decay, a = heads(r), heads(k), heads(v), heads(decay), heads(a)
    wkv, y = wkv7_scan(r, decay, k, v, kk, a, wkv0)
    mean = jnp.mean(y, axis=-1, keepdims=True)
    var = jnp.mean(jnp.square(y - mean), axis=-1, keepdims=True)
    y = (y - mean) * lax.rsqrt(var + LN_EPS)
    bonus = jnp.sum(r * k * r_k, axis=-1, keepdims=True) * v
    y = (y.reshape(b, l, W_A) * ln_w + ln_b + bonus.reshape(b, l, W_A)) * g
    return y, u[:, -1], wkv


def fox_attention(q, k, v, logf, past):
    l = q.shape[1]
    if past is None:
        k_all, v_all, lf_all, p = k, v, logf, 0
    else:
        pk, pv, plf = past
        p = pk.shape[1]
        k_all = jnp.concatenate([pk, k], axis=1)
        v_all = jnp.concatenate([pv, v], axis=1)
        lf_all = jnp.concatenate([plf, logf], axis=1)
    c = jnp.transpose(jnp.cumsum(lf_all.astype(jnp.float32), axis=1), (0, 2, 1))
    scale = HD_B ** -0.5
    outs = []
    for qs in range(0, l, Q_BLOCK):
        qe = min(l, qs + Q_BLOCK)
        ke = p + qe
        s = jnp.einsum('bqhd,bkhd->bhqk', q[:, qs:qe], k_all[:, :ke]).astype(jnp.float32) * scale
        s = s + c[:, :, p + qs:p + qe, None] - c[:, :, None, :ke]
        causal = jnp.arange(ke)[None, :] <= (p + jnp.arange(qs, qe))[:, None]
        s = jnp.where(causal, s, -jnp.inf)
        probs = jax.nn.softmax(s, axis=-1).astype(v_all.dtype)
        outs.append(jnp.einsum('bhqk,bkhd->bqhd', probs, v_all[:, :ke]))
    return jnp.concatenate(outs, axis=1)


def fox_mix(u, past, f_bias, q_gain, k_gain):
    b, l = u.shape[:2]
    q, k, v, f = jnp.split(u, [W_B, 2 * W_B, 3 * W_B], axis=-1)
    q = rms_norm(q.reshape(b, l, H_B, HD_B), q_gain)
    k = rms_norm(k.reshape(b, l, H_B, HD_B), k_gain)
    v = v.reshape(b, l, H_B, HD_B)
    logf = jax.nn.log_sigmoid((f + f_bias).astype(jnp.float32))
    y = fox_attention(q, k, v, logf, past)
    return y.reshape(b, l, W_B), k, v, logf


def complex_affine_combine(e1, e2):
    a1r, a1i, b1r, b1i = e1
    a2r, a2i, b2r, b2i = e2
    return (a1r * a2r - a1i * a2i, a1r * a2i + a1i * a2r,
            a2r * b1r - a2i * b1i + b2r, a2r * b1i + a2i * b1r + b2i)


def s5_mix(u, s0_re, s0_im, lam_re, lam_im, log_dt, b_re, b_im, c_re, c_im, d_skip):
    b, l = u.shape[:2]
    ug = u.astype(jnp.float32).reshape(b, l, G_C, GC)
    lam_re = lam_re.astype(jnp.float32)
    lam_im = lam_im.astype(jnp.float32)
    dt = jnp.exp(log_dt.astype(jnp.float32))[:, None]
    mag = jnp.exp(lam_re * dt)
    ab_re = mag * jnp.cos(lam_im * dt)
    ab_im = mag * jnp.sin(lam_im * dt)
    den = lam_re * lam_re + lam_im * lam_im
    f_re = ((ab_re - 1.0) * lam_re + ab_im * lam_im) / den
    f_im = (ab_im * lam_re - (ab_re - 1.0) * lam_im) / den
    bb_re = f_re[..., None] * b_re - f_im[..., None] * b_im
    bb_im = f_re[..., None] * b_im + f_im[..., None] * b_re
    bu_re = jnp.einsum('blgc,gpc->blgp', ug, bb_re)
    bu_im = jnp.einsum('blgc,gpc->blgp', ug, bb_im)
    s0_re = s0_re.astype(jnp.float32)
    s0_im = s0_im.astype(jnp.float32)
    bu_re = bu_re.at[:, 0].add(ab_re * s0_re - ab_im * s0_im)
    bu_im = bu_im.at[:, 0].add(ab_re * s0_im + ab_im * s0_re)
    a_re = jnp.broadcast_to(ab_re, (1, l, G_C, P_C))
    a_im = jnp.broadcast_to(ab_im, (1, l, G_C, P_C))
    _, _, s_re, s_im = lax.associative_scan(complex_affine_combine, (a_re, a_im, bu_re, bu_im), axis=1)
    y = (jnp.einsum('blgp,gcp->blgc', s_re, c_re) - jnp.einsum('blgp,gcp->blgc', s_im, c_im)
         + d_skip * ug)
    return y.reshape(b, l, D_MODEL), s_re[:, -1], s_im[:, -1]


def conv_ffn(h, buf, w_up, conv_w, conv_b, w_down):
    up = h @ w_up
    l = up.shape[1]
    hp = jnp.concatenate([buf.astype(up.dtype), up], axis=1)
    z = conv_b + sum(conv_w[i] * hp[:, i:i + l] for i in range(CONV_W))
    val, gate = jnp.split(z, 2, axis=-1)
    return (jax.nn.silu(gate) * val) @ w_down, hp[:, -(CONV_W - 1):]


def trunk(x, init, prm):
    a_shift, a_wkv, b_k, b_v, b_lf, c_re, c_im, f_conv = [], [], [], [], [], [], [], []
    for layer in range(DEPTH):
        j = layer // 2
        h = rms_norm(x, prm['norm_mix'][layer])
        if layer % 2 == 0:
            proj = h @ prm['w_in_e'][j]
            y_a, sh, wkv = rwkv7_mix(
                proj[..., :A_COLS], init['a_shift'][j], init['a_wkv'][j], prm['a_mu'][j],
                prm['a_w0'][j], prm['a_w2'][j], prm['a_a0'][j], prm['a_a2'][j], prm['a_g2'][j],
                prm['a_kk'][j], prm['a_ka'][j], prm['a_rk'][j], prm['a_ln_w'][j], prm['a_ln_b'][j])
            y_b, k, v, lf = fox_mix(proj[..., A_COLS:], init['b_past'][j], prm['b_fbias'][j],
                                    prm['b_qnorm'][j], prm['b_knorm'][j])
            mix = jnp.concatenate([y_a, y_b], axis=-1) @ prm['w_out_e'][j]
            a_shift.append(sh)
            a_wkv.append(wkv)
            b_k.append(k)
            b_v.append(v)
            b_lf.append(lf)
        else:
            y_c, sre, sim = s5_mix(
                h, init['c_re'][j], init['c_im'][j], prm['c_lam_re'][j], prm['c_lam_im'][j],
                prm['c_log_dt'][j], prm['c_b_re'][j], prm['c_b_im'][j], prm['c_c_re'][j],
                prm['c_c_im'][j], prm['c_d'][j])
            ga, gb = jnp.split(jax.nn.gelu(y_c) @ prm['w_glu'][j], 2, axis=-1)
            mix = ga * jax.nn.sigmoid(gb)
            c_re.append(sre)
            c_im.append(sim)
        x = x + mix.astype(x.dtype)
        f_out, buf = conv_ffn(rms_norm(x, prm['norm_ffn'][layer]), init['ffn_conv'][layer],
                              prm['w_up'][layer], prm['conv_w'][layer], prm['conv_b'][layer],
                              prm['w_down'][layer])
        x = x + f_out.astype(x.dtype)
        f_conv.append(buf)
    st = lambda t: jnp.stack(t, axis=0)
    return x, (st(a_shift), st(a_wkv), st(b_k), st(b_v), st(b_lf), st(c_re), st(c_im), st(f_conv))


def setup_inputs(seed: int = 0) -> dict:
    key = jax.random.key(seed)
    keys = list(jax.random.split(key, 48))

    def nrm(shape, scale):
        return jax.random.normal(keys.pop(), shape, jnp.float32) * scale

    def uni(shape, lo, hi):
        return jax.random.uniform(keys.pop(), shape, jnp.float32, lo, hi)

    return dict(
        x_prompt=nrm((BATCH, SEQ, D_MODEL), 1.0),
        x_sample=nrm((DEC_BATCH, DEC_SEQ, D_MODEL), 1.0),
        state_a_shift=nrm((N_EVEN, DEC_BATCH, A_COLS), 1.0),
        state_a_wkv=nrm((N_EVEN, DEC_BATCH, H_A, HD_A, HD_A), 1.0),
        cache_b_k=nrm((N_EVEN, DEC_BATCH, PAST_LEN, H_B, HD_B), 1.0),
        cache_b_v=nrm((N_EVEN, DEC_BATCH, PAST_LEN, H_B, HD_B), 1.0),
        cache_b_logf=jax.nn.log_sigmoid(nrm((N_EVEN, DEC_BATCH, PAST_LEN, H_B), 1.0) + FORGET_BIAS_INIT),
        state_c_re=nrm((N_ODD, DEC_BATCH, G_C, P_C), 1.0),
        state_c_im=nrm((N_ODD, DEC_BATCH, G_C, P_C), 1.0),
        state_ffn_conv=nrm((DEPTH, DEC_BATCH, CONV_W - 1, 2 * D_FF), 1.0),
        meta=nrm((N_META, D_MODEL), 1.0),
        norm_mix=1.0 + nrm((DEPTH, D_MODEL), 0.02),
        norm_ffn=1.0 + nrm((DEPTH, D_MODEL), 0.02),
        w_in_e=nrm((N_EVEN, D_MODEL, N_IN_E), D_MODEL ** -0.5),
        a_mu=uni((N_EVEN, A_COLS), 0.0, 1.0),
        a_w0=nrm((N_EVEN, W_A), 0.5) - 1.0,
        a_w2=nrm((N_EVEN, R_W, W_A), 0.1 * R_W ** -0.5),
        a_a0=nrm((N_EVEN, W_A), 0.5),
        a_a2=nrm((N_EVEN, R_A, W_A), 0.1 * R_A ** -0.5),
        a_g2=nrm((N_EVEN, R_G, W_A), R_G ** -0.5),
        a_kk=1.0 + nrm((N_EVEN, W_A), 0.1),
        a_ka=1.0 + nrm((N_EVEN, W_A), 0.1),
        a_rk=nrm((N_EVEN, H_A, HD_A), 0.1),
        a_ln_w=1.0 + nrm((N_EVEN, W_A), 0.02),
        a_ln_b=nrm((N_EVEN, W_A), 0.02),
        b_fbias=FORGET_BIAS_INIT + nrm((N_EVEN, H_B), 0.1),
        b_qnorm=1.0 + nrm((N_EVEN, HD_B), 0.02),
        b_knorm=1.0 + nrm((N_EVEN, HD_B), 0.02),
        w_out_e=nrm((N_EVEN, W_A + W_B, D_MODEL), (W_A + W_B) ** -0.5),
        c_lam_re=-0.5 + nrm((N_ODD, G_C, P_C), 0.01),
        c_lam_im=jnp.broadcast_to(jnp.pi * jnp.arange(P_C, dtype=jnp.float32), (N_ODD, G_C, P_C)),
        c_log_dt=uni((N_ODD, G_C), math.log(1e-3), math.log(1e-1)),
        c_b_re=nrm((N_ODD, G_C, P_C, GC), (2 * GC) ** -0.5),
        c_b_im=nrm((N_ODD, G_C, P_C, GC), (2 * GC) ** -0.5),
        c_c_re=nrm((N_ODD, G_C, GC, P_C), (2 * P_C) ** -0.5),
        c_c_im=nrm((N_ODD, G_C, GC, P_C), (2 * P_C) ** -0.5),
        c_d=nrm((N_ODD, G_C, GC), 0.5),
        w_glu=nrm((N_ODD, D_MODEL, 2 * D_MODEL), D_MODEL ** -0.5),
        w_up=nrm((DEPTH, D_MODEL, 2 * D_FF), D_MODEL ** -0.5),
        conv_w=nrm((DEPTH, CONV_W, 2 * D_FF), CONV_W ** -0.5),
        conv_b=nrm((DEPTH, 2 * D_FF), 0.02),
        w_down=nrm((DEPTH, D_FF, D_MODEL), D_FF ** -0.5),
    )


def reference(x_prompt, x_sample, state_a_shift, state_a_wkv, cache_b_k, cache_b_v, cache_b_logf,
              state_c_re, state_c_im, state_ffn_conv, meta, norm_mix, norm_ffn, w_in_e, a_mu, a_w0,
              a_w2, a_a0, a_a2, a_g2, a_kk, a_ka, a_rk, a_ln_w, a_ln_b, b_fbias, b_qnorm, b_knorm,
              w_out_e, c_lam_re, c_lam_im, c_log_dt, c_b_re, c_b_im, c_c_re, c_c_im, c_d, w_glu,
              w_up, conv_w, conv_b, w_down):
    prm = dict(norm_mix=norm_mix, norm_ffn=norm_ffn, w_in_e=w_in_e, a_mu=a_mu, a_w0=a_w0,
               a_w2=a_w2, a_a0=a_a0, a_a2=a_a2, a_g2=a_g2, a_kk=a_kk, a_ka=a_ka, a_rk=a_rk,
               a_ln_w=a_ln_w, a_ln_b=a_ln_b, b_fbias=b_fbias, b_qnorm=b_qnorm, b_knorm=b_knorm,
               w_out_e=w_out_e, c_lam_re=c_lam_re, c_lam_im=c_lam_im, c_log_dt=c_log_dt,
               c_b_re=c_b_re, c_b_im=c_b_im, c_c_re=c_c_re, c_c_im=c_c_im, c_d=c_d, w_glu=w_glu,
               w_up=w_up, conv_w=conv_w, conv_b=conv_b, w_down=w_down)
    b = x_prompt.shape[0]
    meta_rows = jnp.broadcast_to(meta.astype(x_prompt.dtype)[None], (b, N_META, D_MODEL))
    xp = jnp.concatenate([meta_rows, x_prompt], axis=1)
    init_p = dict(
        a_shift=jnp.zeros((N_EVEN, b, A_COLS), jnp.float32),
        a_wkv=jnp.zeros((N_EVEN, b, H_A, HD_A, HD_A), jnp.float32),
        b_past=[None] * N_EVEN,
        c_re=jnp.zeros((N_ODD, b, G_C, P_C), jnp.float32),
        c_im=jnp.zeros((N_ODD, b, G_C, P_C), jnp.float32),
        ffn_conv=jnp.zeros((DEPTH, b, CONV_W - 1, 2 * D_FF), jnp.float32))
    yp, (p_a_shift, p_a_wkv, p_b_k, p_b_v, p_b_logf, p_c_re, p_c_im, p_ffn_conv) = trunk(xp, init_p, prm)
    init_s = dict(
        a_shift=state_a_shift,
        a_wkv=state_a_wkv,
        b_past=[(cache_b_k[j], cache_b_v[j], cache_b_logf[j]) for j in range(N_EVEN)],
        c_re=state_c_re,
        c_im=state_c_im,
        ffn_conv=state_ffn_conv)
    ys, (s_a_shift, s_a_wkv, s_b_k, s_b_v, s_b_logf, s_c_re, s_c_im, s_ffn_conv) = trunk(x_sample, init_s, prm)
    return (yp[:, N_META:], ys, p_a_shift, p_a_wkv, p_b_k, p_b_v, p_b_logf, p_c_re, p_c_im, p_ffn_conv,
            s_a_shift, s_a_wkv, s_b_k, s_b_v, s_b_logf, s_c_re, s_c_im, s_ffn_conv)
```

```python
import functools
import math

import jax
import jax.numpy as jnp
from jax import lax
from jax.experimental import pallas as pl
from jax.experimental.pallas import tpu as pltpu

F32, BF16 = jnp.float32, jnp.bfloat16

D_MODEL = 1024
N_HEADS = 8
HEAD_DIM = 64
W_MIX = N_HEADS * HEAD_DIM
R_W, R_A, R_G = 64, 64, 128
A_COLS = 3 * W_MIX + R_W + R_A + R_G
B_PAD_COLS = 3 * W_MIX + 128
D_FF = 2816
CONV_W = 3
G_C, GC, P_C = 64, 16, 64
S5_STATE = G_C * P_C
EPS = 1e-6
LN_EPS = 64e-5
NEG = -1e30

LANES = 128
SUBLANES = 8
VMEM_LIMIT = 56 * 1024 * 1024
WKV_CHUNK = 64
WKV_SUB = 16
S5_BATCH = 8
TIME_TILE = 704
ATTN_TILE = 768


def _cparams(*sem):
    return pltpu.CompilerParams(dimension_semantics=sem, vmem_limit_bytes=VMEM_LIMIT)


def _const_spec(shape):
    nd = len(shape)
    return pl.BlockSpec(shape, lambda *_: (0,) * nd, pipeline_mode=pl.Buffered(1))


def _largest_tile(n, cap, mult):
    best = None
    for t in range(mult, min(n, cap) + 1, mult):
        if n % t == 0:
            best = t
    assert best is not None, (n, cap, mult)
    return best


def _mm(a, b):
    return jnp.dot(a.astype(BF16), b.astype(BF16), preferred_element_type=F32)


def _mm_nt(a, b):
    return lax.dot_general(a.astype(BF16), b.astype(BF16), (((1,), (1,)), ((), ())),
                           preferred_element_type=F32)


def _mm_tn(a, b):
    return lax.dot_general(a.astype(BF16), b.astype(BF16), (((0,), (0,)), ((), ())),
                           preferred_element_type=F32)


def _split2(x):
    hi = x.astype(BF16)
    lo = (x - hi.astype(F32)).astype(BF16)
    return hi, lo


def _split3(x):
    hi = x.astype(BF16)
    r = x - hi.astype(F32)
    mid = r.astype(BF16)
    lo = (r - mid.astype(F32)).astype(BF16)
    return hi, mid, lo


def _mm_exact_rhs(x, m):
    hi, mid, lo = _split3(x)
    return (jnp.dot(hi, m, preferred_element_type=F32) + jnp.dot(mid, m, preferred_element_type=F32)
            + jnp.dot(lo, m, preferred_element_type=F32))


def _headsum(x, e):
    hi, lo = _split2(x)
    return jnp.dot(hi, e, preferred_element_type=F32) + jnp.dot(lo, e, preferred_element_type=F32)


def _rms(x, g):
    return x * lax.rsqrt(jnp.mean(x * x, axis=-1, keepdims=True) + EPS) * g


def _sigmoid(x):
    return 1.0 / (1.0 + jnp.exp(-x))


def _log_sigmoid(x):
    return jnp.minimum(x, 0.0) - jnp.log(1.0 + jnp.exp(-jnp.abs(x)))


def _gelu(x):
    return 0.5 * x * (1.0 + jnp.tanh(math.sqrt(2.0 / math.pi) * (x + 0.044715 * (x * x * x))))


def _inproj_kernel(x_ref, g_ref, wa_ref, wb_ref, e_ref, qg_ref, kg_ref, fb_ref,
                   ua_ref, q_ref, kf_ref, kb_ref, vf_ref, vb_ref, lf_ref):
    h = _rms(x_ref[...], g_ref[...]).astype(BF16)
    ua_ref[...] = jnp.dot(h, wa_ref[...], preferred_element_type=F32)
    ub = jnp.dot(h, wb_ref[...], preferred_element_type=F32)
    q = ub[:, 0:W_MIX]
    k = ub[:, W_MIX:2 * W_MIX]
    v = ub[:, 2 * W_MIX:3 * W_MIX]
    f = ub[:, 3 * W_MIX:]
    e = e_ref[...]
    qn = q * lax.rsqrt(_headsum(q * q, e) * (1.0 / HEAD_DIM) + EPS) * qg_ref[...]
    kn = k * lax.rsqrt(_headsum(k * k, e) * (1.0 / HEAD_DIM) + EPS) * kg_ref[...]
    q_ref[...] = (qn * (HEAD_DIM ** -0.5)).astype(BF16)
    kf_ref[...] = kn
    kb_ref[...] = kn.astype(BF16)
    vf_ref[...] = v
    vb_ref[...] = v.astype(BF16)
    lf = _log_sigmoid(f + fb_ref[...])
    lane = lax.broadcasted_iota(jnp.int32, lf.shape, 1)
    lf_ref[...] = jnp.where(lane < N_HEADS, lf, 0.0)


def _inproj(x2d, g, wa, wb, e, qg, kg, fb):
    rows = x2d.shape[0]
    tm = _largest_tile(rows, 512, SUBLANES)
    row = lambda c: pl.BlockSpec((tm, c), lambda i: (i, 0))
    outs = [(A_COLS, F32), (W_MIX, BF16), (W_MIX, F32), (W_MIX, BF16), (W_MIX, F32), (W_MIX, BF16),
            (LANES, F32)]
    return pl.pallas_call(
        _inproj_kernel,
        grid=(rows // tm,),
        in_specs=[row(D_MODEL), _const_spec(g.shape), _const_spec(wa.shape), _const_spec(wb.shape),
                  _const_spec(e.shape), _const_spec(qg.shape), _const_spec(kg.shape), _const_spec(fb.shape)],
        out_specs=[row(c) for c, _ in outs],
        out_shape=[jax.ShapeDtypeStruct((rows, c), dt) for c, dt in outs],
        compiler_params=_cparams("parallel"),
        name="inproj",
    )(x2d, g, wa, wb, e, qg, kg, fb)


def _cumsum_kernel(x_ref, tri_ref, o_ref, *, nblk):
    tri = tri_ref[...]
    carry = jnp.zeros((x_ref.shape[0], 1), F32)
    for j in range(nblk):
        cs = _mm_exact_rhs(x_ref[:, j * LANES:(j + 1) * LANES], tri) + carry
        o_ref[:, j * LANES:(j + 1) * LANES] = cs
        carry = cs[:, LANES - 1:LANES]


def _cumsum_lanes(x):
    rows, n = x.shape
    assert n % LANES == 0
    tri = (jnp.arange(LANES)[:, None] <= jnp.arange(LANES)[None, :]).astype(BF16)
    return pl.pallas_call(
        functools.partial(_cumsum_kernel, nblk=n // LANES),
        grid=(1,),
        in_specs=[_const_spec(x.shape), _const_spec(tri.shape)],
        out_specs=_const_spec(x.shape),
        out_shape=jax.ShapeDtypeStruct(x.shape, F32),
        compiler_params=_cparams("arbitrary"),
        name="cumsum",
    )(x, tri)


def _attn_prompt_kernel(q_ref, k_ref, v_ref, cq_ref, cr_ref, o_ref, *, bounds):
    p = pl.program_id(1)
    lane = lax.broadcasted_iota(jnp.int32, (1, LANES), 1)
    head_lanes = (lane < HEAD_DIM, lane >= HEAD_DIM)
    for qi in range(len(bounds) - 1):
        r0, r1 = bounds[qi], bounds[qi + 1]
        q2 = q_ref[0, r0:r1, :]
        cqt = cq_ref[0, r0:r1, :]
        lane_q = lax.broadcasted_iota(jnp.int32, cqt.shape, 1)
        res = []
        for hh in range(2):
            h = 2 * p + hh
            qm = jnp.where(head_lanes[hh], q2, jnp.zeros_like(q2))
            cq = jnp.sum(jnp.where(lane_q == h, cqt, 0.0), axis=-1, keepdims=True)
            ck_all = cr_ref[0, pl.ds(h, 1), :]
            m = jnp.full((r1 - r0, 1), NEG, F32)
            l = jnp.zeros((r1 - r0, 1), F32)
            acc = jnp.zeros((r1 - r0, LANES), F32)
            for ki in range(qi + 1):
                c0, c1 = bounds[ki], bounds[ki + 1]
                s = lax.dot_general(qm, k_ref[0, c0:c1, :], (((1,), (1,)), ((), ())),
                                    preferred_element_type=F32)
                s = s + cq - ck_all[:, c0:c1]
                if ki == qi:
                    ri = lax.broadcasted_iota(jnp.int32, s.shape, 0)
                    ci = lax.broadcasted_iota(jnp.int32, s.shape, 1)
                    s = jnp.where(ri >= ci, s, NEG)
                m_new = jnp.maximum(m, jnp.max(s, axis=-1, keepdims=True))
                alpha = jnp.exp(m - m_new)
                pe = jnp.exp(s - m_new)
                l = alpha * l + jnp.sum(pe, axis=-1, keepdims=True)
                acc = alpha * acc + jnp.dot(pe.astype(BF16), v_ref[0, c0:c1, :],
                                            preferred_element_type=F32)
                m = m_new
            res.append(acc / l)
        o_ref[0, r0:r1, :] = jnp.where(head_lanes[0], res[0], res[1]).astype(BF16)


def _attn_prompt(q, k, v, cq, cr, tile):
    b, lp, _ = q.shape
    bounds = tuple(range(0, lp, tile)) + (lp,)
    col = pl.BlockSpec((1, lp, LANES), lambda i, p: (i, 0, p))
    return pl.pallas_call(
        functools.partial(_attn_prompt_kernel, bounds=bounds),
        grid=(b, N_HEADS // 2),
        in_specs=[col, col, col,
                  pl.BlockSpec((1, lp, LANES), lambda i, p: (i, 0, 0)),
                  pl.BlockSpec((1, N_HEADS, lp), lambda i, p: (i, 0, 0))],
        out_specs=col,
        out_shape=jax.ShapeDtypeStruct((b, lp, W_MIX), BF16),
        compiler_params=_cparams("parallel", "arbitrary"),
        name="attn_prompt",
    )(q, k, v, cq, cr)


def _attn_sample_kernel(q_ref, cq_ref, kp_ref, vp_ref, cp_ref, kn_ref, vn_ref, cn_ref, hm_ref, o_ref,
                        qx_scr, m_scr, l_scr, acc_scr, *, nkp, ls):
    j = pl.program_id(1)
    rows = N_HEADS * ls

    @pl.when(j == 0)
    def _():
        q = q_ref[0]
        qx = jnp.broadcast_to(q[None], (N_HEADS, ls, W_MIX)) * hm_ref[...][:, None, :]
        qx_scr[...] = qx.reshape(rows, W_MIX)
        m_scr[...] = jnp.full(m_scr.shape, NEG, F32)
        l_scr[...] = jnp.zeros(l_scr.shape, F32)
        acc_scr[...] = jnp.zeros(acc_scr.shape, F32)

    def update(k, v, ck, causal):
        tk = k.shape[0]
        s = lax.dot_general(qx_scr[...], k, (((1,), (1,)), ((), ())), preferred_element_type=F32)
        ckx = jnp.broadcast_to(ck[:, None, :], (N_HEADS, ls, tk)).reshape(rows, tk)
        s = s + cq_ref[0][:, 0:1] - ckx
        if causal:
            ri = lax.broadcasted_iota(jnp.int32, (N_HEADS, ls, tk), 1).reshape(rows, tk)
            ci = lax.broadcasted_iota(jnp.int32, (rows, tk), 1)
            s = jnp.where(ri >= ci, s, NEG)
        m = m_scr[...]
        m_new = jnp.maximum(m, jnp.max(s, axis=-1, keepdims=True))
        alpha = jnp.exp(m - m_new)
        pe = jnp.exp(s - m_new)
        l_scr[...] = alpha * l_scr[...] + jnp.sum(pe, axis=-1, keepdims=True)
        acc_scr[...] = alpha * acc_scr[...] + jnp.dot(pe.astype(BF16), v, preferred_element_type=F32)
        m_scr[...] = m_new

    @pl.when(j < nkp)
    def _():
        update(kp_ref[0].astype(BF16), vp_ref[0].astype(BF16), cp_ref[0], False)

    @pl.when(j == nkp)
    def _():
        update(kn_ref[0], vn_ref[0], cn_ref[0], True)
        o = (acc_scr[...] / l_scr[...]).reshape(N_HEADS, ls, W_MIX) * hm_ref[...].astype(F32)[:, None, :]
        o_ref[0] = jnp.sum(o, axis=0).astype(BF16)


def _attn_sample(q, cq_stack, kp, vp, cp, kn, vn, cn, hm):
    b, ls, _ = q.shape
    past = kp.shape[1]
    tkp = _largest_tile(past, 1024, LANES)
    nkp = past // tkp
    rows = N_HEADS * ls
    pidx = lambda i, j: (i, jnp.minimum(j, nkp - 1), 0)
    return pl.pallas_call(
        functools.partial(_attn_sample_kernel, nkp=nkp, ls=ls),
        grid=(b, nkp + 1),
        in_specs=[pl.BlockSpec((1, ls, W_MIX), lambda i, j: (i, 0, 0)),
                  pl.BlockSpec((1, rows, LANES), lambda i, j: (i, 0, 0)),
                  pl.BlockSpec((1, tkp, W_MIX), pidx),
                  pl.BlockSpec((1, tkp, W_MIX), pidx),
                  pl.BlockSpec((1, N_HEADS, tkp), lambda i, j: (i, 0, jnp.minimum(j, nkp - 1))),
                  pl.BlockSpec((1, ls, W_MIX), lambda i, j: (i, 0, 0)),
                  pl.BlockSpec((1, ls, W_MIX), lambda i, j: (i, 0, 0)),
                  pl.BlockSpec((1, N_HEADS, ls), lambda i, j: (i, 0, 0)),
                  _const_spec(hm.shape)],
        out_specs=pl.BlockSpec((1, ls, W_MIX), lambda i, j: (i, 0, 0)),
        out_shape=jax.ShapeDtypeStruct((b, ls, W_MIX), BF16),
        scratch_shapes=[pltpu.VMEM((rows, W_MIX), BF16), pltpu.VMEM((rows, 1), F32),
                        pltpu.VMEM((rows, 1), F32), pltpu.VMEM((rows, W_MIX), F32)],
        compiler_params=_cparams("parallel", "arbitrary"),
        name="attn_sample",
    )(q, cq_stack, kp, vp, cp, kn, vn, cn, hm)


def _wkv_head(top, bot, v_h, bkT, wT_h, s_h, t):
    ri = lax.broadcasted_iota(jnp.int32, (t, t), 0)
    ci = lax.broadcasted_iota(jnp.int32, (t, t), 1)
    eye = (ri == ci).astype(F32)
    aa = _mm_nt(top, bot)
    n = jnp.where(ri > ci, aa[:t, :t], 0.0)
    a_ak = jnp.where(ri > ci, aa[:t, t:], 0.0)
    r2 = lax.broadcasted_iota(jnp.int32, (t, 2 * t), 0)
    c2 = lax.broadcasted_iota(jnp.int32, (t, 2 * t), 1)
    c2 = jnp.where(c2 >= t, c2 - t, c2)
    a_r = jnp.where(r2 >= c2, aa[t:, :], 0.0)
    same = (ri // WKV_SUB) == (ci // WKV_SUB)
    d = jnp.where(same, n, 0.0)
    lo = n - d
    d2 = _mm(d, d)
    d4 = _mm(d2, d2)
    d8 = _mm(d4, d4)
    x = _mm(_mm(_mm(eye - d, eye + d2), eye + d4), eye + d8)
    mb = _mm(x, lo)
    mb2 = _mm(mb, mb)
    ginv = _mm(eye - mb, _mm(eye + mb2, x))
    pp = _mm_nt(top, s_h)
    u = _mm(ginv, -(pp[:t] + _mm(a_ak, v_h)))
    uv = jnp.concatenate([u, v_h], axis=0)
    y = pp[t:] + _mm(a_r, uv)
    s_new = s_h * wT_h + _mm_tn(uv, bkT)
    return y, s_new


def _rwkv_kernel(u_ref, sh_ref, s0_ref, mu_ref, w0_ref, a0_ref, w2a_ref, g2_ref, kk_ref, ka_ref,
                 rk_ref, lnw_ref, lnb_ref, e_ref, tril_ref,
                 y_ref, sho_ref, so_ref,
                 ubuf, carry, s_scr, r_s, wl_s, k_s, v_s, kk_s, b_s, y_s, bon_s, g_s,
                 *, tl, t, l_real, nt):
    ti = pl.program_id(1)

    @pl.when(ti == 0)
    def _():
        carry[...] = jnp.broadcast_to(sh_ref[0], carry.shape)
        s_scr[...] = s0_ref[0]

    u = u_ref[0]
    ubuf[0:SUBLANES, :] = carry[...]
    ubuf[SUBLANES:SUBLANES + tl, :] = u
    u_prev = ubuf[SUBLANES - 1:SUBLANES - 1 + tl, :]
    carry[...] = u[tl - SUBLANES:tl, :]
    last_tile, last_row = (l_real - 1) // tl, (l_real - 1) % tl

    @pl.when(ti == last_tile)
    def _():
        sho_ref[0] = u[last_row:last_row + 1, :]

    um = u + (u_prev - u) * mu_ref[...]
    r = um[:, 0:W_MIX]
    k = um[:, W_MIX:2 * W_MIX]
    v = um[:, 2 * W_MIX:3 * W_MIX]
    wa = um[:, 3 * W_MIX:3 * W_MIX + R_W + R_A]
    gd = um[:, 3 * W_MIX + R_W + R_A:]
    lane = lax.broadcasted_iota(jnp.int32, wa.shape, 1)
    za = _mm(jnp.where(lane < R_W, jnp.tanh(wa), wa), w2a_ref[...])
    wl = -jnp.exp(_log_sigmoid(w0_ref[...] + za[:, 0:W_MIX]) - 0.5)
    a = _sigmoid(a0_ref[...] + za[:, W_MIX:])
    g_s[...] = _mm(_sigmoid(gd), g2_ref[...])
    e = e_ref[...]
    kk = k * kk_ref[...]
    kkn = kk * lax.rsqrt(_headsum(kk * kk, e) + 1e-12)
    k2 = k * (1.0 + (a - 1.0) * ka_ref[...])
    bon_s[...] = _headsum(r * k2 * rk_ref[...], e) * v
    valid = (ti * tl + lax.broadcasted_iota(jnp.int32, (tl, 1), 0)) < l_real
    r_s[...] = r
    wl_s[...] = jnp.where(valid, wl, 0.0)
    k_s[...] = jnp.where(valid, k2, 0.0)
    v_s[...] = jnp.where(valid, v, 0.0)
    kk_s[...] = jnp.where(valid, kkn, 0.0)
    b_s[...] = jnp.where(valid, kkn * a, 0.0)

    tril = tril_ref[...]

    def chunk_body(c, carry_):
        r0 = pl.multiple_of(c * t, t)
        rows = pl.ds(r0, t)
        wlc = wl_s[rows, :]
        hi, mid, lo = _split3(wlc)
        cw = (jnp.dot(tril, hi, preferred_element_type=F32) + jnp.dot(tril, mid, preferred_element_type=F32)
              + jnp.dot(tril, lo, preferred_element_type=F32))
        w_inc = jnp.exp(cw)
        w_inv = jnp.exp(-cw)
        w_prev = jnp.exp(cw - wlc)
        w_t = w_inc[t - 1:t, :]
        kkt = kk_s[rows, :] * w_prev
        rt = r_s[rows, :] * w_inc
        bt = b_s[rows, :] * w_inv
        kt = k_s[rows, :] * w_inv
        top = jnp.concatenate([kkt, rt], axis=0)
        bot = jnp.concatenate([bt, kt], axis=0)
        bk_t = bot * w_t
        vc = v_s[rows, :]
        for h in range(N_HEADS):
            hs = slice(h * HEAD_DIM, (h + 1) * HEAD_DIM)
            y_h, s_new = _wkv_head(top[:, hs], bot[:, hs], vc[:, hs], bk_t[:, hs], w_t[:, hs], s_scr[h], t)
            s_scr[h] = s_new
            y_s[rows, hs] = y_h
        return carry_

    lax.fori_loop(0, tl // t, chunk_body, 0)

    y = y_s[...]
    yc = y - _headsum(y, e) * (1.0 / HEAD_DIM)
    var = _headsum(yc * yc, e) * (1.0 / HEAD_DIM)
    yn = yc * lax.rsqrt(var + LN_EPS)
    y_ref[0] = ((yn * lnw_ref[...] + lnb_ref[...] + bon_s[...]) * g_s[...]).astype(BF16)

    @pl.when(ti == nt - 1)
    def _():
        so_ref[0] = s_scr[...]


def _rwkv(ua, shift0, wkv0, prm, l_real):
    b, lp, _ = ua.shape
    t = min(WKV_CHUNK, lp)
    tl = _largest_tile(lp, TIME_TILE, t)
    nt = lp // tl
    tril = (jnp.arange(t)[:, None] >= jnp.arange(t)[None, :]).astype(BF16)
    consts = [prm["mu"], prm["w0"], prm["a0"], prm["w2a"], prm["g2"], prm["kk"], prm["ka"], prm["rk"],
              prm["lnw"], prm["lnb"], prm["e"], tril]
    big = lambda: pltpu.VMEM((tl, W_MIX), F32)
    return pl.pallas_call(
        functools.partial(_rwkv_kernel, tl=tl, t=t, l_real=l_real, nt=nt),
        grid=(b, nt),
        in_specs=[pl.BlockSpec((1, tl, A_COLS), lambda i, j: (i, j, 0)),
                  pl.BlockSpec((1, 1, A_COLS), lambda i, j: (i, 0, 0)),
                  pl.BlockSpec((1, N_HEADS, HEAD_DIM, HEAD_DIM), lambda i, j: (i, 0, 0, 0))]
                 + [_const_spec(c.shape) for c in consts],
        out_specs=[pl.BlockSpec((1, tl, W_MIX), lambda i, j: (i, j, 0)),
                   pl.BlockSpec((1, 1, A_COLS), lambda i, j: (i, 0, 0)),
                   pl.BlockSpec((1, N_HEADS, HEAD_DIM, HEAD_DIM), lambda i, j: (i, 0, 0, 0))],
        out_shape=[jax.ShapeDtypeStruct((b, lp, W_MIX), BF16),
                   jax.ShapeDtypeStruct((b, 1, A_COLS), F32),
                   jax.ShapeDtypeStruct((b, N_HEADS, HEAD_DIM, HEAD_DIM), F32)],
        scratch_shapes=[pltpu.VMEM((tl + SUBLANES, A_COLS), F32), pltpu.VMEM((SUBLANES, A_COLS), F32),
                        pltpu.VMEM((N_HEADS, HEAD_DIM, HEAD_DIM), F32)] + [big() for _ in range(9)],
        compiler_params=_cparams("parallel", "arbitrary"),
        name="rwkv",
    )(ua, shift0, wkv0, *consts)


def _outproj_kernel(x_ref, ya_ref, yb_ref, w1_ref, w2_ref, o_ref):
    o_ref[...] = (x_ref[...] + jnp.dot(ya_ref[...], w1_ref[...], preferred_element_type=F32)
                  + jnp.dot(yb_ref[...], w2_ref[...], preferred_element_type=F32))


def _outproj(x2d, ya, yb, w1, w2):
    rows = x2d.shape[0]
    tm = _largest_tile(rows, 1024, SUBLANES)
    row = lambda c: pl.BlockSpec((tm, c), lambda i: (i, 0))
    return pl.pallas_call(
        _outproj_kernel,
        grid=(rows // tm,),
        in_specs=[row(D_MODEL), row(W_MIX), row(W_MIX), _const_spec(w1.shape), _const_spec(w2.shape)],
        out_specs=row(D_MODEL),
        out_shape=jax.ShapeDtypeStruct((rows, D_MODEL), F32),
        compiler_params=_cparams("parallel"),
        name="outproj",
    )(x2d, ya, yb, w1, w2)


def _ffn_kernel(x_ref, g_ref, wu_ref, cw_ref, cb_ref, wd_ref, buf_ref, o_ref, st_ref,
                h_scr, acc_scr, sh_scr, carry_scr, *, tl, tf, l_real):
    ti = pl.program_id(1)
    x = x_ref[0]
    h_scr[...] = _rms(x, g_ref[...]).astype(BF16)
    acc_scr[...] = jnp.zeros(acc_scr.shape, F32)
    last_tile, last_row = (l_real - 1) // tl, (l_real - 1) % tl

    @pl.when(ti == 0)
    def _():
        carry_scr[SUBLANES - (CONV_W - 1):SUBLANES, :] = buf_ref[0]

    for f in range(D_FF // tf):
        z = []
        for part in range(2):
            c0 = part * D_FF + f * tf
            up = jnp.dot(h_scr[...], wu_ref[:, c0:c0 + tf], preferred_element_type=F32)
            sh_scr[0:SUBLANES, :] = carry_scr[:, c0:c0 + tf]
            sh_scr[SUBLANES:SUBLANES + tl, :] = up
            carry_scr[:, c0:c0 + tf] = up[tl - SUBLANES:tl, :]

            @pl.when(ti == last_tile)
            def _():
                st_ref[0, :, c0:c0 + tf] = sh_scr[SUBLANES + last_row - 1:SUBLANES + last_row + 1, :]

            z.append(cb_ref[:, c0:c0 + tf]
                     + cw_ref[0:1, c0:c0 + tf] * sh_scr[SUBLANES - 2:SUBLANES - 2 + tl, :]
                     + cw_ref[1:2, c0:c0 + tf] * sh_scr[SUBLANES - 1:SUBLANES - 1 + tl, :]
                     + cw_ref[2:3, c0:c0 + tf] * up)
        val, gate = z
        act = (gate * _sigmoid(gate) * val).astype(BF16)
        acc_scr[...] += jnp.dot(act, wd_ref[f * tf:(f + 1) * tf, :], preferred_element_type=F32)
    o_ref[0] = x + acc_scr[...]


def _ffn(x, g, wu, cw, cb, wd, buf, l_real):
    b, lp, _ = x.shape
    tl = _largest_tile(lp, TIME_TILE, SUBLANES)
    tf = 256
    assert l_real >= CONV_W - 1 and (l_real - 1) % tl >= 1
    return pl.pallas_call(
        functools.partial(_ffn_kernel, tl=tl, tf=tf, l_real=l_real),
        grid=(b, lp // tl),
        in_specs=[pl.BlockSpec((1, tl, D_MODEL), lambda i, j: (i, j, 0)),
                  _const_spec(g.shape), _const_spec(wu.shape), _const_spec(cw.shape), _const_spec(cb.shape),
                  _const_spec(wd.shape),
                  pl.BlockSpec((1, CONV_W - 1, 2 * D_FF), lambda i, j: (i, 0, 0))],
        out_specs=[pl.BlockSpec((1, tl, D_MODEL), lambda i, j: (i, j, 0)),
                   pl.BlockSpec((1, CONV_W - 1, 2 * D_FF), lambda i, j: (i, 0, 0))],
        out_shape=[jax.ShapeDtypeStruct((b, lp, D_MODEL), F32),
                   jax.ShapeDtypeStruct((b, CONV_W - 1, 2 * D_FF), F32)],
        scratch_shapes=[pltpu.VMEM((tl, D_MODEL), BF16), pltpu.VMEM((tl, D_MODEL), F32),
                        pltpu.VMEM((tl + SUBLANES, tf), F32), pltpu.VMEM((SUBLANES, 2 * D_FF), F32)],
        compiler_params=_cparams("parallel", "arbitrary"),
        name="ffn",
    )(x, g, wu, cw, cb, wd, buf)


def _s5_kernel(x_ref, g_ref, wbr_ref, wbi_ref, wcr_ref, wci_ref, are_ref, aim_ref, d_ref, wg_ref,
               s0r_ref, s0i_ref, o_ref, sor_ref, soi_ref, bu_scr, st_scr, *, tt, l_real):
    ti = pl.program_id(1)
    nb = S5_BATCH
    m = nb * tt
    x = x_ref[...].reshape(m, D_MODEL)
    u = _rms(x, g_ref[...])
    ub = u.astype(BF16)
    ntile = S5_STATE // LANES
    tpq = LANES // GC * P_C // LANES
    for q in range(D_MODEL // LANES):
        uq = ub[:, q * LANES:(q + 1) * LANES]
        br = jnp.dot(uq, wbr_ref[q], preferred_element_type=F32)
        bi = jnp.dot(uq, wbi_ref[q], preferred_element_type=F32)
        for c in range(tpq):
            bu_scr[q * tpq + c] = br[:, c * LANES:(c + 1) * LANES]
            bu_scr[ntile + q * tpq + c] = bi[:, c * LANES:(c + 1) * LANES]

    @pl.when(ti == 0)
    def _():
        for c in range(ntile):
            st_scr[c] = s0r_ref[:, c * LANES:(c + 1) * LANES]
            st_scr[ntile + c] = s0i_ref[:, c * LANES:(c + 1) * LANES]

    grp = 4
    for cb in range(ntile // grp):
        tiles = list(range(cb * grp, (cb + 1) * grp))
        a_re = [jnp.broadcast_to(are_ref[:, c * LANES:(c + 1) * LANES], (nb, LANES)) for c in tiles]
        a_im = [jnp.broadcast_to(aim_ref[:, c * LANES:(c + 1) * LANES], (nb, LANES)) for c in tiles]
        s_re = [st_scr[c] for c in tiles]
        s_im = [st_scr[ntile + c] for c in tiles]
        for t in range(tt):
            rows = pl.ds(t, nb, stride=tt)
            for i, c in enumerate(tiles):
                n_re = a_re[i] * s_re[i] - a_im[i] * s_im[i] + bu_scr[c, rows, :]
                n_im = a_re[i] * s_im[i] + a_im[i] * s_re[i] + bu_scr[ntile + c, rows, :]
                bu_scr[c, rows, :] = n_re
                bu_scr[ntile + c, rows, :] = n_im
                s_re[i], s_im[i] = n_re, n_im
        for i, c in enumerate(tiles):
            st_scr[c] = s_re[i]
            st_scr[ntile + c] = s_im[i]

    last_tile, last_row = (l_real - 1) // tt, (l_real - 1) % tt

    @pl.when(ti == last_tile)
    def _():
        rows = pl.ds(last_row, nb, stride=tt)
        for c in range(ntile):
            sor_ref[:, c * LANES:(c + 1) * LANES] = bu_scr[c, rows, :]
            soi_ref[:, c * LANES:(c + 1) * LANES] = bu_scr[ntile + c, rows, :]

    ys = []
    for q in range(D_MODEL // LANES):
        sr = jnp.concatenate([bu_scr[q * tpq + c] for c in range(tpq)], axis=1).astype(BF16)
        si = jnp.concatenate([bu_scr[ntile + q * tpq + c] for c in range(tpq)], axis=1).astype(BF16)
        ys.append(jnp.dot(sr, wcr_ref[q], preferred_element_type=F32)
                  + jnp.dot(si, wci_ref[q], preferred_element_type=F32))
    yc = jnp.concatenate(ys, axis=1) + d_ref[...] * u
    z = jnp.dot(_gelu(yc).astype(BF16), wg_ref[...], preferred_element_type=F32)
    mix = z[:, 0:D_MODEL] * _sigmoid(z[:, D_MODEL:])
    o_ref[...] = (x + mix).reshape(nb, tt, D_MODEL)


def _s5(x, g, sp, s0r, s0i, l_real):
    b, lp, _ = x.shape
    assert b % S5_BATCH == 0
    tt = _largest_tile(lp, 32, SUBLANES)
    m = S5_BATCH * tt
    consts = [g, sp["wbr"], sp["wbi"], sp["wcr"], sp["wci"], sp["are"], sp["aim"], sp["d"], sp["wglu"]]
    st = pl.BlockSpec((S5_BATCH, S5_STATE), lambda i, j: (i, 0))
    return pl.pallas_call(
        functools.partial(_s5_kernel, tt=tt, l_real=l_real),
        grid=(b // S5_BATCH, lp // tt),
        in_specs=[pl.BlockSpec((S5_BATCH, tt, D_MODEL), lambda i, j: (i, j, 0))]
                 + [_const_spec(c.shape) for c in consts] + [st, st],
        out_specs=[pl.BlockSpec((S5_BATCH, tt, D_MODEL), lambda i, j: (i, j, 0)), st, st],
        out_shape=[jax.ShapeDtypeStruct((b, lp, D_MODEL), F32),
                   jax.ShapeDtypeStruct((b, S5_STATE), F32), jax.ShapeDtypeStruct((b, S5_STATE), F32)],
        scratch_shapes=[pltpu.VMEM((2 * S5_STATE // LANES, m, LANES), F32),
                        pltpu.VMEM((2 * S5_STATE // LANES, S5_BATCH, LANES), F32)],
        compiler_params=_cparams("parallel", "arbitrary"),
        name="s5",
    )(x, *consts, s0r, s0i)


def _prep_params(p):
    j = 0
    head_of = jnp.arange(W_MIX) // HEAD_DIM
    e = (head_of[:, None] == head_of[None, :]).astype(BF16)
    w_in = p["w_in_e"][j]
    wf = jnp.zeros((D_MODEL, LANES), F32).at[:, :N_HEADS].set(w_in[:, A_COLS + 3 * W_MIX:])
    wb = jnp.concatenate([w_in[:, A_COLS:A_COLS + 3 * W_MIX], wf], axis=1).astype(BF16)
    fb = jnp.zeros((1, LANES), F32).at[0, :N_HEADS].set(p["b_fbias"][j])
    w2a = jnp.zeros((R_W + R_A, 2 * W_MIX), F32)
    w2a = w2a.at[:R_W, :W_MIX].set(p["a_w2"][j]).at[R_W:, W_MIX:].set(p["a_a2"][j])
    row = lambda a: a.reshape(1, -1).astype(F32)
    rw = dict(mu=row(p["a_mu"][j]), w0=row(p["a_w0"][j]), a0=row(p["a_a0"][j]), w2a=w2a.astype(BF16),
              g2=p["a_g2"][j].astype(BF16), kk=row(p["a_kk"][j]), ka=row(p["a_ka"][j]),
              rk=row(p["a_rk"][j]), lnw=row(p["a_ln_w"][j]), lnb=row(p["a_ln_b"][j]), e=e)
    hm = (jnp.arange(N_HEADS)[:, None] == head_of[None, :]).astype(BF16)
    l0 = dict(wa=w_in[:, :A_COLS].astype(BF16), wb=wb, fb=fb, e=e, hm=hm,
              qg=row(jnp.tile(p["b_qnorm"][j], N_HEADS)), kg=row(jnp.tile(p["b_knorm"][j], N_HEADS)),
              w_out_a=p["w_out_e"][j][:W_MIX].astype(BF16), w_out_b=p["w_out_e"][j][W_MIX:].astype(BF16))
    lam_re, lam_im = p["c_lam_re"][j].astype(F32), p["c_lam_im"][j].astype(F32)
    dt = jnp.exp(p["c_log_dt"][j].astype(F32))[:, None]
    mag = jnp.exp(lam_re * dt)
    ab_re, ab_im = mag * jnp.cos(lam_im * dt), mag * jnp.sin(lam_im * dt)
    den = lam_re * lam_re + lam_im * lam_im
    f_re = ((ab_re - 1.0) * lam_re + ab_im * lam_im) / den
    f_im = (ab_im * lam_re - (ab_re - 1.0) * lam_im) / den
    b_re, b_im = p["c_b_re"][j], p["c_b_im"][j]
    bb_re = f_re[..., None] * b_re - f_im[..., None] * b_im
    bb_im = f_re[..., None] * b_im + f_im[..., None] * b_re
    gpt = LANES // GC
    eye = jnp.eye(gpt, dtype=F32)

    def in_layout(bb):
        t = bb.reshape(G_C // gpt, gpt, P_C, GC).transpose(0, 1, 3, 2)
        return jnp.einsum("qgcp,gh->qgchp", t, eye).reshape(G_C // gpt, gpt * GC, gpt * P_C).astype(BF16)

    def out_layout(cc):
        t = cc.reshape(G_C // gpt, gpt, GC, P_C).transpose(0, 1, 3, 2)
        return jnp.einsum("qgpc,gh->qgphc", t, eye).reshape(G_C // gpt, gpt * P_C, gpt * GC).astype(BF16)

    s5 = dict(wbr=in_layout(bb_re), wbi=in_layout(bb_im), wcr=out_layout(p["c_c_re"][j]),
              wci=out_layout(-p["c_c_im"][j]), are=ab_re.reshape(1, S5_STATE), aim=ab_im.reshape(1, S5_STATE),
              d=row(p["c_d"][j]), wglu=p["w_glu"][j].astype(BF16))
    ffn = [dict(g=row(p["norm_ffn"][i]), wu=p["w_up"][i].astype(BF16), cw=p["conv_w"][i].astype(F32),
                cb=row(p["conv_b"][i]), wd=p["w_down"][i].astype(BF16)) for i in range(2)]
    return dict(l0=l0, rw=rw, s5=s5, ffn=ffn, g_mix=[row(p["norm_mix"][i]) for i in range(2)])


def _trunk(x, l_real, init, pp, past):
    b, lp, _ = x.shape
    l0, rows = pp["l0"], b * lp
    ua, q, kf, kb, vf, vb, lf = _inproj(x.reshape(rows, D_MODEL), pp["g_mix"][0], l0["wa"], l0["wb"],
                                        l0["e"], l0["qg"], l0["kg"], l0["fb"])
    r3 = lambda a: a.reshape(b, lp, -1)
    y_a, a_shift, a_wkv = _rwkv(r3(ua), init["a_shift"].reshape(b, 1, A_COLS), init["a_wkv"], pp["rw"], l_real)
    lf_row = jnp.transpose(r3(lf)[:, :, :N_HEADS], (0, 2, 1)).reshape(b * N_HEADS, lp)
    if past is None:
        lpad = -lp % LANES
        c_row = _cumsum_lanes(jnp.pad(lf_row, ((0, 0), (0, lpad))))[:, :lp].reshape(b, N_HEADS, lp)
        cq = jnp.pad(jnp.transpose(c_row, (0, 2, 1)), ((0, 0), (0, 0), (0, LANES - N_HEADS)))
        y_b = _attn_prompt(r3(q), r3(kb), r3(vb), cq, c_row, min(ATTN_TILE, lp))
    else:
        pk, pv, plf = past
        plen = pk.shape[1]
        plf_row = jnp.transpose(plf, (0, 2, 1)).reshape(b * N_HEADS, plen)
        c_all = _cumsum_lanes(jnp.concatenate([plf_row, jnp.pad(lf_row, ((0, 0), (0, -lp % LANES)))], axis=1))
        c_past = c_all[:, :plen].reshape(b, N_HEADS, plen)
        c_new = c_all[:, plen:plen + lp].reshape(b, N_HEADS, lp)
        cq_stack = jnp.broadcast_to(c_new.reshape(b, N_HEADS * lp, 1), (b, N_HEADS * lp, LANES))
        y_b = _attn_sample(r3(q), cq_stack, pk.reshape(b, plen, W_MIX), pv.reshape(b, plen, W_MIX), c_past,
                           r3(kb), r3(vb), c_new, l0["hm"])
    x1 = _outproj(x.reshape(rows, D_MODEL), y_a.reshape(rows, W_MIX), y_b.reshape(rows, W_MIX),
                  l0["w_out_a"], l0["w_out_b"]).reshape(b, lp, D_MODEL)
    f0 = pp["ffn"][0]
    x2, conv0 = _ffn(x1, f0["g"], f0["wu"], f0["cw"], f0["cb"], f0["wd"], init["ffn_conv"][0], l_real)
    x3, c_re, c_im = _s5(x2, pp["g_mix"][1], pp["s5"], init["c_re"].reshape(b, S5_STATE),
                         init["c_im"].reshape(b, S5_STATE), l_real)
    f1 = pp["ffn"][1]
    x4, conv1 = _ffn(x3, f1["g"], f1["wu"], f1["cw"], f1["cb"], f1["wd"], init["ffn_conv"][1], l_real)
    hd = lambda a: a.reshape(b, lp, N_HEADS, HEAD_DIM)[None, :, :l_real]
    states = (a_shift.reshape(1, b, A_COLS), a_wkv[None], hd(kf), hd(vf),
              r3(lf)[None, :, :l_real, :N_HEADS], c_re.reshape(1, b, G_C, P_C), c_im.reshape(1, b, G_C, P_C),
              jnp.stack([conv0, conv1], axis=0))
    return x4, states


def kernel(x_prompt, x_sample, state_a_shift, state_a_wkv, cache_b_k, cache_b_v, cache_b_logf, state_c_re, state_c_im, state_ffn_conv, meta, norm_mix, norm_ffn, w_in_e, a_mu, a_w0, a_w2, a_a0, a_a2, a_g2, a_kk, a_ka, a_rk, a_ln_w, a_ln_b, b_fbias, b_qnorm, b_knorm, w_out_e, c_lam_re, c_lam_im, c_log_dt, c_b_re, c_b_im, c_c_re, c_c_im, c_d, w_glu, w_up, conv_w, conv_b, w_down):
    pp = _prep_params(dict(
        norm_mix=norm_mix, norm_ffn=norm_ffn, w_in_e=w_in_e, a_mu=a_mu, a_w0=a_w0, a_w2=a_w2, a_a0=a_a0,
        a_a2=a_a2, a_g2=a_g2, a_kk=a_kk, a_ka=a_ka, a_rk=a_rk, a_ln_w=a_ln_w, a_ln_b=a_ln_b,
        b_fbias=b_fbias, b_qnorm=b_qnorm, b_knorm=b_knorm, w_out_e=w_out_e, c_lam_re=c_lam_re,
        c_lam_im=c_lam_im, c_log_dt=c_log_dt, c_b_re=c_b_re, c_b_im=c_b_im, c_c_re=c_c_re, c_c_im=c_c_im,
        c_d=c_d, w_glu=w_glu, w_up=w_up, conv_w=conv_w, conv_b=conv_b, w_down=w_down))
    b, seq, _ = x_prompt.shape
    n_meta = meta.shape[0]
    lr = n_meta + seq
    lp = -(-lr // WKV_CHUNK) * WKV_CHUNK
    xp = jnp.concatenate([jnp.broadcast_to(meta.astype(F32)[None], (b, n_meta, D_MODEL)), x_prompt,
                          jnp.zeros((b, lp - lr, D_MODEL), F32)], axis=1)
    init_p = dict(a_shift=jnp.zeros((b, A_COLS), F32),
                  a_wkv=jnp.zeros((b, N_HEADS, HEAD_DIM, HEAD_DIM), F32),
                  c_re=jnp.zeros((b, S5_STATE), F32), c_im=jnp.zeros((b, S5_STATE), F32),
                  ffn_conv=jnp.zeros((2, b, CONV_W - 1, 2 * D_FF), F32))
    yp, st_p = _trunk(xp, lr, init_p, pp, None)
    bs, ls, _ = x_sample.shape
    init_s = dict(a_shift=state_a_shift[0], a_wkv=state_a_wkv[0], c_re=state_c_re[0], c_im=state_c_im[0],
                  ffn_conv=state_ffn_conv)
    past = (cache_b_k[0], cache_b_v[0], cache_b_logf[0])
    ys, st_s = _trunk(x_sample, ls, init_s, pp, past)
    return (yp[:, n_meta:lr], ys, *st_p, *st_s)
```

```python
import functools
import math

import jax
import jax.numpy as jnp
from jax import lax
from jax.experimental import pallas as pl
from jax.experimental.pallas import tpu as pltpu

F32, BF16 = jnp.float32, jnp.bfloat16

D_MODEL = 1024
N_HEADS = 8
HEAD_DIM = 64
W_MIX = N_HEADS * HEAD_DIM
R_W, R_A, R_G = 64, 64, 128
A_COLS = 3 * W_MIX + R_W + R_A + R_G
B_PAD_COLS = 3 * W_MIX + 128
D_FF = 2816
CONV_W = 3
G_C, GC, P_C = 64, 16, 64
S5_STATE = G_C * P_C
EPS = 1e-6
LN_EPS = 64e-5
NEG = -1e30

LANES = 128
SUBLANES = 8
VMEM_LIMIT = 56 * 1024 * 1024
WKV_CHUNK = 64
WKV_SUB = 16
S5_BATCH = 8
TIME_TILE = 704
FFN_TILE = 352
ATTN_TILE = 768


def _cparams(*sem):
    return pltpu.CompilerParams(dimension_semantics=sem, vmem_limit_bytes=VMEM_LIMIT)


def _const_spec(shape):
    nd = len(shape)
    return pl.BlockSpec(shape, lambda *_: (0,) * nd, pipeline_mode=pl.Buffered(1))


def _largest_tile(n, cap, mult):
    best = None
    for t in range(mult, min(n, cap) + 1, mult):
        if n % t == 0:
            best = t
    assert best is not None, (n, cap, mult)
    return best


def _mm(a, b):
    return jnp.dot(a.astype(BF16), b.astype(BF16), preferred_element_type=F32)


def _mm_nt(a, b):
    return lax.dot_general(a.astype(BF16), b.astype(BF16), (((1,), (1,)), ((), ())),
                           preferred_element_type=F32)


def _mm_tn(a, b):
    return lax.dot_general(a.astype(BF16), b.astype(BF16), (((0,), (0,)), ((), ())),
                           preferred_element_type=F32)


def _split2(x):
    hi = x.astype(BF16)
    lo = (x - hi.astype(F32)).astype(BF16)
    return hi, lo


def _split3(x):
    hi = x.astype(BF16)
    r = x - hi.astype(F32)
    mid = r.astype(BF16)
    lo = (r - mid.astype(F32)).astype(BF16)
    return hi, mid, lo


def _mm_exact_rhs(x, m):
    hi, mid, lo = _split3(x)
    return (jnp.dot(hi, m, preferred_element_type=F32) + jnp.dot(mid, m, preferred_element_type=F32)
            + jnp.dot(lo, m, preferred_element_type=F32))


def _mm_exact_lhs(m, x):
    hi, mid, lo = _split3(x)
    return (jnp.dot(m, hi, preferred_element_type=F32) + jnp.dot(m, mid, preferred_element_type=F32)
            + jnp.dot(m, lo, preferred_element_type=F32))


def _headsum(x, e):
    hi, lo = _split2(x)
    return jnp.dot(hi, e, preferred_element_type=F32) + jnp.dot(lo, e, preferred_element_type=F32)


def _rms(x, g):
    return x * lax.rsqrt(jnp.mean(x * x, axis=-1, keepdims=True) + EPS) * g


def _sigmoid(x):
    return 1.0 / (1.0 + jnp.exp(-x))


def _log_sigmoid(x):
    return jnp.minimum(x, 0.0) - jnp.log(1.0 + jnp.exp(-jnp.abs(x)))


def _gelu(x):
    return 0.5 * x * (1.0 + jnp.tanh(math.sqrt(2.0 / math.pi) * (x + 0.044715 * (x * x * x))))


def _inproj_kernel(x_ref, g_ref, wa_ref, wb_ref, e_ref, qg_ref, kg_ref, fb_ref,
                   ua_ref, q_ref, kf_ref, kb_ref, vf_ref, vb_ref, lf_ref):
    h = _rms(x_ref[...], g_ref[...]).astype(BF16)
    ua_ref[...] = jnp.dot(h, wa_ref[...], preferred_element_type=F32)
    ub = jnp.dot(h, wb_ref[...], preferred_element_type=F32)
    q = ub[:, 0:W_MIX]
    k = ub[:, W_MIX:2 * W_MIX]
    v = ub[:, 2 * W_MIX:3 * W_MIX]
    f = ub[:, 3 * W_MIX:]
    e = e_ref[...]
    qn = q * lax.rsqrt(_headsum(q * q, e) * (1.0 / HEAD_DIM) + EPS) * qg_ref[...]
    kn = k * lax.rsqrt(_headsum(k * k, e) * (1.0 / HEAD_DIM) + EPS) * kg_ref[...]
    q_ref[...] = (qn * (HEAD_DIM ** -0.5)).astype(BF16)
    kf_ref[...] = kn
    kb_ref[...] = kn.astype(BF16)
    vf_ref[...] = v
    vb_ref[...] = v.astype(BF16)
    lf = _log_sigmoid(f + fb_ref[...])
    lane = lax.broadcasted_iota(jnp.int32, lf.shape, 1)
    lf_ref[...] = jnp.where(lane < N_HEADS, lf, 0.0)


def _inproj(x2d, g, wa, wb, e, qg, kg, fb):
    rows = x2d.shape[0]
    tm = _largest_tile(rows, 512, SUBLANES)
    row = lambda c: pl.BlockSpec((tm, c), lambda i: (i, 0))
    outs = [(A_COLS, F32), (W_MIX, BF16), (W_MIX, F32), (W_MIX, BF16), (W_MIX, F32), (W_MIX, BF16),
            (LANES, F32)]
    return pl.pallas_call(
        _inproj_kernel,
        grid=(rows // tm,),
        in_specs=[row(D_MODEL), _const_spec(g.shape), _const_spec(wa.shape), _const_spec(wb.shape),
                  _const_spec(e.shape), _const_spec(qg.shape), _const_spec(kg.shape), _const_spec(fb.shape)],
        out_specs=[row(c) for c, _ in outs],
        out_shape=[jax.ShapeDtypeStruct((rows, c), dt) for c, dt in outs],
        compiler_params=_cparams("parallel"),
        name="inproj",
    )(x2d, g, wa, wb, e, qg, kg, fb)


def _cumsum_kernel(x_ref, tri_ref, o_ref, *, nblk):
    tri = tri_ref[...]
    carry = jnp.zeros((x_ref.shape[0], 1), F32)
    for j in range(nblk):
        cs = _mm_exact_rhs(x_ref[:, j * LANES:(j + 1) * LANES], tri) + carry
        o_ref[:, j * LANES:(j + 1) * LANES] = cs
        carry = cs[:, LANES - 1:LANES]


def _cumsum_lanes(x):
    rows, n = x.shape
    assert n % LANES == 0
    tri = (jnp.arange(LANES)[:, None] <= jnp.arange(LANES)[None, :]).astype(BF16)
    return pl.pallas_call(
        functools.partial(_cumsum_kernel, nblk=n // LANES),
        grid=(1,),
        in_specs=[_const_spec(x.shape), _const_spec(tri.shape)],
        out_specs=_const_spec(x.shape),
        out_shape=jax.ShapeDtypeStruct(x.shape, F32),
        compiler_params=_cparams("arbitrary"),
        name="cumsum",
    )(x, tri)


def _attn_prompt_kernel(q_ref, k_ref, v_ref, cq_ref, cr_ref, o_ref, *, bounds):
    p = pl.program_id(1)
    lane = lax.broadcasted_iota(jnp.int32, (1, LANES), 1)
    head_lanes = (lane < HEAD_DIM, lane >= HEAD_DIM)
    for qi in range(len(bounds) - 1):
        r0, r1 = bounds[qi], bounds[qi + 1]
        q2 = q_ref[0, r0:r1, :]
        cqt = cq_ref[0, r0:r1, :]
        lane_q = lax.broadcasted_iota(jnp.int32, cqt.shape, 1)
        res = []
        for hh in range(2):
            h = 2 * p + hh
            qm = jnp.where(head_lanes[hh], q2, jnp.zeros_like(q2))
            cq = jnp.sum(jnp.where(lane_q == h, cqt, 0.0), axis=-1, keepdims=True)
            ck_all = cr_ref[0, pl.ds(h, 1), :]
            m = jnp.full((r1 - r0, 1), NEG, F32)
            l = jnp.zeros((r1 - r0, 1), F32)
            acc = jnp.zeros((r1 - r0, LANES), F32)
            for ki in range(qi + 1):
                c0, c1 = bounds[ki], bounds[ki + 1]
                s = lax.dot_general(qm, k_ref[0, c0:c1, :], (((1,), (1,)), ((), ())),
                                    preferred_element_type=F32)
                s = s + cq - ck_all[:, c0:c1]
                if ki == qi:
                    ri = lax.broadcasted_iota(jnp.int32, s.shape, 0)
                    ci = lax.broadcasted_iota(jnp.int32, s.shape, 1)
                    s = jnp.where(ri >= ci, s, NEG)
                m_new = jnp.maximum(m, jnp.max(s, axis=-1, keepdims=True))
                alpha = jnp.exp(m - m_new)
                pe = jnp.exp(s - m_new)
                l = alpha * l + jnp.sum(pe, axis=-1, keepdims=True)
                acc = alpha * acc + jnp.dot(pe.astype(BF16), v_ref[0, c0:c1, :],
                                            preferred_element_type=F32)
                m = m_new
            res.append(acc / l)
        o_ref[0, r0:r1, :] = jnp.where(head_lanes[0], res[0], res[1]).astype(BF16)


def _attn_prompt(q, k, v, cq, cr, tile):
    b, lp, _ = q.shape
    bounds = tuple(range(0, lp, tile)) + (lp,)
    col = pl.BlockSpec((1, lp, LANES), lambda i, p: (i, 0, p))
    return pl.pallas_call(
        functools.partial(_attn_prompt_kernel, bounds=bounds),
        grid=(b, N_HEADS // 2),
        in_specs=[col, col, col,
                  pl.BlockSpec((1, lp, LANES), lambda i, p: (i, 0, 0)),
                  pl.BlockSpec((1, N_HEADS, lp), lambda i, p: (i, 0, 0))],
        out_specs=col,
        out_shape=jax.ShapeDtypeStruct((b, lp, W_MIX), BF16),
        compiler_params=_cparams("parallel", "arbitrary"),
        name="attn_prompt",
    )(q, k, v, cq, cr)


def _attn_sample_kernel(q_ref, cq_ref, kp_ref, vp_ref, cp_ref, kn_ref, vn_ref, cn_ref, hm_ref, o_ref,
                        qx_scr, m_scr, l_scr, acc_scr, *, nkp, ls):
    j = pl.program_id(1)
    rows = N_HEADS * ls

    @pl.when(j == 0)
    def _():
        q = q_ref[0]
        qx = jnp.broadcast_to(q[None], (N_HEADS, ls, W_MIX)) * hm_ref[...][:, None, :]
        qx_scr[...] = qx.reshape(rows, W_MIX)
        m_scr[...] = jnp.full(m_scr.shape, NEG, F32)
        l_scr[...] = jnp.zeros(l_scr.shape, F32)
        acc_scr[...] = jnp.zeros(acc_scr.shape, F32)

    def update(k, v, ck, causal):
        tk = k.shape[0]
        s = lax.dot_general(qx_scr[...], k, (((1,), (1,)), ((), ())), preferred_element_type=F32)
        ckx = jnp.broadcast_to(ck[:, None, :], (N_HEADS, ls, tk)).reshape(rows, tk)
        s = s + cq_ref[0][:, 0:1] - ckx
        if causal:
            ri = lax.broadcasted_iota(jnp.int32, (N_HEADS, ls, tk), 1).reshape(rows, tk)
            ci = lax.broadcasted_iota(jnp.int32, (rows, tk), 1)
            s = jnp.where(ri >= ci, s, NEG)
        m = m_scr[...]
        m_new = jnp.maximum(m, jnp.max(s, axis=-1, keepdims=True))
        alpha = jnp.exp(m - m_new)
        pe = jnp.exp(s - m_new)
        l_scr[...] = alpha * l_scr[...] + jnp.sum(pe, axis=-1, keepdims=True)
        acc_scr[...] = alpha * acc_scr[...] + jnp.dot(pe.astype(BF16), v, preferred_element_type=F32)
        m_scr[...] = m_new

    @pl.when(j < nkp)
    def _():
        update(kp_ref[0].astype(BF16), vp_ref[0].astype(BF16), cp_ref[0], False)

    @pl.when(j == nkp)
    def _():
        update(kn_ref[0], vn_ref[0], cn_ref[0], True)
        o = (acc_scr[...] / l_scr[...]).reshape(N_HEADS, ls, W_MIX) * hm_ref[...].astype(F32)[:, None, :]
        o_ref[0] = jnp.sum(o, axis=0).astype(BF16)


def _attn_sample(q, cq_stack, kp, vp, cp, kn, vn, cn, hm):
    b, ls, _ = q.shape
    past = kp.shape[1]
    tkp = _largest_tile(past, 1024, LANES)
    nkp = past // tkp
    rows = N_HEADS * ls
    pidx = lambda i, j: (i, jnp.minimum(j, nkp - 1), 0)
    return pl.pallas_call(
        functools.partial(_attn_sample_kernel, nkp=nkp, ls=ls),
        grid=(b, nkp + 1),
        in_specs=[pl.BlockSpec((1, ls, W_MIX), lambda i, j: (i, 0, 0)),
                  pl.BlockSpec((1, rows, LANES), lambda i, j: (i, 0, 0)),
                  pl.BlockSpec((1, tkp, W_MIX), pidx),
                  pl.BlockSpec((1, tkp, W_MIX), pidx),
                  pl.BlockSpec((1, N_HEADS, tkp), lambda i, j: (i, 0, jnp.minimum(j, nkp - 1))),
                  pl.BlockSpec((1, ls, W_MIX), lambda i, j: (i, 0, 0)),
                  pl.BlockSpec((1, ls, W_MIX), lambda i, j: (i, 0, 0)),
                  pl.BlockSpec((1, N_HEADS, ls), lambda i, j: (i, 0, 0)),
                  _const_spec(hm.shape)],
        out_specs=pl.BlockSpec((1, ls, W_MIX), lambda i, j: (i, 0, 0)),
        out_shape=jax.ShapeDtypeStruct((b, ls, W_MIX), BF16),
        scratch_shapes=[pltpu.VMEM((rows, W_MIX), BF16), pltpu.VMEM((rows, 1), F32),
                        pltpu.VMEM((rows, 1), F32), pltpu.VMEM((rows, W_MIX), F32)],
        compiler_params=_cparams("parallel", "arbitrary"),
        name="attn_sample",
    )(q, cq_stack, kp, vp, cp, kn, vn, cn, hm)


def _wkv_chunk_matrices(tops, bots, vs, t):
    hd = range(len(tops))
    ri = lax.broadcasted_iota(jnp.int32, (t, t), 0)
    ci = lax.broadcasted_iota(jnp.int32, (t, t), 1)
    eye = (ri == ci).astype(F32)
    r2 = lax.broadcasted_iota(jnp.int32, (t, 2 * t), 0)
    c2 = lax.broadcasted_iota(jnp.int32, (t, 2 * t), 1)
    c2 = jnp.where(c2 >= t, c2 - t, c2)
    same = (ri // WKV_SUB) == (ci // WKV_SUB)
    aa = [_mm_nt(tops[h], bots[h]) for h in hd]
    n = [jnp.where(ri > ci, aa[h][:t, :t], 0.0) for h in hd]
    a_ak = [jnp.where(ri > ci, aa[h][:t, t:], 0.0) for h in hd]
    a_r = [jnp.where(r2 >= c2, aa[h][t:, :], 0.0) for h in hd]
    av = [_mm(a_ak[h], vs[h]) for h in hd]
    d = [jnp.where(same, n[h], 0.0) for h in hd]
    lo = [n[h] - d[h] for h in hd]
    d2 = [_mm(d[h], d[h]) for h in hd]
    x = [_mm(eye - d[h], eye + d2[h]) for h in hd]
    d4 = [_mm(d2[h], d2[h]) for h in hd]
    x = [_mm(x[h], eye + d4[h]) for h in hd]
    d8 = [_mm(d4[h], d4[h]) for h in hd]
    x = [_mm(x[h], eye + d8[h]) for h in hd]
    mb = [_mm(x[h], lo[h]) for h in hd]
    mb2 = [_mm(mb[h], mb[h]) for h in hd]
    xx = [_mm(eye + mb2[h], x[h]) for h in hd]
    ginv = [_mm(eye - mb[h], xx[h]) for h in hd]
    return ginv, a_r, av


def _rwkv_kernel(u_ref, sh_ref, s0_ref, mu_ref, w0_ref, a0_ref, w2a_ref, g2_ref, kk_ref, ka_ref,
                 rk_ref, lnw_ref, lnb_ref, e_ref, tril_ref,
                 y_ref, sho_ref, so_ref,
                 ubuf, carry, s_scr, r_s, wl_s, k_s, v_s, kk_s, b_s, y_s, bon_s, g_s,
                 top_s, bk_s, vh_s, av_s, gi_s, ar_s, wt_s, *, tl, t, l_real, nt):
    ti = pl.program_id(1)

    @pl.when(ti == 0)
    def _():
        carry[...] = jnp.broadcast_to(sh_ref[0], carry.shape)
        s_scr[...] = s0_ref[0]

    u = u_ref[0]
    ubuf[0:SUBLANES, :] = carry[...]
    ubuf[SUBLANES:SUBLANES + tl, :] = u
    u_prev = ubuf[SUBLANES - 1:SUBLANES - 1 + tl, :]
    carry[...] = u[tl - SUBLANES:tl, :]
    last_tile, last_row = (l_real - 1) // tl, (l_real - 1) % tl

    @pl.when(ti == last_tile)
    def _():
        sho_ref[0] = u[last_row:last_row + 1, :]

    um = u + (u_prev - u) * mu_ref[...]
    r = um[:, 0:W_MIX]
    k = um[:, W_MIX:2 * W_MIX]
    v = um[:, 2 * W_MIX:3 * W_MIX]
    wa = um[:, 3 * W_MIX:3 * W_MIX + R_W + R_A]
    gd = um[:, 3 * W_MIX + R_W + R_A:]
    lane = lax.broadcasted_iota(jnp.int32, wa.shape, 1)
    za = _mm(jnp.where(lane < R_W, jnp.tanh(wa), wa), w2a_ref[...])
    wl = -jnp.exp(_log_sigmoid(w0_ref[...] + za[:, 0:W_MIX]) - 0.5)
    a = _sigmoid(a0_ref[...] + za[:, W_MIX:])
    g_s[...] = _mm(_sigmoid(gd), g2_ref[...])
    e = e_ref[...]
    kk = k * kk_ref[...]
    kkn = kk * lax.rsqrt(_headsum(kk * kk, e) + 1e-12)
    k2 = k * (1.0 + (a - 1.0) * ka_ref[...])
    bon_s[...] = _headsum(r * k2 * rk_ref[...], e) * v
    valid = (ti * tl + lax.broadcasted_iota(jnp.int32, (tl, 1), 0)) < l_real
    r_s[...] = r
    wl_s[...] = jnp.where(valid, wl, 0.0)
    k_s[...] = jnp.where(valid, k2, 0.0)
    v_s[...] = jnp.where(valid, v, 0.0)
    kk_s[...] = jnp.where(valid, kkn, 0.0)
    b_s[...] = jnp.where(valid, kkn * a, 0.0)

    tril = tril_ref[...]

    hd = range(N_HEADS)
    hsl = [slice(h * HEAD_DIM, (h + 1) * HEAD_DIM) for h in hd]

    def chunk_matrices(c, carry_):
        rows = pl.ds(pl.multiple_of(c * t, t), t)
        wlc = wl_s[rows, :]
        hi, mid, lo = _split3(wlc)
        cw = (jnp.dot(tril, hi, preferred_element_type=F32) + jnp.dot(tril, mid, preferred_element_type=F32)
              + jnp.dot(tril, lo, preferred_element_type=F32))
        w_inc = jnp.exp(cw)
        w_inv = jnp.exp(-cw)
        w_prev = jnp.exp(cw - wlc)
        w_t = w_inc[t - 1:t, :]
        top = jnp.concatenate([kk_s[rows, :] * w_prev, r_s[rows, :] * w_inc], axis=0).astype(BF16)
        bot = jnp.concatenate([b_s[rows, :] * w_inv, k_s[rows, :] * w_inv], axis=0)
        bk = (bot * w_t).astype(BF16)
        bot = bot.astype(BF16)
        vc = v_s[rows, :].astype(BF16)
        tops = [top[:, hsl[h]] for h in hd]
        vs = [vc[:, hsl[h]] for h in hd]
        ginv, a_r, av = _wkv_chunk_matrices(tops, [bot[:, hsl[h]] for h in hd], vs, t)
        for h in hd:
            top_s[c, h] = tops[h]
            bk_s[c, h] = bk[:, hsl[h]]
            vh_s[c, h] = vs[h]
            av_s[c, h] = av[h]
            gi_s[c, h] = ginv[h].astype(BF16)
            ar_s[c, h] = a_r[h].astype(BF16)
        wt_s[c] = jnp.broadcast_to(w_t, (SUBLANES, W_MIX))
        return carry_

    def chunk_state(c, carry_):
        rows = pl.ds(pl.multiple_of(c * t, t), t)
        pp = [_mm_nt(top_s[c, h], s_scr[h]) for h in hd]
        u = [_mm(gi_s[c, h], -(pp[h][:t] + av_s[c, h])) for h in hd]
        uv = [jnp.concatenate([u[h].astype(BF16), vh_s[c, h]], axis=0) for h in hd]
        y = [pp[h][t:] + _mm(ar_s[c, h], uv[h]) for h in hd]
        w_t = wt_s[c]
        for h in hd:
            s_scr[h] = s_scr[h] * w_t[0:1, hsl[h]] + _mm_tn(uv[h], bk_s[c, h])
        y_s[rows, :] = jnp.concatenate(y, axis=1)
        return carry_

    lax.fori_loop(0, tl // t, chunk_matrices, 0)
    lax.fori_loop(0, tl // t, chunk_state, 0)

    y = y_s[...]
    yc = y - _headsum(y, e) * (1.0 / HEAD_DIM)
    var = _headsum(yc * yc, e) * (1.0 / HEAD_DIM)
    yn = yc * lax.rsqrt(var + LN_EPS)
    y_ref[0] = ((yn * lnw_ref[...] + lnb_ref[...] + bon_s[...]) * g_s[...]).astype(BF16)

    @pl.when(ti == nt - 1)
    def _():
        so_ref[0] = s_scr[...]


def _rwkv(ua, shift0, wkv0, prm, l_real):
    b, lp, _ = ua.shape
    t = min(WKV_CHUNK, lp)
    tl = _largest_tile(lp, TIME_TILE, t)
    nt = lp // tl
    tril = (jnp.arange(t)[:, None] >= jnp.arange(t)[None, :]).astype(BF16)
    consts = [prm["mu"], prm["w0"], prm["a0"], prm["w2a"], prm["g2"], prm["kk"], prm["ka"], prm["rk"],
              prm["lnw"], prm["lnb"], prm["e"], tril]
    big = lambda: pltpu.VMEM((tl, W_MIX), F32)
    nc = tl // t
    per_head = lambda r, c, dt: pltpu.VMEM((nc, N_HEADS, r, c), dt)
    chunk_scratch = [per_head(2 * t, HEAD_DIM, BF16), per_head(2 * t, HEAD_DIM, BF16),
                     per_head(t, HEAD_DIM, BF16), per_head(t, HEAD_DIM, F32), per_head(t, t, BF16),
                     per_head(t, 2 * t, BF16), pltpu.VMEM((nc, SUBLANES, W_MIX), F32)]
    return pl.pallas_call(
        functools.partial(_rwkv_kernel, tl=tl, t=t, l_real=l_real, nt=nt),
        grid=(b, nt),
        in_specs=[pl.BlockSpec((1, tl, A_COLS), lambda i, j: (i, j, 0)),
                  pl.BlockSpec((1, 1, A_COLS), lambda i, j: (i, 0, 0)),
                  pl.BlockSpec((1, N_HEADS, HEAD_DIM, HEAD_DIM), lambda i, j: (i, 0, 0, 0))]
                 + [_const_spec(c.shape) for c in consts],
        out_specs=[pl.BlockSpec((1, tl, W_MIX), lambda i, j: (i, j, 0)),
                   pl.BlockSpec((1, 1, A_COLS), lambda i, j: (i, 0, 0)),
                   pl.BlockSpec((1, N_HEADS, HEAD_DIM, HEAD_DIM), lambda i, j: (i, 0, 0, 0))],
        out_shape=[jax.ShapeDtypeStruct((b, lp, W_MIX), BF16),
                   jax.ShapeDtypeStruct((b, 1, A_COLS), F32),
                   jax.ShapeDtypeStruct((b, N_HEADS, HEAD_DIM, HEAD_DIM), F32)],
        scratch_shapes=[pltpu.VMEM((tl + SUBLANES, A_COLS), F32), pltpu.VMEM((SUBLANES, A_COLS), F32),
                        pltpu.VMEM((N_HEADS, HEAD_DIM, HEAD_DIM), F32)] + [big() for _ in range(9)]
                       + chunk_scratch,
        compiler_params=_cparams("parallel", "arbitrary"),
        name="rwkv",
    )(ua, shift0, wkv0, *consts)


def _outproj_kernel(x_ref, ya_ref, yb_ref, w1_ref, w2_ref, o_ref):
    o_ref[...] = (x_ref[...] + jnp.dot(ya_ref[...], w1_ref[...], preferred_element_type=F32)
                  + jnp.dot(yb_ref[...], w2_ref[...], preferred_element_type=F32))


def _outproj(x2d, ya, yb, w1, w2):
    rows = x2d.shape[0]
    tm = _largest_tile(rows, 1024, SUBLANES)
    row = lambda c: pl.BlockSpec((tm, c), lambda i: (i, 0))
    return pl.pallas_call(
        _outproj_kernel,
        grid=(rows // tm,),
        in_specs=[row(D_MODEL), row(W_MIX), row(W_MIX), _const_spec(w1.shape), _const_spec(w2.shape)],
        out_specs=row(D_MODEL),
        out_shape=jax.ShapeDtypeStruct((rows, D_MODEL), F32),
        compiler_params=_cparams("parallel"),
        name="outproj",
    )(x2d, ya, yb, w1, w2)


def _ffn_kernel(x_ref, g_ref, wu_ref, cw_ref, cb_ref, wd_ref, buf_ref, o_ref, st_ref,
                up_scr, act_scr, carry_scr, *, tl, tf, tu, l_real):
    ti = pl.program_id(1)
    x = x_ref[0]
    h = _rms(x, g_ref[...]).astype(BF16)
    last_tile, last_row = (l_real - 1) // tl, (l_real - 1) % tl

    @pl.when(ti == 0)
    def _():
        carry_scr[SUBLANES - (CONV_W - 1):SUBLANES, :] = buf_ref[0]

    up_scr[0:SUBLANES, :] = carry_scr[...]
    for c in range(2 * D_FF // tu):
        up_scr[SUBLANES:SUBLANES + tl, c * tu:(c + 1) * tu] = jnp.dot(
            h, wu_ref[:, c * tu:(c + 1) * tu], preferred_element_type=F32)
    carry_scr[...] = up_scr[tl:tl + SUBLANES, :]

    @pl.when(ti == last_tile)
    def _():
        st_ref[0] = up_scr[SUBLANES + last_row - 1:SUBLANES + last_row + 1, :]

    for f in range(D_FF // tf):
        z = []
        for part in range(2):
            cols = slice(part * D_FF + f * tf, part * D_FF + (f + 1) * tf)
            z.append(cb_ref[:, cols]
                     + cw_ref[0:1, cols] * up_scr[SUBLANES - 2:SUBLANES - 2 + tl, cols]
                     + cw_ref[1:2, cols] * up_scr[SUBLANES - 1:SUBLANES - 1 + tl, cols]
                     + cw_ref[2:3, cols] * up_scr[SUBLANES:SUBLANES + tl, cols])
        val, gate = z
        act_scr[:, f * tf:(f + 1) * tf] = (gate * _sigmoid(gate) * val).astype(BF16)
    o_ref[0] = x + jnp.dot(act_scr[...], wd_ref[...], preferred_element_type=F32)


def _ffn(x, g, wu, cw, cb, wd, buf, l_real):
    b, lp, _ = x.shape
    tl = _largest_tile(lp, FFN_TILE, SUBLANES)
    tf, tu = 256, 512
    assert l_real >= CONV_W - 1 and (l_real - 1) % tl >= 1
    return pl.pallas_call(
        functools.partial(_ffn_kernel, tl=tl, tf=tf, tu=tu, l_real=l_real),
        grid=(b, lp // tl),
        in_specs=[pl.BlockSpec((1, tl, D_MODEL), lambda i, j: (i, j, 0)),
                  _const_spec(g.shape), _const_spec(wu.shape), _const_spec(cw.shape), _const_spec(cb.shape),
                  _const_spec(wd.shape),
                  pl.BlockSpec((1, CONV_W - 1, 2 * D_FF), lambda i, j: (i, 0, 0))],
        out_specs=[pl.BlockSpec((1, tl, D_MODEL), lambda i, j: (i, j, 0)),
                   pl.BlockSpec((1, CONV_W - 1, 2 * D_FF), lambda i, j: (i, 0, 0))],
        out_shape=[jax.ShapeDtypeStruct((b, lp, D_MODEL), F32),
                   jax.ShapeDtypeStruct((b, CONV_W - 1, 2 * D_FF), F32)],
        scratch_shapes=[pltpu.VMEM((tl + SUBLANES, 2 * D_FF), F32), pltpu.VMEM((tl, D_FF), BF16),
                        pltpu.VMEM((SUBLANES, 2 * D_FF), F32)],
        compiler_params=_cparams("parallel", "arbitrary"),
        name="ffn",
    )(x, g, wu, cw, cb, wd, buf)


def _s5_kernel(x_ref, g_ref, wbr_ref, wbi_ref, wcr_ref, wci_ref, are_ref, aim_ref, d_ref, wg_ref,
               perm_ref, permt_ref, s0r_ref, s0i_ref, o_ref, sor_ref, soi_ref, bu_scr, st_scr, *, tt, l_real):
    ti = pl.program_id(1)
    nb = S5_BATCH
    m = nb * tt
    x = x_ref[...].reshape(m, D_MODEL)
    perm = perm_ref[...]
    u_hi, u_lo = _split2(_rms(x, g_ref[...]))
    ub_f = jnp.dot(perm, u_hi, preferred_element_type=F32)
    ub = ub_f.astype(BF16)
    u = ub_f + jnp.dot(perm, u_lo, preferred_element_type=F32)
    ntile = S5_STATE // LANES
    tpq = LANES // GC * P_C // LANES
    for q in range(D_MODEL // LANES):
        uq = ub[:, q * LANES:(q + 1) * LANES]
        br = jnp.dot(uq, wbr_ref[q], preferred_element_type=F32)
        bi = jnp.dot(uq, wbi_ref[q], preferred_element_type=F32)
        for c in range(tpq):
            bu_scr[q * tpq + c] = br[:, c * LANES:(c + 1) * LANES]
            bu_scr[ntile + q * tpq + c] = bi[:, c * LANES:(c + 1) * LANES]

    @pl.when(ti == 0)
    def _():
        for c in range(ntile):
            st_scr[c] = s0r_ref[:, c * LANES:(c + 1) * LANES]
            st_scr[ntile + c] = s0i_ref[:, c * LANES:(c + 1) * LANES]

    grp = 4
    for cb in range(ntile // grp):
        tiles = list(range(cb * grp, (cb + 1) * grp))
        a_re = [jnp.broadcast_to(are_ref[:, c * LANES:(c + 1) * LANES], (nb, LANES)) for c in tiles]
        a_im = [jnp.broadcast_to(aim_ref[:, c * LANES:(c + 1) * LANES], (nb, LANES)) for c in tiles]
        s_re = [st_scr[c] for c in tiles]
        s_im = [st_scr[ntile + c] for c in tiles]
        for t in range(tt):
            rows = slice(t * nb, (t + 1) * nb)
            for i, c in enumerate(tiles):
                n_re = a_re[i] * s_re[i] - a_im[i] * s_im[i] + bu_scr[c, rows, :]
                n_im = a_re[i] * s_im[i] + a_im[i] * s_re[i] + bu_scr[ntile + c, rows, :]
                bu_scr[c, rows, :] = n_re
                bu_scr[ntile + c, rows, :] = n_im
                s_re[i], s_im[i] = n_re, n_im
        for i, c in enumerate(tiles):
            st_scr[c] = s_re[i]
            st_scr[ntile + c] = s_im[i]

    last_tile, last_row = (l_real - 1) // tt, (l_real - 1) % tt

    @pl.when(ti == last_tile)
    def _():
        rows = slice(last_row * nb, (last_row + 1) * nb)
        for c in range(ntile):
            sor_ref[:, c * LANES:(c + 1) * LANES] = bu_scr[c, rows, :]
            soi_ref[:, c * LANES:(c + 1) * LANES] = bu_scr[ntile + c, rows, :]

    ys = []
    for q in range(D_MODEL // LANES):
        sr = jnp.concatenate([bu_scr[q * tpq + c] for c in range(tpq)], axis=1).astype(BF16)
        si = jnp.concatenate([bu_scr[ntile + q * tpq + c] for c in range(tpq)], axis=1).astype(BF16)
        ys.append(jnp.dot(sr, wcr_ref[q], preferred_element_type=F32)
                  + jnp.dot(si, wci_ref[q], preferred_element_type=F32))
    yc = jnp.concatenate(ys, axis=1) + d_ref[...] * u
    z = jnp.dot(_gelu(yc).astype(BF16), wg_ref[...], preferred_element_type=F32)
    mix = z[:, 0:D_MODEL] * _sigmoid(z[:, D_MODEL:])
    o_ref[...] = (x + _mm_exact_lhs(permt_ref[...], mix)).reshape(nb, tt, D_MODEL)


def _s5(x, g, sp, s0r, s0i, l_real):
    b, lp, _ = x.shape
    assert b % S5_BATCH == 0
    tt = _largest_tile(lp, 32, SUBLANES)
    m = S5_BATCH * tt
    dst = jnp.arange(m)
    src = (dst % S5_BATCH) * tt + dst // S5_BATCH
    perm = (src[:, None] == jnp.arange(m)[None, :]).astype(BF16)
    consts = [g, sp["wbr"], sp["wbi"], sp["wcr"], sp["wci"], sp["are"], sp["aim"], sp["d"], sp["wglu"],
              perm, perm.T]
    st = pl.BlockSpec((S5_BATCH, S5_STATE), lambda i, j: (i, 0))
    return pl.pallas_call(
        functools.partial(_s5_kernel, tt=tt, l_real=l_real),
        grid=(b // S5_BATCH, lp // tt),
        in_specs=[pl.BlockSpec((S5_BATCH, tt, D_MODEL), lambda i, j: (i, j, 0))]
                 + [_const_spec(c.shape) for c in consts] + [st, st],
        out_specs=[pl.BlockSpec((S5_BATCH, tt, D_MODEL), lambda i, j: (i, j, 0)), st, st],
        out_shape=[jax.ShapeDtypeStruct((b, lp, D_MODEL), F32),
                   jax.ShapeDtypeStruct((b, S5_STATE), F32), jax.ShapeDtypeStruct((b, S5_STATE), F32)],
        scratch_shapes=[pltpu.VMEM((2 * S5_STATE // LANES, m, LANES), F32),
                        pltpu.VMEM((2 * S5_STATE // LANES, S5_BATCH, LANES), F32)],
        compiler_params=_cparams("parallel", "arbitrary"),
        name="s5",
    )(x, *consts, s0r, s0i)


def _prep_params(p):
    j = 0
    head_of = jnp.arange(W_MIX) // HEAD_DIM
    e = (head_of[:, None] == head_of[None, :]).astype(BF16)
    w_in = p["w_in_e"][j]
    wf = jnp.zeros((D_MODEL, LANES), F32).at[:, :N_HEADS].set(w_in[:, A_COLS + 3 * W_MIX:])
    wb = jnp.concatenate([w_in[:, A_COLS:A_COLS + 3 * W_MIX], wf], axis=1).astype(BF16)
    fb = jnp.zeros((1, LANES), F32).at[0, :N_HEADS].set(p["b_fbias"][j])
    w2a = jnp.zeros((R_W + R_A, 2 * W_MIX), F32)
    w2a = w2a.at[:R_W, :W_MIX].set(p["a_w2"][j]).at[R_W:, W_MIX:].set(p["a_a2"][j])
    row = lambda a: a.reshape(1, -1).astype(F32)
    rw = dict(mu=row(p["a_mu"][j]), w0=row(p["a_w0"][j]), a0=row(p["a_a0"][j]), w2a=w2a.astype(BF16),
              g2=p["a_g2"][j].astype(BF16), kk=row(p["a_kk"][j]), ka=row(p["a_ka"][j]),
              rk=row(p["a_rk"][j]), lnw=row(p["a_ln_w"][j]), lnb=row(p["a_ln_b"][j]), e=e)
    hm = (jnp.arange(N_HEADS)[:, None] == head_of[None, :]).astype(BF16)
    l0 = dict(wa=w_in[:, :A_COLS].astype(BF16), wb=wb, fb=fb, e=e, hm=hm,
              qg=row(jnp.tile(p["b_qnorm"][j], N_HEADS)), kg=row(jnp.tile(p["b_knorm"][j], N_HEADS)),
              w_out_a=p["w_out_e"][j][:W_MIX].astype(BF16), w_out_b=p["w_out_e"][j][W_MIX:].astype(BF16))
    lam_re, lam_im = p["c_lam_re"][j].astype(F32), p["c_lam_im"][j].astype(F32)
    dt = jnp.exp(p["c_log_dt"][j].astype(F32))[:, None]
    mag = jnp.exp(lam_re * dt)
    ab_re, ab_im = mag * jnp.cos(lam_im * dt), mag * jnp.sin(lam_im * dt)
    den = lam_re * lam_re + lam_im * lam_im
    f_re = ((ab_re - 1.0) * lam_re + ab_im * lam_im) / den
    f_im = (ab_im * lam_re - (ab_re - 1.0) * lam_im) / den
    b_re, b_im = p["c_b_re"][j], p["c_b_im"][j]
    bb_re = f_re[..., None] * b_re - f_im[..., None] * b_im
    bb_im = f_re[..., None] * b_im + f_im[..., None] * b_re
    gpt = LANES // GC
    eye = jnp.eye(gpt, dtype=F32)

    def in_layout(bb):
        t = bb.reshape(G_C // gpt, gpt, P_C, GC).transpose(0, 1, 3, 2)
        return jnp.einsum("qgcp,gh->qgchp", t, eye).reshape(G_C // gpt, gpt * GC, gpt * P_C).astype(BF16)

    def out_layout(cc):
        t = cc.reshape(G_C // gpt, gpt, GC, P_C).transpose(0, 1, 3, 2)
        return jnp.einsum("qgpc,gh->qgphc", t, eye).reshape(G_C // gpt, gpt * P_C, gpt * GC).astype(BF16)

    s5 = dict(wbr=in_layout(bb_re), wbi=in_layout(bb_im), wcr=out_layout(p["c_c_re"][j]),
              wci=out_layout(-p["c_c_im"][j]), are=ab_re.reshape(1, S5_STATE), aim=ab_im.reshape(1, S5_STATE),
              d=row(p["c_d"][j]), wglu=p["w_glu"][j].astype(BF16))
    ffn = [dict(g=row(p["norm_ffn"][i]), wu=p["w_up"][i].astype(BF16), cw=p["conv_w"][i].astype(F32),
                cb=row(p["conv_b"][i]), wd=p["w_down"][i].astype(BF16)) for i in range(2)]
    return dict(l0=l0, rw=rw, s5=s5, ffn=ffn, g_mix=[row(p["norm_mix"][i]) for i in range(2)])


def _trunk(x, l_real, init, pp, past):
    b, lp, _ = x.shape
    l0, rows = pp["l0"], b * lp
    ua, q, kf, kb, vf, vb, lf = _inproj(x.reshape(rows, D_MODEL), pp["g_mix"][0], l0["wa"], l0["wb"],
                                        l0["e"], l0["qg"], l0["kg"], l0["fb"])
    r3 = lambda a: a.reshape(b, lp, -1)
    y_a, a_shift, a_wkv = _rwkv(r3(ua), init["a_shift"].reshape(b, 1, A_COLS), init["a_wkv"], pp["rw"], l_real)
    lf_row = jnp.transpose(r3(lf)[:, :, :N_HEADS], (0, 2, 1)).reshape(b * N_HEADS, lp)
    if past is None:
        lpad = -lp % LANES
        c_row = _cumsum_lanes(jnp.pad(lf_row, ((0, 0), (0, lpad))))[:, :lp].reshape(b, N_HEADS, lp)
        cq = jnp.pad(jnp.transpose(c_row, (0, 2, 1)), ((0, 0), (0, 0), (0, LANES - N_HEADS)))
        y_b = _attn_prompt(r3(q), r3(kb), r3(vb), cq, c_row, min(ATTN_TILE, lp))
    else:
        pk, pv, plf = past
        plen = pk.shape[1]
        plf_row = jnp.transpose(plf, (0, 2, 1)).reshape(b * N_HEADS, plen)
        c_all = _cumsum_lanes(jnp.concatenate([plf_row, jnp.pad(lf_row, ((0, 0), (0, -lp % LANES)))], axis=1))
        c_past = c_all[:, :plen].reshape(b, N_HEADS, plen)
        c_new = c_all[:, plen:plen + lp].reshape(b, N_HEADS, lp)
        cq_stack = jnp.broadcast_to(c_new.reshape(b, N_HEADS * lp, 1), (b, N_HEADS * lp, LANES))
        y_b = _attn_sample(r3(q), cq_stack, pk.reshape(b, plen, W_MIX), pv.reshape(b, plen, W_MIX), c_past,
                           r3(kb), r3(vb), c_new, l0["hm"])
    x1 = _outproj(x.reshape(rows, D_MODEL), y_a.reshape(rows, W_MIX), y_b.reshape(rows, W_MIX),
                  l0["w_out_a"], l0["w_out_b"]).reshape(b, lp, D_MODEL)
    f0 = pp["ffn"][0]
    x2, conv0 = _ffn(x1, f0["g"], f0["wu"], f0["cw"], f0["cb"], f0["wd"], init["ffn_conv"][0], l_real)
    x3, c_re, c_im = _s5(x2, pp["g_mix"][1], pp["s5"], init["c_re"].reshape(b, S5_STATE),
                         init["c_im"].reshape(b, S5_STATE), l_real)
    f1 = pp["ffn"][1]
    x4, conv1 = _ffn(x3, f1["g"], f1["wu"], f1["cw"], f1["cb"], f1["wd"], init["ffn_conv"][1], l_real)
    hd = lambda a: a.reshape(b, lp, N_HEADS, HEAD_DIM)[None, :, :l_real]
    states = (a_shift.reshape(1, b, A_COLS), a_wkv[None], hd(kf), hd(vf),
              r3(lf)[None, :, :l_real, :N_HEADS], c_re.reshape(1, b, G_C, P_C), c_im.reshape(1, b, G_C, P_C),
              jnp.stack([conv0, conv1], axis=0))
    return x4, states


def kernel(x_prompt, x_sample, state_a_shift, state_a_wkv, cache_b_k, cache_b_v, cache_b_logf, state_c_re, state_c_im, state_ffn_conv, meta, norm_mix, norm_ffn, w_in_e, a_mu, a_w0, a_w2, a_a0, a_a2, a_g2, a_kk, a_ka, a_rk, a_ln_w, a_ln_b, b_fbias, b_qnorm, b_knorm, w_out_e, c_lam_re, c_lam_im, c_log_dt, c_b_re, c_b_im, c_c_re, c_c_im, c_d, w_glu, w_up, conv_w, conv_b, w_down):
    pp = _prep_params(dict(
        norm_mix=norm_mix, norm_ffn=norm_ffn, w_in_e=w_in_e, a_mu=a_mu, a_w0=a_w0, a_w2=a_w2, a_a0=a_a0,
        a_a2=a_a2, a_g2=a_g2, a_kk=a_kk, a_ka=a_ka, a_rk=a_rk, a_ln_w=a_ln_w, a_ln_b=a_ln_b,
        b_fbias=b_fbias, b_qnorm=b_qnorm, b_knorm=b_knorm, w_out_e=w_out_e, c_lam_re=c_lam_re,
        c_lam_im=c_lam_im, c_log_dt=c_log_dt, c_b_re=c_b_re, c_b_im=c_b_im, c_c_re=c_c_re, c_c_im=c_c_im,
        c_d=c_d, w_glu=w_glu, w_up=w_up, conv_w=conv_w, conv_b=conv_b, w_down=w_down))
    b, seq, _ = x_prompt.shape
    n_meta = meta.shape[0]
    lr = n_meta + seq
    lp = -(-lr // WKV_CHUNK) * WKV_CHUNK
    xp = jnp.concatenate([jnp.broadcast_to(meta.astype(F32)[None], (b, n_meta, D_MODEL)), x_prompt,
                          jnp.zeros((b, lp - lr, D_MODEL), F32)], axis=1)
    init_p = dict(a_shift=jnp.zeros((b, A_COLS), F32),
                  a_wkv=jnp.zeros((b, N_HEADS, HEAD_DIM, HEAD_DIM), F32),
                  c_re=jnp.zeros((b, S5_STATE), F32), c_im=jnp.zeros((b, S5_STATE), F32),
                  ffn_conv=jnp.zeros((2, b, CONV_W - 1, 2 * D_FF), F32))
    yp, st_p = _trunk(xp, lr, init_p, pp, None)
    bs, ls, _ = x_sample.shape
    init_s = dict(a_shift=state_a_shift[0], a_wkv=state_a_wkv[0], c_re=state_c_re[0], c_im=state_c_im[0],
                  ffn_conv=state_ffn_conv)
    past = (cache_b_k[0], cache_b_v[0], cache_b_logf[0])
    ys, st_s = _trunk(x_sample, ls, init_s, pp, past)
    return (yp[:, n_meta:lr], ys, *st_p, *st_s)
```

```python
import functools
import math

import jax
import jax.numpy as jnp
from jax import lax
from jax.experimental import pallas as pl
from jax.experimental.pallas import tpu as pltpu

F32, BF16 = jnp.float32, jnp.bfloat16

D_MODEL = 1024
N_HEADS = 8
HEAD_DIM = 64
W_MIX = N_HEADS * HEAD_DIM
R_W, R_A, R_G = 64, 64, 128
A_COLS = 3 * W_MIX + R_W + R_A + R_G
B_PAD_COLS = 3 * W_MIX + 128
D_FF = 2816
CONV_W = 3
G_C, GC, P_C = 64, 16, 64
S5_STATE = G_C * P_C
EPS = 1e-6
LN_EPS = 64e-5
NEG = -1e30

LANES = 128
SUBLANES = 8
VMEM_LIMIT = 56 * 1024 * 1024
WKV_CHUNK = 64
WKV_SUB = 16
WKV_GROUP = 3
S5_BATCH = 8
TIME_TILE = 704
FFN_TILE = 352
ATTN_TILE = 768


def _cparams(*sem):
    return pltpu.CompilerParams(dimension_semantics=sem, vmem_limit_bytes=VMEM_LIMIT)


def _const_spec(shape):
    nd = len(shape)
    return pl.BlockSpec(shape, lambda *_: (0,) * nd, pipeline_mode=pl.Buffered(1))


def _largest_tile(n, cap, mult):
    best = None
    for t in range(mult, min(n, cap) + 1, mult):
        if n % t == 0:
            best = t
    assert best is not None, (n, cap, mult)
    return best


def _mm(a, b):
    return jnp.dot(a.astype(BF16), b.astype(BF16), preferred_element_type=F32)


def _mm_nt(a, b):
    return lax.dot_general(a.astype(BF16), b.astype(BF16), (((1,), (1,)), ((), ())),
                           preferred_element_type=F32)


def _mm_tn(a, b):
    return lax.dot_general(a.astype(BF16), b.astype(BF16), (((0,), (0,)), ((), ())),
                           preferred_element_type=F32)


def _split2(x):
    hi = x.astype(BF16)
    lo = (x - hi.astype(F32)).astype(BF16)
    return hi, lo


def _split3(x):
    hi = x.astype(BF16)
    r = x - hi.astype(F32)
    mid = r.astype(BF16)
    lo = (r - mid.astype(F32)).astype(BF16)
    return hi, mid, lo


def _mm_exact_rhs(x, m):
    hi, mid, lo = _split3(x)
    return (jnp.dot(hi, m, preferred_element_type=F32) + jnp.dot(mid, m, preferred_element_type=F32)
            + jnp.dot(lo, m, preferred_element_type=F32))


def _mm_exact_lhs(m, x):
    hi, mid, lo = _split3(x)
    return (jnp.dot(m, hi, preferred_element_type=F32) + jnp.dot(m, mid, preferred_element_type=F32)
            + jnp.dot(m, lo, preferred_element_type=F32))


def _headsum(x, e):
    hi, lo = _split2(x)
    return jnp.dot(hi, e, preferred_element_type=F32) + jnp.dot(lo, e, preferred_element_type=F32)


def _rms(x, g):
    return x * lax.rsqrt(jnp.mean(x * x, axis=-1, keepdims=True) + EPS) * g


def _sigmoid(x):
    return 1.0 / (1.0 + jnp.exp(-x))


def _log_sigmoid(x):
    return jnp.minimum(x, 0.0) - jnp.log(1.0 + jnp.exp(-jnp.abs(x)))


def _gelu(x):
    return 0.5 * x * (1.0 + jnp.tanh(math.sqrt(2.0 / math.pi) * (x + 0.044715 * (x * x * x))))


def _inproj_kernel(x_ref, g_ref, wa_ref, wb_ref, e_ref, qg_ref, kg_ref, fb_ref,
                   ua_ref, q_ref, kf_ref, kb_ref, vf_ref, vb_ref, lf_ref):
    h = _rms(x_ref[...], g_ref[...]).astype(BF16)
    ua_ref[...] = jnp.dot(h, wa_ref[...], preferred_element_type=F32)
    ub = jnp.dot(h, wb_ref[...], preferred_element_type=F32)
    q = ub[:, 0:W_MIX]
    k = ub[:, W_MIX:2 * W_MIX]
    v = ub[:, 2 * W_MIX:3 * W_MIX]
    f = ub[:, 3 * W_MIX:]
    e = e_ref[...]
    qn = q * lax.rsqrt(_headsum(q * q, e) * (1.0 / HEAD_DIM) + EPS) * qg_ref[...]
    kn = k * lax.rsqrt(_headsum(k * k, e) * (1.0 / HEAD_DIM) + EPS) * kg_ref[...]
    q_ref[...] = (qn * (HEAD_DIM ** -0.5)).astype(BF16)
    kf_ref[...] = kn
    kb_ref[...] = kn.astype(BF16)
    vf_ref[...] = v
    vb_ref[...] = v.astype(BF16)
    lf = _log_sigmoid(f + fb_ref[...])
    lane = lax.broadcasted_iota(jnp.int32, lf.shape, 1)
    lf_ref[...] = jnp.where(lane < N_HEADS, lf, 0.0)


def _inproj(x2d, g, wa, wb, e, qg, kg, fb):
    rows = x2d.shape[0]
    tm = _largest_tile(rows, 512, SUBLANES)
    row = lambda c: pl.BlockSpec((tm, c), lambda i: (i, 0))
    outs = [(A_COLS, F32), (W_MIX, BF16), (W_MIX, F32), (W_MIX, BF16), (W_MIX, F32), (W_MIX, BF16),
            (LANES, F32)]
    return pl.pallas_call(
        _inproj_kernel,
        grid=(rows // tm,),
        in_specs=[row(D_MODEL), _const_spec(g.shape), _const_spec(wa.shape), _const_spec(wb.shape),
                  _const_spec(e.shape), _const_spec(qg.shape), _const_spec(kg.shape), _const_spec(fb.shape)],
        out_specs=[row(c) for c, _ in outs],
        out_shape=[jax.ShapeDtypeStruct((rows, c), dt) for c, dt in outs],
        compiler_params=_cparams("parallel"),
        name="inproj",
    )(x2d, g, wa, wb, e, qg, kg, fb)


def _cumsum_kernel(x_ref, tri_ref, o_ref, *, nblk):
    tri = tri_ref[...]
    carry = jnp.zeros((x_ref.shape[0], 1), F32)
    for j in range(nblk):
        cs = _mm_exact_rhs(x_ref[:, j * LANES:(j + 1) * LANES], tri) + carry
        o_ref[:, j * LANES:(j + 1) * LANES] = cs
        carry = cs[:, LANES - 1:LANES]


def _cumsum_lanes(x):
    rows, n = x.shape
    assert n % LANES == 0
    tri = (jnp.arange(LANES)[:, None] <= jnp.arange(LANES)[None, :]).astype(BF16)
    return pl.pallas_call(
        functools.partial(_cumsum_kernel, nblk=n // LANES),
        grid=(1,),
        in_specs=[_const_spec(x.shape), _const_spec(tri.shape)],
        out_specs=_const_spec(x.shape),
        out_shape=jax.ShapeDtypeStruct(x.shape, F32),
        compiler_params=_cparams("arbitrary"),
        name="cumsum",
    )(x, tri)


def _attn_prompt_kernel(q_ref, k_ref, v_ref, cq_ref, cr_ref, o_ref, *, bounds):
    p = pl.program_id(1)
    lane = lax.broadcasted_iota(jnp.int32, (1, LANES), 1)
    head_lanes = (lane < HEAD_DIM, lane >= HEAD_DIM)
    for qi in range(len(bounds) - 1):
        r0, r1 = bounds[qi], bounds[qi + 1]
        q2 = q_ref[0, r0:r1, :]
        cqt = cq_ref[0, r0:r1, :]
        lane_q = lax.broadcasted_iota(jnp.int32, cqt.shape, 1)
        res = []
        for hh in range(2):
            h = 2 * p + hh
            qm = jnp.where(head_lanes[hh], q2, jnp.zeros_like(q2))
            cq = jnp.sum(jnp.where(lane_q == h, cqt, 0.0), axis=-1, keepdims=True)
            ck_all = cr_ref[0, pl.ds(h, 1), :]
            m = jnp.full((r1 - r0, 1), NEG, F32)
            l = jnp.zeros((r1 - r0, 1), F32)
            acc = jnp.zeros((r1 - r0, LANES), F32)
            for ki in range(qi + 1):
                c0, c1 = bounds[ki], bounds[ki + 1]
                s = lax.dot_general(qm, k_ref[0, c0:c1, :], (((1,), (1,)), ((), ())),
                                    preferred_element_type=F32)
                s = s + cq - ck_all[:, c0:c1]
                if ki == qi:
                    ri = lax.broadcasted_iota(jnp.int32, s.shape, 0)
                    ci = lax.broadcasted_iota(jnp.int32, s.shape, 1)
                    s = jnp.where(ri >= ci, s, NEG)
                m_new = jnp.maximum(m, jnp.max(s, axis=-1, keepdims=True))
                alpha = jnp.exp(m - m_new)
                pe = jnp.exp(s - m_new)
                l = alpha * l + jnp.sum(pe, axis=-1, keepdims=True)
                acc = alpha * acc + jnp.dot(pe.astype(BF16), v_ref[0, c0:c1, :],
                                            preferred_element_type=F32)
                m = m_new
            res.append(acc / l)
        o_ref[0, r0:r1, :] = jnp.where(head_lanes[0], res[0], res[1]).astype(BF16)


def _attn_prompt(q, k, v, cq, cr, tile):
    b, lp, _ = q.shape
    bounds = tuple(range(0, lp, tile)) + (lp,)
    col = pl.BlockSpec((1, lp, LANES), lambda i, p: (i, 0, p))
    return pl.pallas_call(
        functools.partial(_attn_prompt_kernel, bounds=bounds),
        grid=(b, N_HEADS // 2),
        in_specs=[col, col, col,
                  pl.BlockSpec((1, lp, LANES), lambda i, p: (i, 0, 0)),
                  pl.BlockSpec((1, N_HEADS, lp), lambda i, p: (i, 0, 0))],
        out_specs=col,
        out_shape=jax.ShapeDtypeStruct((b, lp, W_MIX), BF16),
        compiler_params=_cparams("parallel", "arbitrary"),
        name="attn_prompt",
    )(q, k, v, cq, cr)


def _attn_sample_kernel(q_ref, cq_ref, kp_ref, vp_ref, cp_ref, kn_ref, vn_ref, cn_ref, hm_ref, o_ref,
                        qx_scr, m_scr, l_scr, acc_scr, *, nkp, ls):
    j = pl.program_id(1)
    rows = N_HEADS * ls

    @pl.when(j == 0)
    def _():
        q = q_ref[0]
        qx = jnp.broadcast_to(q[None], (N_HEADS, ls, W_MIX)) * hm_ref[...][:, None, :]
        qx_scr[...] = qx.reshape(rows, W_MIX)
        m_scr[...] = jnp.full(m_scr.shape, NEG, F32)
        l_scr[...] = jnp.zeros(l_scr.shape, F32)
        acc_scr[...] = jnp.zeros(acc_scr.shape, F32)

    def update(k, v, ck, causal):
        tk = k.shape[0]
        s = lax.dot_general(qx_scr[...], k, (((1,), (1,)), ((), ())), preferred_element_type=F32)
        ckx = jnp.broadcast_to(ck[:, None, :], (N_HEADS, ls, tk)).reshape(rows, tk)
        s = s + cq_ref[0][:, 0:1] - ckx
        if causal:
            ri = lax.broadcasted_iota(jnp.int32, (N_HEADS, ls, tk), 1).reshape(rows, tk)
            ci = lax.broadcasted_iota(jnp.int32, (rows, tk), 1)
            s = jnp.where(ri >= ci, s, NEG)
        m = m_scr[...]
        m_new = jnp.maximum(m, jnp.max(s, axis=-1, keepdims=True))
        alpha = jnp.exp(m - m_new)
        pe = jnp.exp(s - m_new)
        l_scr[...] = alpha * l_scr[...] + jnp.sum(pe, axis=-1, keepdims=True)
        acc_scr[...] = alpha * acc_scr[...] + jnp.dot(pe.astype(BF16), v, preferred_element_type=F32)
        m_scr[...] = m_new

    @pl.when(j < nkp)
    def _():
        update(kp_ref[0].astype(BF16), vp_ref[0].astype(BF16), cp_ref[0], False)

    @pl.when(j == nkp)
    def _():
        update(kn_ref[0], vn_ref[0], cn_ref[0], True)
        o = (acc_scr[...] / l_scr[...]).reshape(N_HEADS, ls, W_MIX) * hm_ref[...].astype(F32)[:, None, :]
        o_ref[0] = jnp.sum(o, axis=0).astype(BF16)


def _attn_sample(q, cq_stack, kp, vp, cp, kn, vn, cn, hm):
    b, ls, _ = q.shape
    past = kp.shape[1]
    tkp = _largest_tile(past, 1024, LANES)
    nkp = past // tkp
    rows = N_HEADS * ls
    pidx = lambda i, j: (i, jnp.minimum(j, nkp - 1), 0)
    return pl.pallas_call(
        functools.partial(_attn_sample_kernel, nkp=nkp, ls=ls),
        grid=(b, nkp + 1),
        in_specs=[pl.BlockSpec((1, ls, W_MIX), lambda i, j: (i, 0, 0)),
                  pl.BlockSpec((1, rows, LANES), lambda i, j: (i, 0, 0)),
                  pl.BlockSpec((1, tkp, W_MIX), pidx),
                  pl.BlockSpec((1, tkp, W_MIX), pidx),
                  pl.BlockSpec((1, N_HEADS, tkp), lambda i, j: (i, 0, jnp.minimum(j, nkp - 1))),
                  pl.BlockSpec((1, ls, W_MIX), lambda i, j: (i, 0, 0)),
                  pl.BlockSpec((1, ls, W_MIX), lambda i, j: (i, 0, 0)),
                  pl.BlockSpec((1, N_HEADS, ls), lambda i, j: (i, 0, 0)),
                  _const_spec(hm.shape)],
        out_specs=pl.BlockSpec((1, ls, W_MIX), lambda i, j: (i, 0, 0)),
        out_shape=jax.ShapeDtypeStruct((b, ls, W_MIX), BF16),
        scratch_shapes=[pltpu.VMEM((rows, W_MIX), BF16), pltpu.VMEM((rows, 1), F32),
                        pltpu.VMEM((rows, 1), F32), pltpu.VMEM((rows, W_MIX), F32)],
        compiler_params=_cparams("parallel", "arbitrary"),
        name="attn_sample",
    )(q, cq_stack, kp, vp, cp, kn, vn, cn, hm)


def _wkv_chunk_matrices(tops, bots, vs, t):
    hd = range(len(tops))
    ri = lax.broadcasted_iota(jnp.int32, (t, t), 0)
    ci = lax.broadcasted_iota(jnp.int32, (t, t), 1)
    eye = (ri == ci).astype(F32)
    r2 = lax.broadcasted_iota(jnp.int32, (t, 2 * t), 0)
    c2 = lax.broadcasted_iota(jnp.int32, (t, 2 * t), 1)
    c2 = jnp.where(c2 >= t, c2 - t, c2)
    same = (ri // WKV_SUB) == (ci // WKV_SUB)
    aa = [_mm_nt(tops[h], bots[h]) for h in hd]
    n = [jnp.where(ri > ci, aa[h][:t, :t], 0.0) for h in hd]
    a_ak = [jnp.where(ri > ci, aa[h][:t, t:], 0.0) for h in hd]
    a_r = [jnp.where(r2 >= c2, aa[h][t:, :], 0.0) for h in hd]
    av = [_mm(a_ak[h], vs[h]) for h in hd]
    d = [jnp.where(same, n[h], 0.0) for h in hd]
    lo = [n[h] - d[h] for h in hd]
    d2 = [_mm(d[h], d[h]) for h in hd]
    x = [_mm(eye - d[h], eye + d2[h]) for h in hd]
    d4 = [_mm(d2[h], d2[h]) for h in hd]
    x = [_mm(x[h], eye + d4[h]) for h in hd]
    d8 = [_mm(d4[h], d4[h]) for h in hd]
    x = [_mm(x[h], eye + d8[h]) for h in hd]
    mb = [_mm(x[h], lo[h]) for h in hd]
    mb2 = [_mm(mb[h], mb[h]) for h in hd]
    xx = [_mm(eye + mb2[h], x[h]) for h in hd]
    ginv = [_mm(eye - mb[h], xx[h]) for h in hd]
    return ginv, a_r, av


def _rwkv_kernel(u_ref, sh_ref, s0_ref, mu_ref, w0_ref, a0_ref, w2a_ref, g2_ref, kk_ref, ka_ref,
                 rk_ref, lnw_ref, lnb_ref, e_ref, tril_ref,
                 y_ref, sho_ref, so_ref,
                 ubuf, carry, s_scr, r_s, wl_s, k_s, v_s, kk_s, b_s, y_s, bon_s, g_s,
                 top_s, bk_s, vh_s, av_s, gi_s, ar_s, wt_s, *, tl, t, l_real, nt):
    ti = pl.program_id(1)

    @pl.when(ti == 0)
    def _():
        carry[...] = jnp.broadcast_to(sh_ref[0], carry.shape)
        s_scr[...] = s0_ref[0]

    u = u_ref[0]
    ubuf[0:SUBLANES, :] = carry[...]
    ubuf[SUBLANES:SUBLANES + tl, :] = u
    u_prev = ubuf[SUBLANES - 1:SUBLANES - 1 + tl, :]
    carry[...] = u[tl - SUBLANES:tl, :]
    last_tile, last_row = (l_real - 1) // tl, (l_real - 1) % tl

    @pl.when(ti == last_tile)
    def _():
        sho_ref[0] = u[last_row:last_row + 1, :]

    um = u + (u_prev - u) * mu_ref[...]
    r = um[:, 0:W_MIX]
    k = um[:, W_MIX:2 * W_MIX]
    v = um[:, 2 * W_MIX:3 * W_MIX]
    wa = um[:, 3 * W_MIX:3 * W_MIX + R_W + R_A]
    gd = um[:, 3 * W_MIX + R_W + R_A:]
    lane = lax.broadcasted_iota(jnp.int32, wa.shape, 1)
    za = _mm(jnp.where(lane < R_W, jnp.tanh(wa), wa), w2a_ref[...])
    wl = -jnp.exp(_log_sigmoid(w0_ref[...] + za[:, 0:W_MIX]) - 0.5)
    a = _sigmoid(a0_ref[...] + za[:, W_MIX:])
    g_s[...] = _mm(_sigmoid(gd), g2_ref[...])
    e = e_ref[...]
    kk = k * kk_ref[...]
    kkn = kk * lax.rsqrt(_headsum(kk * kk, e) + 1e-12)
    k2 = k * (1.0 + (a - 1.0) * ka_ref[...])
    bon_s[...] = _headsum(r * k2 * rk_ref[...], e) * v
    valid = (ti * tl + lax.broadcasted_iota(jnp.int32, (tl, 1), 0)) < l_real
    r_s[...] = r
    wl_s[...] = jnp.where(valid, wl, 0.0)
    k_s[...] = jnp.where(valid, k2, 0.0)
    v_s[...] = jnp.where(valid, v, 0.0)
    kk_s[...] = jnp.where(valid, kkn, 0.0)
    b_s[...] = jnp.where(valid, kkn * a, 0.0)

    tril = tril_ref[...]

    hd = range(N_HEADS)
    hsl = [slice(h * HEAD_DIM, (h + 1) * HEAD_DIM) for h in hd]

    nc = tl // t
    grp = min(WKV_GROUP, nc)

    def chunk_matrices(i, carry_):
        cs = [jnp.minimum(i * grp + j, nc - 1) for j in range(grp)]
        tops, bots, vs, bks = [], [], [], []
        for c in cs:
            rows = pl.ds(pl.multiple_of(c * t, t), t)
            wlc = wl_s[rows, :]
            hi, mid, lo = _split3(wlc)
            cw = (jnp.dot(tril, hi, preferred_element_type=F32) + jnp.dot(tril, mid, preferred_element_type=F32)
                  + jnp.dot(tril, lo, preferred_element_type=F32))
            w_inc = jnp.exp(cw)
            w_inv = jnp.exp(-cw)
            w_prev = jnp.exp(cw - wlc)
            w_t = w_inc[t - 1:t, :]
            top = jnp.concatenate([kk_s[rows, :] * w_prev, r_s[rows, :] * w_inc], axis=0).astype(BF16)
            bot = jnp.concatenate([b_s[rows, :] * w_inv, k_s[rows, :] * w_inv], axis=0)
            bk = (bot * w_t).astype(BF16)
            bot = bot.astype(BF16)
            vc = v_s[rows, :].astype(BF16)
            tops += [top[:, hsl[h]] for h in hd]
            bots += [bot[:, hsl[h]] for h in hd]
            vs += [vc[:, hsl[h]] for h in hd]
            bks += [bk[:, hsl[h]] for h in hd]
            wt_s[c] = jnp.broadcast_to(w_t, (SUBLANES, W_MIX))
        ginv, a_r, av = _wkv_chunk_matrices(tops, bots, vs, t)
        for j, c in enumerate(cs):
            for h in hd:
                k = j * N_HEADS + h
                top_s[c, h] = tops[k]
                bk_s[c, h] = bks[k]
                vh_s[c, h] = vs[k]
                av_s[c, h] = av[k]
                gi_s[c, h] = ginv[k].astype(BF16)
                ar_s[c, h] = a_r[k].astype(BF16)
        return carry_

    def chunk_state(c, carry_):
        rows = pl.ds(pl.multiple_of(c * t, t), t)
        pp = [_mm_nt(top_s[c, h], s_scr[h]) for h in hd]
        u = [_mm(gi_s[c, h], -(pp[h][:t] + av_s[c, h])) for h in hd]
        uv = [jnp.concatenate([u[h].astype(BF16), vh_s[c, h]], axis=0) for h in hd]
        y = [pp[h][t:] + _mm(ar_s[c, h], uv[h]) for h in hd]
        w_t = wt_s[c]
        for h in hd:
            s_scr[h] = s_scr[h] * w_t[0:1, hsl[h]] + _mm_tn(uv[h], bk_s[c, h])
        y_s[rows, :] = jnp.concatenate(y, axis=1)
        return carry_

    lax.fori_loop(0, -(-nc // grp), chunk_matrices, 0)
    lax.fori_loop(0, nc, chunk_state, 0)

    y = y_s[...]
    yc = y - _headsum(y, e) * (1.0 / HEAD_DIM)
    var = _headsum(yc * yc, e) * (1.0 / HEAD_DIM)
    yn = yc * lax.rsqrt(var + LN_EPS)
    y_ref[0] = ((yn * lnw_ref[...] + lnb_ref[...] + bon_s[...]) * g_s[...]).astype(BF16)

    @pl.when(ti == nt - 1)
    def _():
        so_ref[0] = s_scr[...]


def _rwkv(ua, shift0, wkv0, prm, l_real):
    b, lp, _ = ua.shape
    t = min(WKV_CHUNK, lp)
    tl = _largest_tile(lp, TIME_TILE, t)
    nt = lp // tl
    tril = (jnp.arange(t)[:, None] >= jnp.arange(t)[None, :]).astype(BF16)
    consts = [prm["mu"], prm["w0"], prm["a0"], prm["w2a"], prm["g2"], prm["kk"], prm["ka"], prm["rk"],
              prm["lnw"], prm["lnb"], prm["e"], tril]
    big = lambda: pltpu.VMEM((tl, W_MIX), F32)
    nc = tl // t
    per_head = lambda r, c, dt: pltpu.VMEM((nc, N_HEADS, r, c), dt)
    chunk_scratch = [per_head(2 * t, HEAD_DIM, BF16), per_head(2 * t, HEAD_DIM, BF16),
                     per_head(t, HEAD_DIM, BF16), per_head(t, HEAD_DIM, F32), per_head(t, t, BF16),
                     per_head(t, 2 * t, BF16), pltpu.VMEM((nc, SUBLANES, W_MIX), F32)]
    return pl.pallas_call(
        functools.partial(_rwkv_kernel, tl=tl, t=t, l_real=l_real, nt=nt),
        grid=(b, nt),
        in_specs=[pl.BlockSpec((1, tl, A_COLS), lambda i, j: (i, j, 0)),
                  pl.BlockSpec((1, 1, A_COLS), lambda i, j: (i, 0, 0)),
                  pl.BlockSpec((1, N_HEADS, HEAD_DIM, HEAD_DIM), lambda i, j: (i, 0, 0, 0))]
                 + [_const_spec(c.shape) for c in consts],
        out_specs=[pl.BlockSpec((1, tl, W_MIX), lambda i, j: (i, j, 0)),
                   pl.BlockSpec((1, 1, A_COLS), lambda i, j: (i, 0, 0)),
                   pl.BlockSpec((1, N_HEADS, HEAD_DIM, HEAD_DIM), lambda i, j: (i, 0, 0, 0))],
        out_shape=[jax.ShapeDtypeStruct((b, lp, W_MIX), BF16),
                   jax.ShapeDtypeStruct((b, 1, A_COLS), F32),
                   jax.ShapeDtypeStruct((b, N_HEADS, HEAD_DIM, HEAD_DIM), F32)],
        scratch_shapes=[pltpu.VMEM((tl + SUBLANES, A_COLS), F32), pltpu.VMEM((SUBLANES, A_COLS), F32),
                        pltpu.VMEM((N_HEADS, HEAD_DIM, HEAD_DIM), F32)] + [big() for _ in range(9)]
                       + chunk_scratch,
        compiler_params=_cparams("parallel", "arbitrary"),
        name="rwkv",
    )(ua, shift0, wkv0, *consts)


def _outproj_kernel(x_ref, ya_ref, yb_ref, w1_ref, w2_ref, o_ref):
    o_ref[...] = (x_ref[...] + jnp.dot(ya_ref[...], w1_ref[...], preferred_element_type=F32)
                  + jnp.dot(yb_ref[...], w2_ref[...], preferred_element_type=F32))


def _outproj(x2d, ya, yb, w1, w2):
    rows = x2d.shape[0]
    tm = _largest_tile(rows, 1024, SUBLANES)
    row = lambda c: pl.BlockSpec((tm, c), lambda i: (i, 0))
    return pl.pallas_call(
        _outproj_kernel,
        grid=(rows // tm,),
        in_specs=[row(D_MODEL), row(W_MIX), row(W_MIX), _const_spec(w1.shape), _const_spec(w2.shape)],
        out_specs=row(D_MODEL),
        out_shape=jax.ShapeDtypeStruct((rows, D_MODEL), F32),
        compiler_params=_cparams("parallel"),
        name="outproj",
    )(x2d, ya, yb, w1, w2)


def _ffn_kernel(x_ref, g_ref, wu_ref, cw_ref, cb_ref, wd_ref, buf_ref, o_ref, st_ref,
                up_scr, act_scr, carry_scr, *, tl, tf, dgrp, l_real):
    ti = pl.program_id(1)
    x = x_ref[0]
    h = _rms(x, g_ref[...]).astype(BF16)
    last_tile, last_row = (l_real - 1) // tl, (l_real - 1) % tl

    @pl.when(ti == 0)
    def _():
        carry_scr[SUBLANES - (CONV_W - 1):SUBLANES, :] = buf_ref[0]

    nf = D_FF // tf
    col = lambda part, f: slice(part * D_FF + f * tf, part * D_FF + (f + 1) * tf)

    def up(f):
        for part in range(2):
            cols = col(part, f)
            up_scr[0:SUBLANES, cols] = carry_scr[:, cols]
            up_scr[SUBLANES:SUBLANES + tl, cols] = jnp.dot(h, wu_ref[:, cols], preferred_element_type=F32)
            carry_scr[:, cols] = up_scr[tl:tl + SUBLANES, cols]

    up(0)
    out = x
    g0 = 0
    for f in range(nf):
        if f + 1 < nf:
            up(f + 1)
        z = [cb_ref[:, col(part, f)]
             + cw_ref[0:1, col(part, f)] * up_scr[SUBLANES - 2:SUBLANES - 2 + tl, col(part, f)]
             + cw_ref[1:2, col(part, f)] * up_scr[SUBLANES - 1:SUBLANES - 1 + tl, col(part, f)]
             + cw_ref[2:3, col(part, f)] * up_scr[SUBLANES:SUBLANES + tl, col(part, f)]
             for part in range(2)]
        val, gate = z
        act_scr[:, f * tf:(f + 1) * tf] = (gate * _sigmoid(gate) * val).astype(BF16)
        if (f + 1) % dgrp == 0 or f + 1 == nf:
            g1 = (f + 1) * tf
            out = out + jnp.dot(act_scr[:, g0:g1], wd_ref[g0:g1, :], preferred_element_type=F32)
            g0 = g1
    o_ref[0] = out

    @pl.when(ti == last_tile)
    def _():
        st_ref[0] = up_scr[SUBLANES + last_row - 1:SUBLANES + last_row + 1, :]


def _ffn(x, g, wu, cw, cb, wd, buf, l_real):
    b, lp, _ = x.shape
    tl = _largest_tile(lp, FFN_TILE, SUBLANES)
    tf, dgrp = 256, 4
    assert l_real >= CONV_W - 1 and (l_real - 1) % tl >= 1
    return pl.pallas_call(
        functools.partial(_ffn_kernel, tl=tl, tf=tf, dgrp=dgrp, l_real=l_real),
        grid=(b, lp // tl),
        in_specs=[pl.BlockSpec((1, tl, D_MODEL), lambda i, j: (i, j, 0)),
                  _const_spec(g.shape), _const_spec(wu.shape), _const_spec(cw.shape), _const_spec(cb.shape),
                  _const_spec(wd.shape),
                  pl.BlockSpec((1, CONV_W - 1, 2 * D_FF), lambda i, j: (i, 0, 0))],
        out_specs=[pl.BlockSpec((1, tl, D_MODEL), lambda i, j: (i, j, 0)),
                   pl.BlockSpec((1, CONV_W - 1, 2 * D_FF), lambda i, j: (i, 0, 0))],
        out_shape=[jax.ShapeDtypeStruct((b, lp, D_MODEL), F32),
                   jax.ShapeDtypeStruct((b, CONV_W - 1, 2 * D_FF), F32)],
        scratch_shapes=[pltpu.VMEM((tl + SUBLANES, 2 * D_FF), F32), pltpu.VMEM((tl, D_FF), BF16),
                        pltpu.VMEM((SUBLANES, 2 * D_FF), F32)],
        compiler_params=_cparams("parallel", "arbitrary"),
        name="ffn",
    )(x, g, wu, cw, cb, wd, buf)


def _s5_kernel(x_ref, g_ref, wbr_ref, wbi_ref, wcr_ref, wci_ref, are_ref, aim_ref, d_ref, wg_ref,
               perm_ref, permt_ref, s0r_ref, s0i_ref, o_ref, sor_ref, soi_ref, bu_scr, st_scr, *, tt, l_real):
    ti = pl.program_id(1)
    nb = S5_BATCH
    m = nb * tt
    ntile = S5_STATE // LANES

    @pl.when(ti == 0)
    def _():
        for c in range(ntile):
            st_scr[c] = s0r_ref[:, c * LANES:(c + 1) * LANES]
            st_scr[ntile + c] = s0i_ref[:, c * LANES:(c + 1) * LANES]

    x = x_ref[...].reshape(m, D_MODEL)
    perm = perm_ref[...]
    u_hi, u_lo = _split2(_rms(x, g_ref[...]))
    ub_f = jnp.dot(perm, u_hi, preferred_element_type=F32)
    ub = ub_f.astype(BF16)
    u = ub_f + jnp.dot(perm, u_lo, preferred_element_type=F32)
    tpq = LANES // GC * P_C // LANES
    for q in range(D_MODEL // LANES):
        uq = ub[:, q * LANES:(q + 1) * LANES]
        br = jnp.dot(uq, wbr_ref[q], preferred_element_type=F32)
        bi = jnp.dot(uq, wbi_ref[q], preferred_element_type=F32)
        for c in range(tpq):
            bu_scr[q * tpq + c] = br[:, c * LANES:(c + 1) * LANES]
            bu_scr[ntile + q * tpq + c] = bi[:, c * LANES:(c + 1) * LANES]

    grp = 4
    for cb in range(ntile // grp):
        tiles = list(range(cb * grp, (cb + 1) * grp))
        a_re = [jnp.broadcast_to(are_ref[:, c * LANES:(c + 1) * LANES], (nb, LANES)) for c in tiles]
        a_im = [jnp.broadcast_to(aim_ref[:, c * LANES:(c + 1) * LANES], (nb, LANES)) for c in tiles]
        s_re = [st_scr[c] for c in tiles]
        s_im = [st_scr[ntile + c] for c in tiles]
        for t in range(tt):
            rows = slice(t * nb, (t + 1) * nb)
            for i, c in enumerate(tiles):
                n_re = a_re[i] * s_re[i] - a_im[i] * s_im[i] + bu_scr[c, rows, :]
                n_im = a_re[i] * s_im[i] + a_im[i] * s_re[i] + bu_scr[ntile + c, rows, :]
                bu_scr[c, rows, :] = n_re
                bu_scr[ntile + c, rows, :] = n_im
                s_re[i], s_im[i] = n_re, n_im
        for i, c in enumerate(tiles):
            st_scr[c] = s_re[i]
            st_scr[ntile + c] = s_im[i]

    ys = []
    for q in range(D_MODEL // LANES):
        sr = jnp.concatenate([bu_scr[q * tpq + c] for c in range(tpq)], axis=1).astype(BF16)
        si = jnp.concatenate([bu_scr[ntile + q * tpq + c] for c in range(tpq)], axis=1).astype(BF16)
        ys.append(jnp.dot(sr, wcr_ref[q], preferred_element_type=F32)
                  + jnp.dot(si, wci_ref[q], preferred_element_type=F32))
    yc = jnp.concatenate(ys, axis=1) + d_ref[...] * u
    z = jnp.dot(_gelu(yc).astype(BF16), wg_ref[...], preferred_element_type=F32)
    mix = z[:, 0:D_MODEL] * _sigmoid(z[:, D_MODEL:])
    o_ref[...] = (x + _mm_exact_lhs(permt_ref[...], mix)).reshape(nb, tt, D_MODEL)
    last_tile, last_row = (l_real - 1) // tt, (l_real - 1) % tt

    @pl.when(ti == last_tile)
    def _():
        rows = slice(last_row * nb, (last_row + 1) * nb)
        for c in range(ntile):
            sor_ref[:, c * LANES:(c + 1) * LANES] = bu_scr[c, rows, :]
            soi_ref[:, c * LANES:(c + 1) * LANES] = bu_scr[ntile + c, rows, :]


def _s5(x, g, sp, s0r, s0i, l_real):
    b, lp, _ = x.shape
    assert b % S5_BATCH == 0
    tt = _largest_tile(lp, 32, SUBLANES)
    m = S5_BATCH * tt
    dst = jnp.arange(m)
    src = (dst % S5_BATCH) * tt + dst // S5_BATCH
    perm = (src[:, None] == jnp.arange(m)[None, :]).astype(BF16)
    consts = [g, sp["wbr"], sp["wbi"], sp["wcr"], sp["wci"], sp["are"], sp["aim"], sp["d"], sp["wglu"],
              perm, perm.T]
    st = pl.BlockSpec((S5_BATCH, S5_STATE), lambda i, j: (i, 0))
    return pl.pallas_call(
        functools.partial(_s5_kernel, tt=tt, l_real=l_real),
        grid=(b // S5_BATCH, lp // tt),
        in_specs=[pl.BlockSpec((S5_BATCH, tt, D_MODEL), lambda i, j: (i, j, 0))]
                 + [_const_spec(c.shape) for c in consts] + [st, st],
        out_specs=[pl.BlockSpec((S5_BATCH, tt, D_MODEL), lambda i, j: (i, j, 0)), st, st],
        out_shape=[jax.ShapeDtypeStruct((b, lp, D_MODEL), F32),
                   jax.ShapeDtypeStruct((b, S5_STATE), F32), jax.ShapeDtypeStruct((b, S5_STATE), F32)],
        scratch_shapes=[pltpu.VMEM((2 * S5_STATE // LANES, m, LANES), F32),
                        pltpu.VMEM((2 * S5_STATE // LANES, S5_BATCH, LANES), F32)],
        compiler_params=_cparams("parallel", "arbitrary"),
        name="s5",
    )(x, *consts, s0r, s0i)


def _prep_params(p):
    j = 0
    head_of = jnp.arange(W_MIX) // HEAD_DIM
    e = (head_of[:, None] == head_of[None, :]).astype(BF16)
    w_in = p["w_in_e"][j]
    wf = jnp.zeros((D_MODEL, LANES), F32).at[:, :N_HEADS].set(w_in[:, A_COLS + 3 * W_MIX:])
    wb = jnp.concatenate([w_in[:, A_COLS:A_COLS + 3 * W_MIX], wf], axis=1).astype(BF16)
    fb = jnp.zeros((1, LANES), F32).at[0, :N_HEADS].set(p["b_fbias"][j])
    w2a = jnp.zeros((R_W + R_A, 2 * W_MIX), F32)
    w2a = w2a.at[:R_W, :W_MIX].set(p["a_w2"][j]).at[R_W:, W_MIX:].set(p["a_a2"][j])
    row = lambda a: a.reshape(1, -1).astype(F32)
    rw = dict(mu=row(p["a_mu"][j]), w0=row(p["a_w0"][j]), a0=row(p["a_a0"][j]), w2a=w2a.astype(BF16),
              g2=p["a_g2"][j].astype(BF16), kk=row(p["a_kk"][j]), ka=row(p["a_ka"][j]),
              rk=row(p["a_rk"][j]), lnw=row(p["a_ln_w"][j]), lnb=row(p["a_ln_b"][j]), e=e)
    hm = (jnp.arange(N_HEADS)[:, None] == head_of[None, :]).astype(BF16)
    l0 = dict(wa=w_in[:, :A_COLS].astype(BF16), wb=wb, fb=fb, e=e, hm=hm,
              qg=row(jnp.tile(p["b_qnorm"][j], N_HEADS)), kg=row(jnp.tile(p["b_knorm"][j], N_HEADS)),
              w_out_a=p["w_out_e"][j][:W_MIX].astype(BF16), w_out_b=p["w_out_e"][j][W_MIX:].astype(BF16))
    lam_re, lam_im = p["c_lam_re"][j].astype(F32), p["c_lam_im"][j].astype(F32)
    dt = jnp.exp(p["c_log_dt"][j].astype(F32))[:, None]
    mag = jnp.exp(lam_re * dt)
    ab_re, ab_im = mag * jnp.cos(lam_im * dt), mag * jnp.sin(lam_im * dt)
    den = lam_re * lam_re + lam_im * lam_im
    f_re = ((ab_re - 1.0) * lam_re + ab_im * lam_im) / den
    f_im = (ab_im * lam_re - (ab_re - 1.0) * lam_im) / den
    b_re, b_im = p["c_b_re"][j], p["c_b_im"][j]
    bb_re = f_re[..., None] * b_re - f_im[..., None] * b_im
    bb_im = f_re[..., None] * b_im + f_im[..., None] * b_re
    gpt = LANES // GC
    eye = jnp.eye(gpt, dtype=F32)

    def in_layout(bb):
        t = bb.reshape(G_C // gpt, gpt, P_C, GC).transpose(0, 1, 3, 2)
        return jnp.einsum("qgcp,gh->qgchp", t, eye).reshape(G_C // gpt, gpt * GC, gpt * P_C).astype(BF16)

    def out_layout(cc):
        t = cc.reshape(G_C // gpt, gpt, GC, P_C).transpose(0, 1, 3, 2)
        return jnp.einsum("qgpc,gh->qgphc", t, eye).reshape(G_C // gpt, gpt * P_C, gpt * GC).astype(BF16)

    s5 = dict(wbr=in_layout(bb_re), wbi=in_layout(bb_im), wcr=out_layout(p["c_c_re"][j]),
              wci=out_layout(-p["c_c_im"][j]), are=ab_re.reshape(1, S5_STATE), aim=ab_im.reshape(1, S5_STATE),
              d=row(p["c_d"][j]), wglu=p["w_glu"][j].astype(BF16))
    ffn = [dict(g=row(p["norm_ffn"][i]), wu=p["w_up"][i].astype(BF16), cw=p["conv_w"][i].astype(F32),
                cb=row(p["conv_b"][i]), wd=p["w_down"][i].astype(BF16)) for i in range(2)]
    return dict(l0=l0, rw=rw, s5=s5, ffn=ffn, g_mix=[row(p["norm_mix"][i]) for i in range(2)])


def _trunk(x, l_real, init, pp, past):
    b, lp, _ = x.shape
    l0, rows = pp["l0"], b * lp
    ua, q, kf, kb, vf, vb, lf = _inproj(x.reshape(rows, D_MODEL), pp["g_mix"][0], l0["wa"], l0["wb"],
                                        l0["e"], l0["qg"], l0["kg"], l0["fb"])
    r3 = lambda a: a.reshape(b, lp, -1)
    y_a, a_shift, a_wkv = _rwkv(r3(ua), init["a_shift"].reshape(b, 1, A_COLS), init["a_wkv"], pp["rw"], l_real)
    lf_row = jnp.transpose(r3(lf)[:, :, :N_HEADS], (0, 2, 1)).reshape(b * N_HEADS, lp)
    if past is None:
        lpad = -lp % LANES
        c_row = _cumsum_lanes(jnp.pad(lf_row, ((0, 0), (0, lpad))))[:, :lp].reshape(b, N_HEADS, lp)
        cq = jnp.pad(jnp.transpose(c_row, (0, 2, 1)), ((0, 0), (0, 0), (0, LANES - N_HEADS)))
        y_b = _attn_prompt(r3(q), r3(kb), r3(vb), cq, c_row, min(ATTN_TILE, lp))
    else:
        pk, pv, plf = past
        plen = pk.shape[1]
        plf_row = jnp.transpose(plf, (0, 2, 1)).reshape(b * N_HEADS, plen)
        c_all = _cumsum_lanes(jnp.concatenate([plf_row, jnp.pad(lf_row, ((0, 0), (0, -lp % LANES)))], axis=1))
        c_past = c_all[:, :plen].reshape(b, N_HEADS, plen)
        c_new = c_all[:, plen:plen + lp].reshape(b, N_HEADS, lp)
        cq_stack = jnp.broadcast_to(c_new.reshape(b, N_HEADS * lp, 1), (b, N_HEADS * lp, LANES))
        y_b = _attn_sample(r3(q), cq_stack, pk.reshape(b, plen, W_MIX), pv.reshape(b, plen, W_MIX), c_past,
                           r3(kb), r3(vb), c_new, l0["hm"])
    x1 = _outproj(x.reshape(rows, D_MODEL), y_a.reshape(rows, W_MIX), y_b.reshape(rows, W_MIX),
                  l0["w_out_a"], l0["w_out_b"]).reshape(b, lp, D_MODEL)
    f0 = pp["ffn"][0]
    x2, conv0 = _ffn(x1, f0["g"], f0["wu"], f0["cw"], f0["cb"], f0["wd"], init["ffn_conv"][0], l_real)
    x3, c_re, c_im = _s5(x2, pp["g_mix"][1], pp["s5"], init["c_re"].reshape(b, S5_STATE),
                         init["c_im"].reshape(b, S5_STATE), l_real)
    f1 = pp["ffn"][1]
    x4, conv1 = _ffn(x3, f1["g"], f1["wu"], f1["cw"], f1["cb"], f1["wd"], init["ffn_conv"][1], l_real)
    hd = lambda a: a.reshape(b, lp, N_HEADS, HEAD_DIM)[None, :, :l_real]
    states = (a_shift.reshape(1, b, A_COLS), a_wkv[None], hd(kf), hd(vf),
              r3(lf)[None, :, :l_real, :N_HEADS], c_re.reshape(1, b, G_C, P_C), c_im.reshape(1, b, G_C, P_C),
              jnp.stack([conv0, conv1], axis=0))
    return x4, states


def kernel(x_prompt, x_sample, state_a_shift, state_a_wkv, cache_b_k, cache_b_v, cache_b_logf, state_c_re, state_c_im, state_ffn_conv, meta, norm_mix, norm_ffn, w_in_e, a_mu, a_w0, a_w2, a_a0, a_a2, a_g2, a_kk, a_ka, a_rk, a_ln_w, a_ln_b, b_fbias, b_qnorm, b_knorm, w_out_e, c_lam_re, c_lam_im, c_log_dt, c_b_re, c_b_im, c_c_re, c_c_im, c_d, w_glu, w_up, conv_w, conv_b, w_down):
    pp = _prep_params(dict(
        norm_mix=norm_mix, norm_ffn=norm_ffn, w_in_e=w_in_e, a_mu=a_mu, a_w0=a_w0, a_w2=a_w2, a_a0=a_a0,
        a_a2=a_a2, a_g2=a_g2, a_kk=a_kk, a_ka=a_ka, a_rk=a_rk, a_ln_w=a_ln_w, a_ln_b=a_ln_b,
        b_fbias=b_fbias, b_qnorm=b_qnorm, b_knorm=b_knorm, w_out_e=w_out_e, c_lam_re=c_lam_re,
        c_lam_im=c_lam_im, c_log_dt=c_log_dt, c_b_re=c_b_re, c_b_im=c_b_im, c_c_re=c_c_re, c_c_im=c_c_im,
        c_d=c_d, w_glu=w_glu, w_up=w_up, conv_w=conv_w, conv_b=conv_b, w_down=w_down))
    b, seq, _ = x_prompt.shape
    n_meta = meta.shape[0]
    lr = n_meta + seq
    lp = -(-lr // WKV_CHUNK) * WKV_CHUNK
    xp = jnp.concatenate([jnp.broadcast_to(meta.astype(F32)[None], (b, n_meta, D_MODEL)), x_prompt,
                          jnp.zeros((b, lp - lr, D_MODEL), F32)], axis=1)
    init_p = dict(a_shift=jnp.zeros((b, A_COLS), F32),
                  a_wkv=jnp.zeros((b, N_HEADS, HEAD_DIM, HEAD_DIM), F32),
                  c_re=jnp.zeros((b, S5_STATE), F32), c_im=jnp.zeros((b, S5_STATE), F32),
                  ffn_conv=jnp.zeros((2, b, CONV_W - 1, 2 * D_FF), F32))
    yp, st_p = _trunk(xp, lr, init_p, pp, None)
    bs, ls, _ = x_sample.shape
    init_s = dict(a_shift=state_a_shift[0], a_wkv=state_a_wkv[0], c_re=state_c_re[0], c_im=state_c_im[0],
                  ffn_conv=state_ffn_conv)
    past = (cache_b_k[0], cache_b_v[0], cache_b_logf[0])
    ys, st_s = _trunk(x_sample, ls, init_s, pp, past)
    return (yp[:, n_meta:lr], ys, *st_p, *st_s)
```

```python
import functools
import math

import jax
import jax.numpy as jnp
from jax import lax
from jax.experimental import pallas as pl
from jax.experimental.pallas import tpu as pltpu

F32, BF16 = jnp.float32, jnp.bfloat16

D_MODEL = 1024
N_HEADS = 8
HEAD_DIM = 64
W_MIX = N_HEADS * HEAD_DIM
R_W, R_A, R_G = 64, 64, 128
A_COLS = 3 * W_MIX + R_W + R_A + R_G
B_PAD_COLS = 3 * W_MIX + 128
D_FF = 2816
CONV_W = 3
G_C, GC, P_C = 64, 16, 64
S5_STATE = G_C * P_C
EPS = 1e-6
LN_EPS = 64e-5
NEG = -1e30

LANES = 128
SUBLANES = 8
VMEM_LIMIT = 56 * 1024 * 1024
WKV_CHUNK = 64
WKV_SUB = 16
WKV_GROUP = 3
S5_BATCH = 8
TIME_TILE = 704
FFN_TILE = 528
INPROJ_TILE = 352
ATTN_TILE = 768


def _cparams(*sem):
    return pltpu.CompilerParams(dimension_semantics=sem, vmem_limit_bytes=VMEM_LIMIT)


def _const_spec(shape):
    nd = len(shape)
    return pl.BlockSpec(shape, lambda *_: (0,) * nd, pipeline_mode=pl.Buffered(1))


def _largest_tile(n, cap, mult):
    best = None
    for t in range(mult, min(n, cap) + 1, mult):
        if n % t == 0:
            best = t
    assert best is not None, (n, cap, mult)
    return best


def _mm(a, b):
    return jnp.dot(a.astype(BF16), b.astype(BF16), preferred_element_type=F32)


def _mm_nt(a, b):
    return lax.dot_general(a.astype(BF16), b.astype(BF16), (((1,), (1,)), ((), ())),
                           preferred_element_type=F32)


def _mm_tn(a, b):
    return lax.dot_general(a.astype(BF16), b.astype(BF16), (((0,), (0,)), ((), ())),
                           preferred_element_type=F32)


def _split2(x):
    hi = x.astype(BF16)
    lo = (x - hi.astype(F32)).astype(BF16)
    return hi, lo


def _split3(x):
    hi = x.astype(BF16)
    r = x - hi.astype(F32)
    mid = r.astype(BF16)
    lo = (r - mid.astype(F32)).astype(BF16)
    return hi, mid, lo


def _mm_exact_rhs(x, m):
    hi, mid, lo = _split3(x)
    return (jnp.dot(hi, m, preferred_element_type=F32) + jnp.dot(mid, m, preferred_element_type=F32)
            + jnp.dot(lo, m, preferred_element_type=F32))


def _mm_exact_lhs(m, x):
    hi, mid, lo = _split3(x)
    return (jnp.dot(m, hi, preferred_element_type=F32) + jnp.dot(m, mid, preferred_element_type=F32)
            + jnp.dot(m, lo, preferred_element_type=F32))


def _headsum(x, e):
    return jnp.dot(x.astype(BF16), e, preferred_element_type=F32)


def _rms(x, g):
    return x * lax.rsqrt(jnp.mean(x * x, axis=-1, keepdims=True) + EPS) * g


def _sigmoid(x):
    return 1.0 / (1.0 + jnp.exp(-x))


def _log_sigmoid(x):
    return jnp.minimum(x, 0.0) - jnp.log(1.0 + jnp.exp(-jnp.abs(x)))


def _gelu(x):
    return 0.5 * x * (1.0 + jnp.tanh(math.sqrt(2.0 / math.pi) * (x + 0.044715 * (x * x * x))))


def _inproj_kernel(x_ref, g_ref, wa_ref, wb_ref, e_ref, qg_ref, kg_ref, fb_ref,
                   ua_ref, q_ref, kf_ref, kb_ref, vf_ref, vb_ref, lf_ref):
    h = _rms(x_ref[0], g_ref[...]).astype(BF16)
    ua_ref[0] = jnp.dot(h, wa_ref[...], preferred_element_type=F32)
    ub = jnp.dot(h, wb_ref[...], preferred_element_type=F32)
    q = ub[:, 0:W_MIX]
    k = ub[:, W_MIX:2 * W_MIX]
    v = ub[:, 2 * W_MIX:3 * W_MIX]
    f = ub[:, 3 * W_MIX:]
    e = e_ref[...]
    qn = q * lax.rsqrt(_headsum(q * q, e) * (1.0 / HEAD_DIM) + EPS) * qg_ref[...]
    kn = k * lax.rsqrt(_headsum(k * k, e) * (1.0 / HEAD_DIM) + EPS) * kg_ref[...]
    tm = q.shape[0]
    q_ref[0] = (qn * (HEAD_DIM ** -0.5)).astype(BF16)
    kf_ref[0] = kn.reshape(tm, N_HEADS, HEAD_DIM)
    kb_ref[0] = kn.astype(BF16)
    vf_ref[0] = v.reshape(tm, N_HEADS, HEAD_DIM)
    vb_ref[0] = v.astype(BF16)
    lf = _log_sigmoid(f + fb_ref[...])
    lane = lax.broadcasted_iota(jnp.int32, lf.shape, 1)
    lf_ref[0] = jnp.where(lane < N_HEADS, lf, 0.0)


def _inproj(x, l_real, g, wa, wb, e, qg, kg, fb):
    b, lp, _ = x.shape
    tm = _largest_tile(lp, INPROJ_TILE, SUBLANES)
    row = lambda c: pl.BlockSpec((1, tm, c), lambda i, j: (i, j, 0))
    cache = pl.BlockSpec((1, tm, N_HEADS, HEAD_DIM), lambda i, j: (i, j, 0, 0))
    rowshape = lambda c, dt: jax.ShapeDtypeStruct((b, lp, c), dt)
    cacheshape = jax.ShapeDtypeStruct((b, l_real, N_HEADS, HEAD_DIM), F32)
    return pl.pallas_call(
        _inproj_kernel,
        grid=(b, lp // tm),
        in_specs=[row(D_MODEL), _const_spec(g.shape), _const_spec(wa.shape), _const_spec(wb.shape),
                  _const_spec(e.shape), _const_spec(qg.shape), _const_spec(kg.shape), _const_spec(fb.shape)],
        out_specs=[row(A_COLS), row(W_MIX), cache, row(W_MIX), cache, row(W_MIX), row(LANES)],
        out_shape=[rowshape(A_COLS, F32), rowshape(W_MIX, BF16), cacheshape, rowshape(W_MIX, BF16),
                   cacheshape, rowshape(W_MIX, BF16), rowshape(LANES, F32)],
        compiler_params=_cparams("parallel", "parallel"),
        name="inproj",
    )(x, g, wa, wb, e, qg, kg, fb)


def _cumsum_kernel(x_ref, tri_ref, o_ref, *, nblk):
    tri = tri_ref[...]
    carry = jnp.zeros((x_ref.shape[0], 1), F32)
    for j in range(nblk):
        cs = _mm_exact_rhs(x_ref[:, j * LANES:(j + 1) * LANES], tri) + carry
        o_ref[:, j * LANES:(j + 1) * LANES] = cs
        carry = cs[:, LANES - 1:LANES]


def _cumsum_lanes(x):
    rows, n = x.shape
    assert n % LANES == 0
    tri = (jnp.arange(LANES)[:, None] <= jnp.arange(LANES)[None, :]).astype(BF16)
    return pl.pallas_call(
        functools.partial(_cumsum_kernel, nblk=n // LANES),
        grid=(1,),
        in_specs=[_const_spec(x.shape), _const_spec(tri.shape)],
        out_specs=_const_spec(x.shape),
        out_shape=jax.ShapeDtypeStruct(x.shape, F32),
        compiler_params=_cparams("arbitrary"),
        name="cumsum",
    )(x, tri)


def _attn_prompt_kernel(q_ref, k_ref, v_ref, cq_ref, cr_ref, o_ref, *, bounds):
    p = pl.program_id(1)
    lane = lax.broadcasted_iota(jnp.int32, (1, LANES), 1)
    head_lanes = (lane < HEAD_DIM, lane >= HEAD_DIM)
    for qi in range(len(bounds) - 1):
        r0, r1 = bounds[qi], bounds[qi + 1]
        q2 = q_ref[0, r0:r1, :]
        cqt = cq_ref[0, r0:r1, :]
        lane_q = lax.broadcasted_iota(jnp.int32, cqt.shape, 1)
        res = []
        for hh in range(2):
            h = 2 * p + hh
            qm = jnp.where(head_lanes[hh], q2, jnp.zeros_like(q2))
            cq = jnp.sum(jnp.where(lane_q == h, cqt, 0.0), axis=-1, keepdims=True)
            ck_all = cr_ref[0, pl.ds(h, 1), :]
            m = jnp.full((r1 - r0, 1), NEG, F32)
            l = jnp.zeros((r1 - r0, 1), F32)
            acc = jnp.zeros((r1 - r0, LANES), F32)
            for ki in range(qi + 1):
                c0, c1 = bounds[ki], bounds[ki + 1]
                s = lax.dot_general(qm, k_ref[0, c0:c1, :], (((1,), (1,)), ((), ())),
                                    preferred_element_type=F32)
                s = s + cq - ck_all[:, c0:c1]
                if ki == qi:
                    ri = lax.broadcasted_iota(jnp.int32, s.shape, 0)
                    ci = lax.broadcasted_iota(jnp.int32, s.shape, 1)
                    s = jnp.where(ri >= ci, s, NEG)
                m_new = jnp.maximum(m, jnp.max(s, axis=-1, keepdims=True))
                alpha = jnp.exp(m - m_new)
                pe = jnp.exp(s - m_new)
                l = alpha * l + jnp.sum(pe, axis=-1, keepdims=True)
                acc = alpha * acc + jnp.dot(pe.astype(BF16), v_ref[0, c0:c1, :],
                                            preferred_element_type=F32)
                m = m_new
            res.append(acc / l)
        o_ref[0, r0:r1, :] = jnp.where(head_lanes[0], res[0], res[1]).astype(BF16)


def _attn_prompt(q, k, v, cq, cr, tile):
    b, lp, _ = q.shape
    bounds = tuple(range(0, lp, tile)) + (lp,)
    col = pl.BlockSpec((1, lp, LANES), lambda i, p: (i, 0, p))
    return pl.pallas_call(
        functools.partial(_attn_prompt_kernel, bounds=bounds),
        grid=(b, N_HEADS // 2),
        in_specs=[col, col, col,
                  pl.BlockSpec((1, lp, LANES), lambda i, p: (i, 0, 0)),
                  pl.BlockSpec((1, N_HEADS, lp), lambda i, p: (i, 0, 0))],
        out_specs=col,
        out_shape=jax.ShapeDtypeStruct((b, lp, W_MIX), BF16),
        compiler_params=_cparams("parallel", "arbitrary"),
        name="attn_prompt",
    )(q, k, v, cq, cr)


def _attn_sample_kernel(q_ref, cq_ref, kp_ref, vp_ref, cp_ref, kn_ref, vn_ref, cn_ref, hm_ref, o_ref,
                        qx_scr, m_scr, l_scr, acc_scr, *, nkp, ls):
    j = pl.program_id(1)
    rows = N_HEADS * ls

    @pl.when(j == 0)
    def _():
        q = q_ref[0]
        qx = jnp.broadcast_to(q[None], (N_HEADS, ls, W_MIX)) * hm_ref[...][:, None, :]
        qx_scr[...] = qx.reshape(rows, W_MIX)
        m_scr[...] = jnp.full(m_scr.shape, NEG, F32)
        l_scr[...] = jnp.zeros(l_scr.shape, F32)
        acc_scr[...] = jnp.zeros(acc_scr.shape, F32)

    def update(k, v, ck, causal):
        tk = k.shape[0]
        s = lax.dot_general(qx_scr[...], k, (((1,), (1,)), ((), ())), preferred_element_type=F32)
        ckx = jnp.broadcast_to(ck[:, None, :], (N_HEADS, ls, tk)).reshape(rows, tk)
        s = s + cq_ref[0][:, 0:1] - ckx
        if causal:
            ri = lax.broadcasted_iota(jnp.int32, (N_HEADS, ls, tk), 1).reshape(rows, tk)
            ci = lax.broadcasted_iota(jnp.int32, (rows, tk), 1)
            s = jnp.where(ri >= ci, s, NEG)
        m = m_scr[...]
        m_new = jnp.maximum(m, jnp.max(s, axis=-1, keepdims=True))
        alpha = jnp.exp(m - m_new)
        pe = jnp.exp(s - m_new)
        l_scr[...] = alpha * l_scr[...] + jnp.sum(pe, axis=-1, keepdims=True)
        acc_scr[...] = alpha * acc_scr[...] + jnp.dot(pe.astype(BF16), v, preferred_element_type=F32)
        m_scr[...] = m_new

    @pl.when(j < nkp)
    def _():
        update(kp_ref[0].astype(BF16), vp_ref[0].astype(BF16), cp_ref[0], False)

    @pl.when(j == nkp)
    def _():
        update(kn_ref[0], vn_ref[0], cn_ref[0], True)
        o = (acc_scr[...] / l_scr[...]).reshape(N_HEADS, ls, W_MIX) * hm_ref[...].astype(F32)[:, None, :]
        o_ref[0] = jnp.sum(o, axis=0).astype(BF16)


def _attn_sample(q, cq_stack, kp, vp, cp, kn, vn, cn, hm):
    b, ls, _ = q.shape
    past = kp.shape[1]
    tkp = _largest_tile(past, 1024, LANES)
    nkp = past // tkp
    rows = N_HEADS * ls
    pidx = lambda i, j: (i, jnp.minimum(j, nkp - 1), 0)
    return pl.pallas_call(
        functools.partial(_attn_sample_kernel, nkp=nkp, ls=ls),
        grid=(b, nkp + 1),
        in_specs=[pl.BlockSpec((1, ls, W_MIX), lambda i, j: (i, 0, 0)),
                  pl.BlockSpec((1, rows, LANES), lambda i, j: (i, 0, 0)),
                  pl.BlockSpec((1, tkp, W_MIX), pidx),
                  pl.BlockSpec((1, tkp, W_MIX), pidx),
                  pl.BlockSpec((1, N_HEADS, tkp), lambda i, j: (i, 0, jnp.minimum(j, nkp - 1))),
                  pl.BlockSpec((1, ls, W_MIX), lambda i, j: (i, 0, 0)),
                  pl.BlockSpec((1, ls, W_MIX), lambda i, j: (i, 0, 0)),
                  pl.BlockSpec((1, N_HEADS, ls), lambda i, j: (i, 0, 0)),
                  _const_spec(hm.shape)],
        out_specs=pl.BlockSpec((1, ls, W_MIX), lambda i, j: (i, 0, 0)),
        out_shape=jax.ShapeDtypeStruct((b, ls, W_MIX), BF16),
        scratch_shapes=[pltpu.VMEM((rows, W_MIX), BF16), pltpu.VMEM((rows, 1), F32),
                        pltpu.VMEM((rows, 1), F32), pltpu.VMEM((rows, W_MIX), F32)],
        compiler_params=_cparams("parallel", "arbitrary"),
        name="attn_sample",
    )(q, cq_stack, kp, vp, cp, kn, vn, cn, hm)


def _wkv_chunk_matrices(tops, bots, vs, t):
    hd = range(len(tops))
    ri = lax.broadcasted_iota(jnp.int32, (t, t), 0)
    ci = lax.broadcasted_iota(jnp.int32, (t, t), 1)
    eye = (ri == ci).astype(F32)
    r2 = lax.broadcasted_iota(jnp.int32, (t, 2 * t), 0)
    c2 = lax.broadcasted_iota(jnp.int32, (t, 2 * t), 1)
    c2 = jnp.where(c2 >= t, c2 - t, c2)
    same = (ri // WKV_SUB) == (ci // WKV_SUB)
    aa = [_mm_nt(tops[h], bots[h]) for h in hd]
    n = [jnp.where(ri > ci, aa[h][:t, :t], 0.0) for h in hd]
    a_ak = [jnp.where(ri > ci, aa[h][:t, t:], 0.0) for h in hd]
    a_r = [jnp.where(r2 >= c2, aa[h][t:, :], 0.0) for h in hd]
    av = [_mm(a_ak[h], vs[h]) for h in hd]
    d = [jnp.where(same, n[h], 0.0) for h in hd]
    lo = [n[h] - d[h] for h in hd]
    d2 = [_mm(d[h], d[h]) for h in hd]
    x = [_mm(eye - d[h], eye + d2[h]) for h in hd]
    d4 = [_mm(d2[h], d2[h]) for h in hd]
    x = [_mm(x[h], eye + d4[h]) for h in hd]
    d8 = [_mm(d4[h], d4[h]) for h in hd]
    x = [_mm(x[h], eye + d8[h]) for h in hd]
    mb = [_mm(x[h], lo[h]) for h in hd]
    mb2 = [_mm(mb[h], mb[h]) for h in hd]
    xx = [_mm(eye + mb2[h], x[h]) for h in hd]
    ginv = [_mm(eye - mb[h], xx[h]) for h in hd]
    return ginv, a_r, av


def _rwkv_kernel(u_ref, sh_ref, s0_ref, mu_ref, w0_ref, a0_ref, w2a_ref, g2_ref, kk_ref, ka_ref,
                 rk_ref, lnw_ref, lnb_ref, e_ref, tril_ref,
                 y_ref, sho_ref, so_ref,
                 ubuf, carry, s_scr, r_s, wl_s, k_s, v_s, kk_s, b_s, y_s, bon_s, g_s,
                 top_s, bk_s, vh_s, av_s, gi_s, ar_s, wt_s, *, tl, t, l_real, nt):
    ti = pl.program_id(1)

    @pl.when(ti == 0)
    def _():
        carry[...] = jnp.broadcast_to(sh_ref[0], carry.shape)
        s_scr[...] = s0_ref[0]

    u = u_ref[0]
    ubuf[0:SUBLANES, :] = carry[...]
    ubuf[SUBLANES:SUBLANES + tl, :] = u
    u_prev = ubuf[SUBLANES - 1:SUBLANES - 1 + tl, :]
    carry[...] = u[tl - SUBLANES:tl, :]
    last_tile, last_row = (l_real - 1) // tl, (l_real - 1) % tl

    @pl.when(ti == last_tile)
    def _():
        sho_ref[0] = u[last_row:last_row + 1, :]

    um = u + (u_prev - u) * mu_ref[...]
    r = um[:, 0:W_MIX]
    k = um[:, W_MIX:2 * W_MIX]
    v = um[:, 2 * W_MIX:3 * W_MIX]
    wa = um[:, 3 * W_MIX:3 * W_MIX + R_W + R_A]
    gd = um[:, 3 * W_MIX + R_W + R_A:]
    lane = lax.broadcasted_iota(jnp.int32, wa.shape, 1)
    za = _mm(jnp.where(lane < R_W, jnp.tanh(wa), wa), w2a_ref[...])
    wl = (-math.exp(-0.5)) * _sigmoid(w0_ref[...] + za[:, 0:W_MIX])
    a = _sigmoid(a0_ref[...] + za[:, W_MIX:])
    g_s[...] = _mm(_sigmoid(gd), g2_ref[...])
    e = e_ref[...]
    kk = k * kk_ref[...]
    kkn = kk * lax.rsqrt(_headsum(kk * kk, e) + 1e-12)
    k2 = k * (1.0 + (a - 1.0) * ka_ref[...])
    bon_s[...] = _headsum(r * k2 * rk_ref[...], e) * v
    valid = (ti * tl + lax.broadcasted_iota(jnp.int32, (tl, 1), 0)) < l_real
    r_s[...] = r
    wl_s[...] = jnp.where(valid, wl, 0.0)
    k_s[...] = jnp.where(valid, k2, 0.0)
    v_s[...] = jnp.where(valid, v, 0.0)
    kk_s[...] = jnp.where(valid, kkn, 0.0)
    b_s[...] = jnp.where(valid, kkn * a, 0.0)

    tril = tril_ref[...]

    hd = range(N_HEADS)
    hsl = [slice(h * HEAD_DIM, (h + 1) * HEAD_DIM) for h in hd]

    nc = tl // t
    grp = min(WKV_GROUP, nc)

    def chunk_matrices(i, carry_):
        cs = [jnp.minimum(i * grp + j, nc - 1) for j in range(grp)]
        tops, bots, vs, bks = [], [], [], []
        for c in cs:
            rows = pl.ds(pl.multiple_of(c * t, t), t)
            wlc = wl_s[rows, :]
            hi, mid, lo = _split3(wlc)
            cw = (jnp.dot(tril, hi, preferred_element_type=F32) + jnp.dot(tril, mid, preferred_element_type=F32)
                  + jnp.dot(tril, lo, preferred_element_type=F32))
            w_inc = jnp.exp(cw)
            w_inv = jnp.exp(-cw)
            w_prev = jnp.exp(cw - wlc)
            w_t = w_inc[t - 1:t, :]
            top = jnp.concatenate([kk_s[rows, :] * w_prev, r_s[rows, :] * w_inc], axis=0).astype(BF16)
            bot = jnp.concatenate([b_s[rows, :] * w_inv, k_s[rows, :] * w_inv], axis=0)
            bk = (bot * w_t).astype(BF16)
            bot = bot.astype(BF16)
            vc = v_s[rows, :].astype(BF16)
            tops += [top[:, hsl[h]] for h in hd]
            bots += [bot[:, hsl[h]] for h in hd]
            vs += [vc[:, hsl[h]] for h in hd]
            bks += [bk[:, hsl[h]] for h in hd]
            wt_s[c] = jnp.broadcast_to(w_t, (SUBLANES, W_MIX))
        ginv, a_r, av = _wkv_chunk_matrices(tops, bots, vs, t)
        for j, c in enumerate(cs):
            for h in hd:
                k = j * N_HEADS + h
                top_s[c, h] = tops[k]
                bk_s[c, h] = bks[k]
                vh_s[c, h] = vs[k]
                av_s[c, h] = av[k]
                gi_s[c, h] = ginv[k].astype(BF16)
                ar_s[c, h] = a_r[k].astype(BF16)
        return carry_

    def chunk_state(c, carry_):
        rows = pl.ds(pl.multiple_of(c * t, t), t)
        pp = [_mm_nt(top_s[c, h], s_scr[h]) for h in hd]
        u = [_mm(gi_s[c, h], -(pp[h][:t] + av_s[c, h])) for h in hd]
        uv = [jnp.concatenate([u[h].astype(BF16), vh_s[c, h]], axis=0) for h in hd]
        y = [pp[h][t:] + _mm(ar_s[c, h], uv[h]) for h in hd]
        w_t = wt_s[c]
        for h in hd:
            s_scr[h] = s_scr[h] * w_t[0:1, hsl[h]] + _mm_tn(uv[h], bk_s[c, h])
        y_s[rows, :] = jnp.concatenate(y, axis=1)
        return carry_

    lax.fori_loop(0, -(-nc // grp), chunk_matrices, 0)
    lax.fori_loop(0, nc, chunk_state, 0)

    y = y_s[...]
    yc = y - _headsum(y, e) * (1.0 / HEAD_DIM)
    var = _headsum(yc * yc, e) * (1.0 / HEAD_DIM)
    yn = yc * lax.rsqrt(var + LN_EPS)
    y_ref[0] = ((yn * lnw_ref[...] + lnb_ref[...] + bon_s[...]) * g_s[...]).astype(BF16)

    @pl.when(ti == nt - 1)
    def _():
        so_ref[0] = s_scr[...]


def _rwkv(ua, shift0, wkv0, prm, l_real):
    b, lp, _ = ua.shape
    t = min(WKV_CHUNK, lp)
    tl = _largest_tile(lp, TIME_TILE, t)
    nt = lp // tl
    tril = (jnp.arange(t)[:, None] >= jnp.arange(t)[None, :]).astype(BF16)
    consts = [prm["mu"], prm["w0"], prm["a0"], prm["w2a"], prm["g2"], prm["kk"], prm["ka"], prm["rk"],
              prm["lnw"], prm["lnb"], prm["e"], tril]
    big = lambda: pltpu.VMEM((tl, W_MIX), F32)
    nc = tl // t
    per_head = lambda r, c, dt: pltpu.VMEM((nc, N_HEADS, r, c), dt)
    chunk_scratch = [per_head(2 * t, HEAD_DIM, BF16), per_head(2 * t, HEAD_DIM, BF16),
                     per_head(t, HEAD_DIM, BF16), per_head(t, HEAD_DIM, F32), per_head(t, t, BF16),
                     per_head(t, 2 * t, BF16), pltpu.VMEM((nc, SUBLANES, W_MIX), F32)]
    return pl.pallas_call(
        functools.partial(_rwkv_kernel, tl=tl, t=t, l_real=l_real, nt=nt),
        grid=(b, nt),
        in_specs=[pl.BlockSpec((1, tl, A_COLS), lambda i, j: (i, j, 0)),
                  pl.BlockSpec((1, 1, A_COLS), lambda i, j: (i, 0, 0)),
                  pl.BlockSpec((1, N_HEADS, HEAD_DIM, HEAD_DIM), lambda i, j: (i, 0, 0, 0))]
                 + [_const_spec(c.shape) for c in consts],
        out_specs=[pl.BlockSpec((1, tl, W_MIX), lambda i, j: (i, j, 0)),
                   pl.BlockSpec((1, 1, A_COLS), lambda i, j: (i, 0, 0)),
                   pl.BlockSpec((1, N_HEADS, HEAD_DIM, HEAD_DIM), lambda i, j: (i, 0, 0, 0))],
        out_shape=[jax.ShapeDtypeStruct((b, lp, W_MIX), BF16),
                   jax.ShapeDtypeStruct((b, 1, A_COLS), F32),
                   jax.ShapeDtypeStruct((b, N_HEADS, HEAD_DIM, HEAD_DIM), F32)],
        scratch_shapes=[pltpu.VMEM((tl + SUBLANES, A_COLS), F32), pltpu.VMEM((SUBLANES, A_COLS), F32),
                        pltpu.VMEM((N_HEADS, HEAD_DIM, HEAD_DIM), F32)] + [big() for _ in range(9)]
                       + chunk_scratch,
        compiler_params=_cparams("parallel", "arbitrary"),
        name="rwkv",
    )(ua, shift0, wkv0, *consts)


def _ffn_kernel(*refs, tl, tf, dgrp, l_real, mixer_out):
    if mixer_out:
        x_ref, ya_ref, yb_ref, wo_ref = refs[:4]
        refs = refs[4:]
        y_ab = jnp.concatenate([ya_ref[0], yb_ref[0]], axis=1)
        x = x_ref[0] + jnp.dot(y_ab, wo_ref[...], preferred_element_type=F32)
    else:
        x = refs[0][0]
        refs = refs[1:]
    g_ref, wu_ref, cw_ref, cb_ref, wd_ref, buf_ref, o_ref, st_ref, up_scr, act_scr, carry_scr = refs
    ti = pl.program_id(1)
    h = _rms(x, g_ref[...]).astype(BF16)
    last_tile, last_row = (l_real - 1) // tl, (l_real - 1) % tl

    @pl.when(ti == 0)
    def _():
        carry_scr[SUBLANES - (CONV_W - 1):SUBLANES, :] = buf_ref[0]

    nf = D_FF // tf
    col = lambda part, f: slice(part * D_FF + f * tf, part * D_FF + (f + 1) * tf)

    def up(f):
        for part in range(2):
            cols = col(part, f)
            up_scr[0:SUBLANES, cols] = carry_scr[:, cols]
            up_scr[SUBLANES:SUBLANES + tl, cols] = jnp.dot(h, wu_ref[:, cols], preferred_element_type=F32)
            carry_scr[:, cols] = up_scr[tl:tl + SUBLANES, cols]

    up(0)
    out = x
    g0 = 0
    for f in range(nf):
        if f + 1 < nf:
            up(f + 1)
        z = [cb_ref[:, col(part, f)]
             + cw_ref[0:1, col(part, f)] * up_scr[SUBLANES - 2:SUBLANES - 2 + tl, col(part, f)]
             + cw_ref[1:2, col(part, f)] * up_scr[SUBLANES - 1:SUBLANES - 1 + tl, col(part, f)]
             + cw_ref[2:3, col(part, f)] * up_scr[SUBLANES:SUBLANES + tl, col(part, f)]
             for part in range(2)]
        val, gate = z
        act_scr[:, f * tf:(f + 1) * tf] = (gate * _sigmoid(gate) * val).astype(BF16)
        if (f + 1) % dgrp == 0 or f + 1 == nf:
            g1 = (f + 1) * tf
            out = out + jnp.dot(act_scr[:, g0:g1], wd_ref[g0:g1, :], preferred_element_type=F32)
            g0 = g1
    o_ref[0] = out

    @pl.when(ti == last_tile)
    def _():
        st_ref[0] = up_scr[SUBLANES + last_row - 1:SUBLANES + last_row + 1, :]


def _ffn(x, g, wu, cw, cb, wd, buf, l_real, mixer=None):
    b, lp, _ = x.shape
    tl = _largest_tile(lp, FFN_TILE, SUBLANES)
    tf, dgrp = 256, 4
    assert l_real >= CONV_W - 1 and (l_real - 1) % tl >= 1
    row = lambda c: pl.BlockSpec((1, tl, c), lambda i, j: (i, j, 0))
    mix_args, mix_specs = [], []
    if mixer is not None:
        mix_args = list(mixer)
        mix_specs = [row(W_MIX), row(W_MIX), _const_spec(mixer[2].shape)]
    return pl.pallas_call(
        functools.partial(_ffn_kernel, tl=tl, tf=tf, dgrp=dgrp, l_real=l_real, mixer_out=mixer is not None),
        grid=(b, lp // tl),
        in_specs=[row(D_MODEL)] + mix_specs
                 + [_const_spec(g.shape), _const_spec(wu.shape), _const_spec(cw.shape), _const_spec(cb.shape),
                    _const_spec(wd.shape),
                    pl.BlockSpec((1, CONV_W - 1, 2 * D_FF), lambda i, j: (i, 0, 0))],
        out_specs=[pl.BlockSpec((1, tl, D_MODEL), lambda i, j: (i, j, 0)),
                   pl.BlockSpec((1, CONV_W - 1, 2 * D_FF), lambda i, j: (i, 0, 0))],
        out_shape=[jax.ShapeDtypeStruct((b, lp, D_MODEL), F32),
                   jax.ShapeDtypeStruct((b, CONV_W - 1, 2 * D_FF), F32)],
        scratch_shapes=[pltpu.VMEM((tl + SUBLANES, 2 * D_FF), F32), pltpu.VMEM((tl, D_FF), BF16),
                        pltpu.VMEM((SUBLANES, 2 * D_FF), F32)],
        compiler_params=_cparams("parallel", "arbitrary"),
        name="ffn",
    )(x, *mix_args, g, wu, cw, cb, wd, buf)


def _s5_kernel(x_ref, g_ref, wbr_ref, wbi_ref, wcr_ref, wci_ref, are_ref, aim_ref, d_ref, wg_ref,
               perm_ref, permt_ref, s0r_ref, s0i_ref, o_ref, sor_ref, soi_ref, bu_scr, st_scr, *, tt, l_real):
    ti = pl.program_id(1)
    nb = S5_BATCH
    m = nb * tt
    ntile = S5_STATE // LANES

    @pl.when(ti == 0)
    def _():
        for c in range(ntile):
            st_scr[c] = s0r_ref[:, c * LANES:(c + 1) * LANES]
            st_scr[ntile + c] = s0i_ref[:, c * LANES:(c + 1) * LANES]

    x = x_ref[...].reshape(m, D_MODEL)
    perm = perm_ref[...]
    u_hi, u_lo = _split2(_rms(x, g_ref[...]))
    ub_f = jnp.dot(perm, u_hi, preferred_element_type=F32)
    ub = ub_f.astype(BF16)
    u = ub_f + jnp.dot(perm, u_lo, preferred_element_type=F32)
    tpq = LANES // GC * P_C // LANES
    for q in range(D_MODEL // LANES):
        uq = ub[:, q * LANES:(q + 1) * LANES]
        br = jnp.dot(uq, wbr_ref[q], preferred_element_type=F32)
        bi = jnp.dot(uq, wbi_ref[q], preferred_element_type=F32)
        for c in range(tpq):
            bu_scr[q * tpq + c] = br[:, c * LANES:(c + 1) * LANES]
            bu_scr[ntile + q * tpq + c] = bi[:, c * LANES:(c + 1) * LANES]

    grp = 4
    for cb in range(ntile // grp):
        tiles = list(range(cb * grp, (cb + 1) * grp))
        a_re = [jnp.broadcast_to(are_ref[:, c * LANES:(c + 1) * LANES], (nb, LANES)) for c in tiles]
        a_im = [jnp.broadcast_to(aim_ref[:, c * LANES:(c + 1) * LANES], (nb, LANES)) for c in tiles]
        s_re = [st_scr[c] for c in tiles]
        s_im = [st_scr[ntile + c] for c in tiles]
        for t in range(tt):
            rows = slice(t * nb, (t + 1) * nb)
            for i, c in enumerate(tiles):
                n_re = a_re[i] * s_re[i] - a_im[i] * s_im[i] + bu_scr[c, rows, :]
                n_im = a_re[i] * s_im[i] + a_im[i] * s_re[i] + bu_scr[ntile + c, rows, :]
                bu_scr[c, rows, :] = n_re
                bu_scr[ntile + c, rows, :] = n_im
                s_re[i], s_im[i] = n_re, n_im
        for i, c in enumerate(tiles):
            st_scr[c] = s_re[i]
            st_scr[ntile + c] = s_im[i]

    ys = []
    for q in range(D_MODEL // LANES):
        sr = jnp.concatenate([bu_scr[q * tpq + c] for c in range(tpq)], axis=1).astype(BF16)
        si = jnp.concatenate([bu_scr[ntile + q * tpq + c] for c in range(tpq)], axis=1).astype(BF16)
        ys.append(jnp.dot(sr, wcr_ref[q], preferred_element_type=F32)
                  + jnp.dot(si, wci_ref[q], preferred_element_type=F32))
    yc = jnp.concatenate(ys, axis=1) + d_ref[...] * u
    z = jnp.dot(_gelu(yc).astype(BF16), wg_ref[...], preferred_element_type=F32)
    mix = z[:, 0:D_MODEL] * _sigmoid(z[:, D_MODEL:])
    o_ref[...] = (x + _mm_exact_lhs(permt_ref[...], mix)).reshape(nb, tt, D_MODEL)
    last_tile, last_row = (l_real - 1) // tt, (l_real - 1) % tt

    @pl.when(ti == last_tile)
    def _():
        rows = slice(last_row * nb, (last_row + 1) * nb)
        for c in range(ntile):
            sor_ref[:, c * LANES:(c + 1) * LANES] = bu_scr[c, rows, :]
            soi_ref[:, c * LANES:(c + 1) * LANES] = bu_scr[ntile + c, rows, :]


def _s5(x, g, sp, s0r, s0i, l_real):
    b, lp, _ = x.shape
    assert b % S5_BATCH == 0
    tt = _largest_tile(lp, 32, SUBLANES)
    m = S5_BATCH * tt
    dst = jnp.arange(m)
    src = (dst % S5_BATCH) * tt + dst // S5_BATCH
    perm = (src[:, None] == jnp.arange(m)[None, :]).astype(BF16)
    consts = [g, sp["wbr"], sp["wbi"], sp["wcr"], sp["wci"], sp["are"], sp["aim"], sp["d"], sp["wglu"],
              perm, perm.T]
    st = pl.BlockSpec((S5_BATCH, S5_STATE), lambda i, j: (i, 0))
    return pl.pallas_call(
        functools.partial(_s5_kernel, tt=tt, l_real=l_real),
        grid=(b // S5_BATCH, lp // tt),
        in_specs=[pl.BlockSpec((S5_BATCH, tt, D_MODEL), lambda i, j: (i, j, 0))]
                 + [_const_spec(c.shape) for c in consts] + [st, st],
        out_specs=[pl.BlockSpec((S5_BATCH, tt, D_MODEL), lambda i, j: (i, j, 0)), st, st],
        out_shape=[jax.ShapeDtypeStruct((b, lp, D_MODEL), F32),
                   jax.ShapeDtypeStruct((b, S5_STATE), F32), jax.ShapeDtypeStruct((b, S5_STATE), F32)],
        scratch_shapes=[pltpu.VMEM((2 * S5_STATE // LANES, m, LANES), F32),
                        pltpu.VMEM((2 * S5_STATE // LANES, S5_BATCH, LANES), F32)],
        compiler_params=_cparams("parallel", "arbitrary"),
        name="s5",
    )(x, *consts, s0r, s0i)


def _prep_params(p):
    j = 0
    head_of = jnp.arange(W_MIX) // HEAD_DIM
    e = (head_of[:, None] == head_of[None, :]).astype(BF16)
    w_in = p["w_in_e"][j]
    wf = jnp.zeros((D_MODEL, LANES), F32).at[:, :N_HEADS].set(w_in[:, A_COLS + 3 * W_MIX:])
    wb = jnp.concatenate([w_in[:, A_COLS:A_COLS + 3 * W_MIX], wf], axis=1).astype(BF16)
    fb = jnp.zeros((1, LANES), F32).at[0, :N_HEADS].set(p["b_fbias"][j])
    w2a = jnp.zeros((R_W + R_A, 2 * W_MIX), F32)
    w2a = w2a.at[:R_W, :W_MIX].set(p["a_w2"][j]).at[R_W:, W_MIX:].set(p["a_a2"][j])
    row = lambda a: a.reshape(1, -1).astype(F32)
    rw = dict(mu=row(p["a_mu"][j]), w0=row(p["a_w0"][j]), a0=row(p["a_a0"][j]), w2a=w2a.astype(BF16),
              g2=p["a_g2"][j].astype(BF16), kk=row(p["a_kk"][j]), ka=row(p["a_ka"][j]),
              rk=row(p["a_rk"][j]), lnw=row(p["a_ln_w"][j]), lnb=row(p["a_ln_b"][j]), e=e)
    hm = (jnp.arange(N_HEADS)[:, None] == head_of[None, :]).astype(BF16)
    l0 = dict(wa=w_in[:, :A_COLS].astype(BF16), wb=wb, fb=fb, e=e, hm=hm,
              qg=row(jnp.tile(p["b_qnorm"][j], N_HEADS)), kg=row(jnp.tile(p["b_knorm"][j], N_HEADS)),
              w_out=p["w_out_e"][j].astype(BF16))
    lam_re, lam_im = p["c_lam_re"][j].astype(F32), p["c_lam_im"][j].astype(F32)
    dt = jnp.exp(p["c_log_dt"][j].astype(F32))[:, None]
    mag = jnp.exp(lam_re * dt)
    ab_re, ab_im = mag * jnp.cos(lam_im * dt), mag * jnp.sin(lam_im * dt)
    den = lam_re * lam_re + lam_im * lam_im
    f_re = ((ab_re - 1.0) * lam_re + ab_im * lam_im) / den
    f_im = (ab_im * lam_re - (ab_re - 1.0) * lam_im) / den
    b_re, b_im = p["c_b_re"][j], p["c_b_im"][j]
    bb_re = f_re[..., None] * b_re - f_im[..., None] * b_im
    bb_im = f_re[..., None] * b_im + f_im[..., None] * b_re
    gpt = LANES // GC
    eye = jnp.eye(gpt, dtype=F32)

    def in_layout(bb):
        t = bb.reshape(G_C // gpt, gpt, P_C, GC).transpose(0, 1, 3, 2)
        return jnp.einsum("qgcp,gh->qgchp", t, eye).reshape(G_C // gpt, gpt * GC, gpt * P_C).astype(BF16)

    def out_layout(cc):
        t = cc.reshape(G_C // gpt, gpt, GC, P_C).transpose(0, 1, 3, 2)
        return jnp.einsum("qgpc,gh->qgphc", t, eye).reshape(G_C // gpt, gpt * P_C, gpt * GC).astype(BF16)

    s5 = dict(wbr=in_layout(bb_re), wbi=in_layout(bb_im), wcr=out_layout(p["c_c_re"][j]),
              wci=out_layout(-p["c_c_im"][j]), are=ab_re.reshape(1, S5_STATE), aim=ab_im.reshape(1, S5_STATE),
              d=row(p["c_d"][j]), wglu=p["w_glu"][j].astype(BF16))
    ffn = [dict(g=row(p["norm_ffn"][i]), wu=p["w_up"][i].astype(BF16), cw=p["conv_w"][i].astype(F32),
                cb=row(p["conv_b"][i]), wd=p["w_down"][i].astype(BF16)) for i in range(2)]
    return dict(l0=l0, rw=rw, s5=s5, ffn=ffn, g_mix=[row(p["norm_mix"][i]) for i in range(2)])


def _trunk(x, l_real, init, pp, past):
    b, lp, _ = x.shape
    l0 = pp["l0"]
    ua, q, kf, kb, vf, vb, lf = _inproj(x, l_real, pp["g_mix"][0], l0["wa"], l0["wb"], l0["e"], l0["qg"],
                                        l0["kg"], l0["fb"])
    y_a, a_shift, a_wkv = _rwkv(ua, init["a_shift"].reshape(b, 1, A_COLS), init["a_wkv"], pp["rw"], l_real)
    lf_row = jnp.transpose(lf[:, :, :N_HEADS], (0, 2, 1)).reshape(b * N_HEADS, lp)
    if past is None:
        lpad = -lp % LANES
        c_row = _cumsum_lanes(jnp.pad(lf_row, ((0, 0), (0, lpad))))[:, :lp].reshape(b, N_HEADS, lp)
        cq = jnp.pad(jnp.transpose(c_row, (0, 2, 1)), ((0, 0), (0, 0), (0, LANES - N_HEADS)))
        y_b = _attn_prompt(q, kb, vb, cq, c_row, min(ATTN_TILE, lp))
    else:
        pk, pv, plf = past
        plen = pk.shape[1]
        plf_row = jnp.transpose(plf, (0, 2, 1)).reshape(b * N_HEADS, plen)
        c_all = _cumsum_lanes(jnp.concatenate([plf_row, jnp.pad(lf_row, ((0, 0), (0, -lp % LANES)))], axis=1))
        c_past = c_all[:, :plen].reshape(b, N_HEADS, plen)
        c_new = c_all[:, plen:plen + lp].reshape(b, N_HEADS, lp)
        cq_stack = jnp.broadcast_to(c_new.reshape(b, N_HEADS * lp, 1), (b, N_HEADS * lp, LANES))
        y_b = _attn_sample(q, cq_stack, pk.reshape(b, plen, W_MIX), pv.reshape(b, plen, W_MIX), c_past,
                           kb, vb, c_new, l0["hm"])
    f0 = pp["ffn"][0]
    x2, conv0 = _ffn(x, f0["g"], f0["wu"], f0["cw"], f0["cb"], f0["wd"], init["ffn_conv"][0], l_real,
                     mixer=(y_a, y_b, l0["w_out"]))
    x3, c_re, c_im = _s5(x2, pp["g_mix"][1], pp["s5"], init["c_re"].reshape(b, S5_STATE),
                         init["c_im"].reshape(b, S5_STATE), l_real)
    f1 = pp["ffn"][1]
    x4, conv1 = _ffn(x3, f1["g"], f1["wu"], f1["cw"], f1["cb"], f1["wd"], init["ffn_conv"][1], l_real)
    states = (a_shift.reshape(1, b, A_COLS), a_wkv[None], kf[None], vf[None],
              lf[None, :, :l_real, :N_HEADS], c_re.reshape(1, b, G_C, P_C), c_im.reshape(1, b, G_C, P_C),
              jnp.stack([conv0, conv1], axis=0))
    return x4, states


def kernel(x_prompt, x_sample, state_a_shift, state_a_wkv, cache_b_k, cache_b_v, cache_b_logf, state_c_re, state_c_im, state_ffn_conv, meta, norm_mix, norm_ffn, w_in_e, a_mu, a_w0, a_w2, a_a0, a_a2, a_g2, a_kk, a_ka, a_rk, a_ln_w, a_ln_b, b_fbias, b_qnorm, b_knorm, w_out_e, c_lam_re, c_lam_im, c_log_dt, c_b_re, c_b_im, c_c_re, c_c_im, c_d, w_glu, w_up, conv_w, conv_b, w_down):
    pp = _prep_params(dict(
        norm_mix=norm_mix, norm_ffn=norm_ffn, w_in_e=w_in_e, a_mu=a_mu, a_w0=a_w0, a_w2=a_w2, a_a0=a_a0,
        a_a2=a_a2, a_g2=a_g2, a_kk=a_kk, a_ka=a_ka, a_rk=a_rk, a_ln_w=a_ln_w, a_ln_b=a_ln_b,
        b_fbias=b_fbias, b_qnorm=b_qnorm, b_knorm=b_knorm, w_out_e=w_out_e, c_lam_re=c_lam_re,
        c_lam_im=c_lam_im, c_log_dt=c_log_dt, c_b_re=c_b_re, c_b_im=c_b_im, c_c_re=c_c_re, c_c_im=c_c_im,
        c_d=c_d, w_glu=w_glu, w_up=w_up, conv_w=conv_w, conv_b=conv_b, w_down=w_down))
    b, seq, _ = x_prompt.shape
    n_meta = meta.shape[0]
    lr = n_meta + seq
    lp = -(-lr // WKV_CHUNK) * WKV_CHUNK
    xp = jnp.concatenate([jnp.broadcast_to(meta.astype(F32)[None], (b, n_meta, D_MODEL)), x_prompt,
                          jnp.zeros((b, lp - lr, D_MODEL), F32)], axis=1)
    init_p = dict(a_shift=jnp.zeros((b, A_COLS), F32),
                  a_wkv=jnp.zeros((b, N_HEADS, HEAD_DIM, HEAD_DIM), F32),
                  c_re=jnp.zeros((b, S5_STATE), F32), c_im=jnp.zeros((b, S5_STATE), F32),
                  ffn_conv=jnp.zeros((2, b, CONV_W - 1, 2 * D_FF), F32))
    yp, st_p = _trunk(xp, lr, init_p, pp, None)
    bs, ls, _ = x_sample.shape
    init_s = dict(a_shift=state_a_shift[0], a_wkv=state_a_wkv[0], c_re=state_c_re[0], c_im=state_c_im[0],
                  ffn_conv=state_ffn_conv)
    past = (cache_b_k[0], cache_b_v[0], cache_b_logf[0])
    ys, st_s = _trunk(x_sample, ls, init_s, pp, past)
    return (yp[:, n_meta:lr], ys, *st_p, *st_s)
```

```python
import functools
import math

import jax
import jax.numpy as jnp
from jax import lax
from jax.experimental import pallas as pl
from jax.experimental.pallas import tpu as pltpu

F32, BF16 = jnp.float32, jnp.bfloat16

D_MODEL = 1024
N_HEADS = 8
HEAD_DIM = 64
W_MIX = N_HEADS * HEAD_DIM
R_W, R_A, R_G = 64, 64, 128
A_COLS = 3 * W_MIX + R_W + R_A + R_G
B_PAD_COLS = 3 * W_MIX + 128
D_FF = 2816
CONV_W = 3
G_C, GC, P_C = 64, 16, 64
S5_STATE = G_C * P_C
EPS = 1e-6
LN_EPS = 64e-5
NEG = -1e30
LOG2E = math.log2(math.e)

LANES = 128
SUBLANES = 8
VMEM_LIMIT = 56 * 1024 * 1024
WKV_CHUNK = 64
WKV_SUB = 16
WKV_GROUP = 3
S5_BATCH = 8
TIME_TILE = 704
FFN_TILE = 528
INPROJ_TILE = 352
ATTN_TILE = 768


def _cparams(*sem):
    return pltpu.CompilerParams(dimension_semantics=sem, vmem_limit_bytes=VMEM_LIMIT)


def _const_spec(shape):
    nd = len(shape)
    return pl.BlockSpec(shape, lambda *_: (0,) * nd, pipeline_mode=pl.Buffered(1))


def _largest_tile(n, cap, mult):
    best = None
    for t in range(mult, min(n, cap) + 1, mult):
        if n % t == 0:
            best = t
    assert best is not None, (n, cap, mult)
    return best


def _mm(a, b):
    return jnp.dot(a.astype(BF16), b.astype(BF16), preferred_element_type=F32)


def _mm_nt(a, b):
    return lax.dot_general(a.astype(BF16), b.astype(BF16), (((1,), (1,)), ((), ())),
                           preferred_element_type=F32)


def _mm_tn(a, b):
    return lax.dot_general(a.astype(BF16), b.astype(BF16), (((0,), (0,)), ((), ())),
                           preferred_element_type=F32)


def _split2(x):
    hi = x.astype(BF16)
    lo = (x - hi.astype(F32)).astype(BF16)
    return hi, lo


def _split3(x):
    hi = x.astype(BF16)
    r = x - hi.astype(F32)
    mid = r.astype(BF16)
    lo = (r - mid.astype(F32)).astype(BF16)
    return hi, mid, lo


def _mm_exact_rhs(x, m):
    hi, mid, lo = _split3(x)
    return (jnp.dot(hi, m, preferred_element_type=F32) + jnp.dot(mid, m, preferred_element_type=F32)
            + jnp.dot(lo, m, preferred_element_type=F32))


def _mm_exact_lhs(m, x):
    hi, mid, lo = _split3(x)
    return (jnp.dot(m, hi, preferred_element_type=F32) + jnp.dot(m, mid, preferred_element_type=F32)
            + jnp.dot(m, lo, preferred_element_type=F32))


def _headsum(x, e):
    return jnp.dot(x.astype(BF16), e, preferred_element_type=F32)


def _rms(x, g):
    return x * lax.rsqrt(jnp.mean(x * x, axis=-1, keepdims=True) + EPS) * g


def _sigmoid(x):
    return 1.0 / (1.0 + jnp.exp(-x))


def _log_sigmoid(x):
    return jnp.minimum(x, 0.0) - jnp.log(1.0 + jnp.exp(-jnp.abs(x)))


def _gelu(x):
    return 0.5 * x * (1.0 + jnp.tanh(math.sqrt(2.0 / math.pi) * (x + 0.044715 * (x * x * x))))


def _inproj_kernel(x_ref, g_ref, wa_ref, wb_ref, e_ref, qg_ref, kg_ref, fb_ref,
                   ua_ref, q_ref, kf_ref, kb_ref, vf_ref, vb_ref, lf_ref):
    h = _rms(x_ref[0], g_ref[...]).astype(BF16)
    ua_ref[0] = jnp.dot(h, wa_ref[...], preferred_element_type=F32)
    ub = jnp.dot(h, wb_ref[...], preferred_element_type=F32)
    q = ub[:, 0:W_MIX]
    k = ub[:, W_MIX:2 * W_MIX]
    v = ub[:, 2 * W_MIX:3 * W_MIX]
    f = ub[:, 3 * W_MIX:]
    e = e_ref[...]
    qn = q * lax.rsqrt(_headsum(q * q, e) * (1.0 / HEAD_DIM) + EPS) * qg_ref[...]
    kn = k * lax.rsqrt(_headsum(k * k, e) * (1.0 / HEAD_DIM) + EPS) * kg_ref[...]
    tm = q.shape[0]
    q_ref[0] = (qn * (HEAD_DIM ** -0.5 * LOG2E)).astype(BF16)
    kf_ref[0] = kn.reshape(tm, N_HEADS, HEAD_DIM)
    kb_ref[0] = kn.astype(BF16)
    vf_ref[0] = v.reshape(tm, N_HEADS, HEAD_DIM)
    vb_ref[0] = v.astype(BF16)
    lf = _log_sigmoid(f + fb_ref[...])
    lane = lax.broadcasted_iota(jnp.int32, lf.shape, 1)
    lf_ref[0] = jnp.where(lane < N_HEADS, lf, 0.0)


def _inproj(x, l_real, g, wa, wb, e, qg, kg, fb):
    b, lp, _ = x.shape
    tm = _largest_tile(lp, INPROJ_TILE, SUBLANES)
    row = lambda c: pl.BlockSpec((1, tm, c), lambda i, j: (i, j, 0))
    cache = pl.BlockSpec((1, tm, N_HEADS, HEAD_DIM), lambda i, j: (i, j, 0, 0))
    rowshape = lambda c, dt: jax.ShapeDtypeStruct((b, lp, c), dt)
    cacheshape = jax.ShapeDtypeStruct((b, l_real, N_HEADS, HEAD_DIM), F32)
    return pl.pallas_call(
        _inproj_kernel,
        grid=(b, lp // tm),
        in_specs=[row(D_MODEL), _const_spec(g.shape), _const_spec(wa.shape), _const_spec(wb.shape),
                  _const_spec(e.shape), _const_spec(qg.shape), _const_spec(kg.shape), _const_spec(fb.shape)],
        out_specs=[row(A_COLS), row(W_MIX), cache, row(W_MIX), cache, row(W_MIX), row(LANES)],
        out_shape=[rowshape(A_COLS, F32), rowshape(W_MIX, BF16), cacheshape, rowshape(W_MIX, BF16),
                   cacheshape, rowshape(W_MIX, BF16), rowshape(LANES, F32)],
        compiler_params=_cparams("parallel", "parallel"),
        name="inproj",
    )(x, g, wa, wb, e, qg, kg, fb)


def _cumsum_kernel(x_ref, tri_ref, o_ref, *, nblk):
    tri = tri_ref[...]
    carry = jnp.zeros((x_ref.shape[0], 1), F32)
    for j in range(nblk):
        cs = _mm_exact_rhs(x_ref[:, j * LANES:(j + 1) * LANES], tri) + carry
        o_ref[:, j * LANES:(j + 1) * LANES] = cs * LOG2E
        carry = cs[:, LANES - 1:LANES]


def _cumsum_lanes(x):
    rows, n = x.shape
    assert n % LANES == 0
    tri = (jnp.arange(LANES)[:, None] <= jnp.arange(LANES)[None, :]).astype(BF16)
    return pl.pallas_call(
        functools.partial(_cumsum_kernel, nblk=n // LANES),
        grid=(1,),
        in_specs=[_const_spec(x.shape), _const_spec(tri.shape)],
        out_specs=_const_spec(x.shape),
        out_shape=jax.ShapeDtypeStruct(x.shape, F32),
        compiler_params=_cparams("arbitrary"),
        name="cumsum",
    )(x, tri)


def _attn_prompt_kernel(q_ref, k_ref, v_ref, cq_ref, cr_ref, o_ref, *, bounds):
    p = pl.program_id(1)
    lane = lax.broadcasted_iota(jnp.int32, (1, LANES), 1)
    head_lanes = (lane < HEAD_DIM, lane >= HEAD_DIM)
    for qi in range(len(bounds) - 1):
        r0, r1 = bounds[qi], bounds[qi + 1]
        q2 = q_ref[0, r0:r1, :]
        cqt = cq_ref[0, r0:r1, :]
        lane_q = lax.broadcasted_iota(jnp.int32, cqt.shape, 1)
        res = []
        for hh in range(2):
            h = 2 * p + hh
            qm = jnp.where(head_lanes[hh], q2, jnp.zeros_like(q2))
            cq = jnp.sum(jnp.where(lane_q == h, cqt, 0.0), axis=-1, keepdims=True)
            ck_all = cr_ref[0, pl.ds(h, 1), :]
            m = jnp.full((r1 - r0, 1), NEG, F32)
            l = jnp.zeros((r1 - r0, 1), F32)
            acc = jnp.zeros((r1 - r0, LANES), F32)

            def block(a0, a1, c0, c1, m, l, acc, diag):
                t = lax.dot_general(qm[a0:a1], k_ref[0, c0:c1, :], (((1,), (1,)), ((), ())),
                                    preferred_element_type=F32) - ck_all[:, c0:c1]
                if diag:
                    ri = lax.broadcasted_iota(jnp.int32, t.shape, 0) + (r0 + a0)
                    ci = lax.broadcasted_iota(jnp.int32, t.shape, 1) + c0
                    t = jnp.where(ri >= ci, t, NEG)
                m_new = jnp.maximum(m, cq[a0:a1] + jnp.max(t, axis=-1, keepdims=True))
                alpha = jnp.exp2(m - m_new)
                pe = jnp.exp2(t + (cq[a0:a1] - m_new))
                l = alpha * l + jnp.sum(pe, axis=-1, keepdims=True)
                acc = alpha * acc + jnp.dot(pe.astype(BF16), v_ref[0, c0:c1, :], preferred_element_type=F32)
                return m_new, l, acc

            for ki in range(qi):
                m, l, acc = block(0, r1 - r0, bounds[ki], bounds[ki + 1], m, l, acc, False)
            half = ((r1 - r0) // 2) // LANES * LANES
            if half == 0:
                m, l, acc = block(0, r1 - r0, r0, r1, m, l, acc, True)
            else:
                m, l, acc = block(0, r1 - r0, r0, r0 + half, m, l, acc, True)
                mb, lb, ab = block(half, r1 - r0, r0 + half, r1, m[half:], l[half:], acc[half:], True)
                m = jnp.concatenate([m[:half], mb], axis=0)
                l = jnp.concatenate([l[:half], lb], axis=0)
                acc = jnp.concatenate([acc[:half], ab], axis=0)
            res.append(acc / l)
        o_ref[0, r0:r1, :] = jnp.where(head_lanes[0], res[0], res[1]).astype(BF16)


def _attn_prompt(q, k, v, cq, cr, tile):
    b, lp, _ = q.shape
    bounds = tuple(range(0, lp, tile)) + (lp,)
    col = pl.BlockSpec((1, lp, LANES), lambda i, p: (i, 0, p))
    return pl.pallas_call(
        functools.partial(_attn_prompt_kernel, bounds=bounds),
        grid=(b, N_HEADS // 2),
        in_specs=[col, col, col,
                  pl.BlockSpec((1, lp, LANES), lambda i, p: (i, 0, 0)),
                  pl.BlockSpec((1, N_HEADS, lp), lambda i, p: (i, 0, 0))],
        out_specs=col,
        out_shape=jax.ShapeDtypeStruct((b, lp, W_MIX), BF16),
        compiler_params=_cparams("parallel", "arbitrary"),
        name="attn_prompt",
    )(q, k, v, cq, cr)


def _attn_sample_kernel(q_ref, cq_ref, kp_ref, vp_ref, cp_ref, kn_ref, vn_ref, cn_ref, hm_ref, o_ref,
                        qx_scr, m_scr, l_scr, acc_scr, *, nkp, ls):
    j = pl.program_id(1)
    rows = N_HEADS * ls

    @pl.when(j == 0)
    def _():
        q = q_ref[0]
        qx = jnp.broadcast_to(q[None], (N_HEADS, ls, W_MIX)) * hm_ref[...][:, None, :]
        qx_scr[...] = qx.reshape(rows, W_MIX)
        m_scr[...] = jnp.full(m_scr.shape, NEG, F32)
        l_scr[...] = jnp.zeros(l_scr.shape, F32)
        acc_scr[...] = jnp.zeros(acc_scr.shape, F32)

    def update(k, v, ck, causal):
        tk = k.shape[0]
        s = lax.dot_general(qx_scr[...], k, (((1,), (1,)), ((), ())), preferred_element_type=F32)
        ckx = jnp.broadcast_to(ck[:, None, :], (N_HEADS, ls, tk)).reshape(rows, tk)
        s = s + cq_ref[0][:, 0:1] - ckx
        if causal:
            ri = lax.broadcasted_iota(jnp.int32, (N_HEADS, ls, tk), 1).reshape(rows, tk)
            ci = lax.broadcasted_iota(jnp.int32, (rows, tk), 1)
            s = jnp.where(ri >= ci, s, NEG)
        m = m_scr[...]
        m_new = jnp.maximum(m, jnp.max(s, axis=-1, keepdims=True))
        alpha = jnp.exp2(m - m_new)
        pe = jnp.exp2(s - m_new)
        l_scr[...] = alpha * l_scr[...] + jnp.sum(pe, axis=-1, keepdims=True)
        acc_scr[...] = alpha * acc_scr[...] + jnp.dot(pe.astype(BF16), v, preferred_element_type=F32)
        m_scr[...] = m_new

    @pl.when(j < nkp)
    def _():
        update(kp_ref[0].astype(BF16), vp_ref[0].astype(BF16), cp_ref[0], False)

    @pl.when(j == nkp)
    def _():
        update(kn_ref[0], vn_ref[0], cn_ref[0], True)
        o = (acc_scr[...] / l_scr[...]).reshape(N_HEADS, ls, W_MIX) * hm_ref[...].astype(F32)[:, None, :]
        o_ref[0] = jnp.sum(o, axis=0).astype(BF16)


def _attn_sample(q, cq_stack, kp, vp, cp, kn, vn, cn, hm):
    b, ls, _ = q.shape
    past = kp.shape[1]
    tkp = _largest_tile(past, 1024, LANES)
    nkp = past // tkp
    rows = N_HEADS * ls
    pidx = lambda i, j: (i, jnp.minimum(j, nkp - 1), 0)
    return pl.pallas_call(
        functools.partial(_attn_sample_kernel, nkp=nkp, ls=ls),
        grid=(b, nkp + 1),
        in_specs=[pl.BlockSpec((1, ls, W_MIX), lambda i, j: (i, 0, 0)),
                  pl.BlockSpec((1, rows, LANES), lambda i, j: (i, 0, 0)),
                  pl.BlockSpec((1, tkp, W_MIX), pidx),
                  pl.BlockSpec((1, tkp, W_MIX), pidx),
                  pl.BlockSpec((1, N_HEADS, tkp), lambda i, j: (i, 0, jnp.minimum(j, nkp - 1))),
                  pl.BlockSpec((1, ls, W_MIX), lambda i, j: (i, 0, 0)),
                  pl.BlockSpec((1, ls, W_MIX), lambda i, j: (i, 0, 0)),
                  pl.BlockSpec((1, N_HEADS, ls), lambda i, j: (i, 0, 0)),
                  _const_spec(hm.shape)],
        out_specs=pl.BlockSpec((1, ls, W_MIX), lambda i, j: (i, 0, 0)),
        out_shape=jax.ShapeDtypeStruct((b, ls, W_MIX), BF16),
        scratch_shapes=[pltpu.VMEM((rows, W_MIX), BF16), pltpu.VMEM((rows, 1), F32),
                        pltpu.VMEM((rows, 1), F32), pltpu.VMEM((rows, W_MIX), F32)],
        compiler_params=_cparams("parallel", "arbitrary"),
        name="attn_sample",
    )(q, cq_stack, kp, vp, cp, kn, vn, cn, hm)


def _wkv_chunk_matrices(tops, bots, vs, t):
    hd = range(len(tops))
    ri = lax.broadcasted_iota(jnp.int32, (t, t), 0)
    ci = lax.broadcasted_iota(jnp.int32, (t, t), 1)
    eye = (ri == ci).astype(F32)
    r2 = lax.broadcasted_iota(jnp.int32, (t, 2 * t), 0)
    c2 = lax.broadcasted_iota(jnp.int32, (t, 2 * t), 1)
    c2 = jnp.where(c2 >= t, c2 - t, c2)
    same = (ri // WKV_SUB) == (ci // WKV_SUB)
    aa = [_mm_nt(tops[h], bots[h]) for h in hd]
    n = [jnp.where(ri > ci, aa[h][:t, :t], 0.0) for h in hd]
    a_ak = [jnp.where(ri > ci, aa[h][:t, t:], 0.0) for h in hd]
    a_r = [jnp.where(r2 >= c2, aa[h][t:, :], 0.0) for h in hd]
    av = [_mm(a_ak[h], vs[h]) for h in hd]
    d = [jnp.where(same, n[h], 0.0) for h in hd]
    lo = [n[h] - d[h] for h in hd]
    d2 = [_mm(d[h], d[h]) for h in hd]
    x = [_mm(eye - d[h], eye + d2[h]) for h in hd]
    d4 = [_mm(d2[h], d2[h]) for h in hd]
    x = [_mm(x[h], eye + d4[h]) for h in hd]
    d8 = [_mm(d4[h], d4[h]) for h in hd]
    x = [_mm(x[h], eye + d8[h]) for h in hd]
    mb = [_mm(x[h], lo[h]) for h in hd]
    mb2 = [_mm(mb[h], mb[h]) for h in hd]
    xx = [_mm(eye + mb2[h], x[h]) for h in hd]
    ginv = [_mm(eye - mb[h], xx[h]) for h in hd]
    return ginv, a_r, av


def _rwkv_kernel(u_ref, sh_ref, s0_ref, mu_ref, w0_ref, a0_ref, w2a_ref, g2_ref, kk_ref, ka_ref,
                 rk_ref, lnw_ref, lnb_ref, e_ref, tril_ref,
                 y_ref, sho_ref, so_ref,
                 ubuf, carry, s_scr, r_s, wl_s, k_s, v_s, kk_s, b_s, y_s, bon_s, g_s,
                 top_s, bk_s, vh_s, av_s, gi_s, ar_s, wt_s, *, tl, t, l_real, nt):
    ti = pl.program_id(1)

    @pl.when(ti == 0)
    def _():
        carry[...] = jnp.broadcast_to(sh_ref[0], carry.shape)
        s_scr[...] = s0_ref[0]

    u = u_ref[0]
    ubuf[0:SUBLANES, :] = carry[...]
    ubuf[SUBLANES:SUBLANES + tl, :] = u
    u_prev = ubuf[SUBLANES - 1:SUBLANES - 1 + tl, :]
    carry[...] = u[tl - SUBLANES:tl, :]
    um = u + (u_prev - u) * mu_ref[...]
    r = um[:, 0:W_MIX]
    k = um[:, W_MIX:2 * W_MIX]
    v = um[:, 2 * W_MIX:3 * W_MIX]
    wa = um[:, 3 * W_MIX:3 * W_MIX + R_W + R_A]
    gd = um[:, 3 * W_MIX + R_W + R_A:]
    lane = lax.broadcasted_iota(jnp.int32, wa.shape, 1)
    za = _mm(jnp.where(lane < R_W, jnp.tanh(wa), wa), w2a_ref[...])
    wl = (-math.exp(-0.5)) * _sigmoid(w0_ref[...] + za[:, 0:W_MIX])
    a = _sigmoid(a0_ref[...] + za[:, W_MIX:])
    g_s[...] = _mm(_sigmoid(gd), g2_ref[...])
    e = e_ref[...]
    kk = k * kk_ref[...]
    kkn = kk * lax.rsqrt(_headsum(kk * kk, e) + 1e-12)
    k2 = k * (1.0 + (a - 1.0) * ka_ref[...])
    bon_s[...] = _headsum(r * k2 * rk_ref[...], e) * v
    valid = (ti * tl + lax.broadcasted_iota(jnp.int32, (tl, 1), 0)) < l_real
    r_s[...] = r
    wl_s[...] = jnp.where(valid, wl, 0.0)
    k_s[...] = jnp.where(valid, k2, 0.0)
    v_s[...] = jnp.where(valid, v, 0.0)
    kk_s[...] = jnp.where(valid, kkn, 0.0)
    b_s[...] = jnp.where(valid, kkn * a, 0.0)

    tril = tril_ref[...]

    hd = range(N_HEADS)
    hsl = [slice(h * HEAD_DIM, (h + 1) * HEAD_DIM) for h in hd]

    nc = tl // t
    grp = min(WKV_GROUP, nc)

    def chunk_matrices(cs):
        tops, bots, vs, bks = [], [], [], []
        for c in cs:
            rows = slice(c * t, (c + 1) * t)
            wlc = wl_s[rows, :]
            hi, mid, lo = _split3(wlc)
            cw = (jnp.dot(tril, hi, preferred_element_type=F32) + jnp.dot(tril, mid, preferred_element_type=F32)
                  + jnp.dot(tril, lo, preferred_element_type=F32))
            w_inc = jnp.exp(cw)
            w_inv = jnp.exp(-cw)
            w_prev = jnp.exp(cw - wlc)
            w_t = w_inc[t - 1:t, :]
            top = jnp.concatenate([kk_s[rows, :] * w_prev, r_s[rows, :] * w_inc], axis=0).astype(BF16)
            bot = jnp.concatenate([b_s[rows, :] * w_inv, k_s[rows, :] * w_inv], axis=0)
            bk = (bot * w_t).astype(BF16)
            bot = bot.astype(BF16)
            vc = v_s[rows, :].astype(BF16)
            tops += [top[:, hsl[h]] for h in hd]
            bots += [bot[:, hsl[h]] for h in hd]
            vs += [vc[:, hsl[h]] for h in hd]
            bks += [bk[:, hsl[h]] for h in hd]
            wt_s[c] = jnp.broadcast_to(w_t, (SUBLANES, W_MIX))
        ginv, a_r, av = _wkv_chunk_matrices(tops, bots, vs, t)
        for j, c in enumerate(cs):
            for h in hd:
                k = j * N_HEADS + h
                top_s[c, h] = tops[k]
                bk_s[c, h] = bks[k]
                vh_s[c, h] = vs[k]
                av_s[c, h] = av[k]
                gi_s[c, h] = ginv[k].astype(BF16)
                ar_s[c, h] = a_r[k].astype(BF16)

    def chunk_state(c):
        rows = slice(c * t, (c + 1) * t)
        pp = [_mm_nt(top_s[c, h], s_scr[h]) for h in hd]
        u = [_mm(gi_s[c, h], -(pp[h][:t] + av_s[c, h])) for h in hd]
        uv = [jnp.concatenate([u[h].astype(BF16), vh_s[c, h]], axis=0) for h in hd]
        y = [pp[h][t:] + _mm(ar_s[c, h], uv[h]) for h in hd]
        w_t = wt_s[c]
        for h in hd:
            s_scr[h] = s_scr[h] * w_t[0:1, hsl[h]] + _mm_tn(uv[h], bk_s[c, h])
        y_s[rows, :] = jnp.concatenate(y, axis=1)

    groups = [list(range(g, min(g + grp, nc))) for g in range(0, nc, grp)]
    chunk_matrices(groups[0])
    for gi in range(1, len(groups)):
        for c in groups[gi - 1]:
            chunk_state(c)
        chunk_matrices(groups[gi])
    for c in groups[-1]:
        chunk_state(c)

    y = y_s[...]
    yc = y - _headsum(y, e) * (1.0 / HEAD_DIM)
    var = _headsum(yc * yc, e) * (1.0 / HEAD_DIM)
    yn = yc * lax.rsqrt(var + LN_EPS)
    y_ref[0] = ((yn * lnw_ref[...] + lnb_ref[...] + bon_s[...]) * g_s[...]).astype(BF16)

    last_tile, last_row = (l_real - 1) // tl, (l_real - 1) % tl

    @pl.when(ti == last_tile)
    def _():
        sho_ref[0] = ubuf[SUBLANES + last_row:SUBLANES + last_row + 1, :]

    @pl.when(ti == nt - 1)
    def _():
        so_ref[0] = s_scr[...]


def _rwkv(ua, shift0, wkv0, prm, l_real):
    b, lp, _ = ua.shape
    t = min(WKV_CHUNK, lp)
    tl = _largest_tile(lp, TIME_TILE, t)
    nt = lp // tl
    tril = (jnp.arange(t)[:, None] >= jnp.arange(t)[None, :]).astype(BF16)
    consts = [prm["mu"], prm["w0"], prm["a0"], prm["w2a"], prm["g2"], prm["kk"], prm["ka"], prm["rk"],
              prm["lnw"], prm["lnb"], prm["e"], tril]
    big = lambda: pltpu.VMEM((tl, W_MIX), F32)
    nc = tl // t
    per_head = lambda r, c, dt: pltpu.VMEM((nc, N_HEADS, r, c), dt)
    chunk_scratch = [per_head(2 * t, HEAD_DIM, BF16), per_head(2 * t, HEAD_DIM, BF16),
                     per_head(t, HEAD_DIM, BF16), per_head(t, HEAD_DIM, F32), per_head(t, t, BF16),
                     per_head(t, 2 * t, BF16), pltpu.VMEM((nc, SUBLANES, W_MIX), F32)]
    return pl.pallas_call(
        functools.partial(_rwkv_kernel, tl=tl, t=t, l_real=l_real, nt=nt),
        grid=(b, nt),
        in_specs=[pl.BlockSpec((1, tl, A_COLS), lambda i, j: (i, j, 0)),
                  pl.BlockSpec((1, 1, A_COLS), lambda i, j: (i, 0, 0)),
                  pl.BlockSpec((1, N_HEADS, HEAD_DIM, HEAD_DIM), lambda i, j: (i, 0, 0, 0))]
                 + [_const_spec(c.shape) for c in consts],
        out_specs=[pl.BlockSpec((1, tl, W_MIX), lambda i, j: (i, j, 0)),
                   pl.BlockSpec((1, 1, A_COLS), lambda i, j: (i, 0, 0)),
                   pl.BlockSpec((1, N_HEADS, HEAD_DIM, HEAD_DIM), lambda i, j: (i, 0, 0, 0))],
        out_shape=[jax.ShapeDtypeStruct((b, lp, W_MIX), BF16),
                   jax.ShapeDtypeStruct((b, 1, A_COLS), F32),
                   jax.ShapeDtypeStruct((b, N_HEADS, HEAD_DIM, HEAD_DIM), F32)],
        scratch_shapes=[pltpu.VMEM((tl + SUBLANES, A_COLS), F32), pltpu.VMEM((SUBLANES, A_COLS), F32),
                        pltpu.VMEM((N_HEADS, HEAD_DIM, HEAD_DIM), F32)] + [big() for _ in range(9)]
                       + chunk_scratch,
        compiler_params=_cparams("parallel", "arbitrary"),
        name="rwkv",
    )(ua, shift0, wkv0, *consts)


def _ffn_kernel(*refs, tl, tf, dgrp, l_real, mixer_out):
    if mixer_out:
        x_ref, ya_ref, yb_ref, wo_ref = refs[:4]
        refs = refs[4:]
        y_ab = jnp.concatenate([ya_ref[0], yb_ref[0]], axis=1)
        x = x_ref[0] + jnp.dot(y_ab, wo_ref[...], preferred_element_type=F32)
    else:
        x = refs[0][0]
        refs = refs[1:]
    g_ref, wu_ref, cw_ref, cb_ref, wd_ref, buf_ref, o_ref, st_ref, up_scr, act_scr, carry_scr = refs
    ti = pl.program_id(1)
    h = _rms(x, g_ref[...]).astype(BF16)
    last_tile, last_row = (l_real - 1) // tl, (l_real - 1) % tl

    @pl.when(ti == 0)
    def _():
        carry_scr[SUBLANES - (CONV_W - 1):SUBLANES, :] = buf_ref[0]

    nf = D_FF // tf
    col = lambda part, f: slice(part * D_FF + f * tf, part * D_FF + (f + 1) * tf)

    def up(f):
        for part in range(2):
            cols = col(part, f)
            up_scr[0:SUBLANES, cols] = carry_scr[:, cols]
            up_scr[SUBLANES:SUBLANES + tl, cols] = jnp.dot(h, wu_ref[:, cols], preferred_element_type=F32)
            carry_scr[:, cols] = up_scr[tl:tl + SUBLANES, cols]

    ahead = 2
    for f in range(min(ahead, nf)):
        up(f)
    out = x
    g0 = 0
    for f in range(nf):
        z = [cb_ref[:, col(part, f)]
             + cw_ref[0:1, col(part, f)] * up_scr[SUBLANES - 2:SUBLANES - 2 + tl, col(part, f)]
             + cw_ref[1:2, col(part, f)] * up_scr[SUBLANES - 1:SUBLANES - 1 + tl, col(part, f)]
             + cw_ref[2:3, col(part, f)] * up_scr[SUBLANES:SUBLANES + tl, col(part, f)]
             for part in range(2)]
        val, gate = z
        act_scr[:, f * tf:(f + 1) * tf] = (gate * _sigmoid(gate) * val).astype(BF16)
        if (f + 1) % dgrp == 0 or f + 1 == nf:
            g1 = (f + 1) * tf
            out = out + jnp.dot(act_scr[:, g0:g1], wd_ref[g0:g1, :], preferred_element_type=F32)
            g0 = g1
        if f + ahead < nf:
            up(f + ahead)
    o_ref[0] = out

    @pl.when(ti == last_tile)
    def _():
        st_ref[0] = up_scr[SUBLANES + last_row - 1:SUBLANES + last_row + 1, :]


def _ffn(x, g, wu, cw, cb, wd, buf, l_real, mixer=None):
    b, lp, _ = x.shape
    tl = _largest_tile(lp, FFN_TILE, SUBLANES)
    tf, dgrp = 256, 4
    assert l_real >= CONV_W - 1 and (l_real - 1) % tl >= 1
    row = lambda c: pl.BlockSpec((1, tl, c), lambda i, j: (i, j, 0))
    mix_args, mix_specs = [], []
    if mixer is not None:
        mix_args = list(mixer)
        mix_specs = [row(W_MIX), row(W_MIX), _const_spec(mixer[2].shape)]
    return pl.pallas_call(
        functools.partial(_ffn_kernel, tl=tl, tf=tf, dgrp=dgrp, l_real=l_real, mixer_out=mixer is not None),
        grid=(b, lp // tl),
        in_specs=[row(D_MODEL)] + mix_specs
                 + [_const_spec(g.shape), _const_spec(wu.shape), _const_spec(cw.shape), _const_spec(cb.shape),
                    _const_spec(wd.shape),
                    pl.BlockSpec((1, CONV_W - 1, 2 * D_FF), lambda i, j: (i, 0, 0))],
        out_specs=[pl.BlockSpec((1, tl, D_MODEL), lambda i, j: (i, j, 0)),
                   pl.BlockSpec((1, CONV_W - 1, 2 * D_FF), lambda i, j: (i, 0, 0))],
        out_shape=[jax.ShapeDtypeStruct((b, lp, D_MODEL), F32),
                   jax.ShapeDtypeStruct((b, CONV_W - 1, 2 * D_FF), F32)],
        scratch_shapes=[pltpu.VMEM((tl + SUBLANES, 2 * D_FF), F32), pltpu.VMEM((tl, D_FF), BF16),
                        pltpu.VMEM((SUBLANES, 2 * D_FF), F32)],
        compiler_params=_cparams("parallel", "arbitrary"),
        name="ffn",
    )(x, *mix_args, g, wu, cw, cb, wd, buf)


def _s5_kernel(x_ref, g_ref, wbr_ref, wbi_ref, wcr_ref, wci_ref, are_ref, aim_ref, d_ref, wg_ref,
               perm_ref, permt_ref, s0r_ref, s0i_ref, o_ref, sor_ref, soi_ref, bu_scr, st_scr, *, tt, l_real):
    ti = pl.program_id(1)
    nb = S5_BATCH
    m = nb * tt
    ntile = S5_STATE // LANES

    @pl.when(ti == 0)
    def _():
        for c in range(ntile):
            st_scr[c] = s0r_ref[:, c * LANES:(c + 1) * LANES]
            st_scr[ntile + c] = s0i_ref[:, c * LANES:(c + 1) * LANES]

    x = x_ref[...].reshape(m, D_MODEL)
    perm = perm_ref[...]
    u_hi, u_lo = _split2(_rms(x, g_ref[...]))
    ub_f = jnp.dot(perm, u_hi, preferred_element_type=F32)
    ub = ub_f.astype(BF16)
    u = ub_f + jnp.dot(perm, u_lo, preferred_element_type=F32)
    tpq = LANES // GC * P_C // LANES
    for q in range(D_MODEL // LANES):
        uq = ub[:, q * LANES:(q + 1) * LANES]
        br = jnp.dot(uq, wbr_ref[q], preferred_element_type=F32)
        bi = jnp.dot(uq, wbi_ref[q], preferred_element_type=F32)
        for c in range(tpq):
            bu_scr[q * tpq + c] = br[:, c * LANES:(c + 1) * LANES]
            bu_scr[ntile + q * tpq + c] = bi[:, c * LANES:(c + 1) * LANES]

    grp = 4
    for cb in range(ntile // grp):
        tiles = list(range(cb * grp, (cb + 1) * grp))
        a_re = [jnp.broadcast_to(are_ref[:, c * LANES:(c + 1) * LANES], (nb, LANES)) for c in tiles]
        a_im = [jnp.broadcast_to(aim_ref[:, c * LANES:(c + 1) * LANES], (nb, LANES)) for c in tiles]
        s_re = [st_scr[c] for c in tiles]
        s_im = [st_scr[ntile + c] for c in tiles]
        for t in range(tt):
            rows = slice(t * nb, (t + 1) * nb)
            for i, c in enumerate(tiles):
                n_re = a_re[i] * s_re[i] - a_im[i] * s_im[i] + bu_scr[c, rows, :]
                n_im = a_re[i] * s_im[i] + a_im[i] * s_re[i] + bu_scr[ntile + c, rows, :]
                bu_scr[c, rows, :] = n_re
                bu_scr[ntile + c, rows, :] = n_im
                s_re[i], s_im[i] = n_re, n_im
        for i, c in enumerate(tiles):
            st_scr[c] = s_re[i]
            st_scr[ntile + c] = s_im[i]

    ys = []
    for q in range(D_MODEL // LANES):
        sr = jnp.concatenate([bu_scr[q * tpq + c] for c in range(tpq)], axis=1).astype(BF16)
        si = jnp.concatenate([bu_scr[ntile + q * tpq + c] for c in range(tpq)], axis=1).astype(BF16)
        ys.append(jnp.dot(sr, wcr_ref[q], preferred_element_type=F32)
                  + jnp.dot(si, wci_ref[q], preferred_element_type=F32))
    yc = jnp.concatenate(ys, axis=1) + d_ref[...] * u
    z = jnp.dot(_gelu(yc).astype(BF16), wg_ref[...], preferred_element_type=F32)
    mix = z[:, 0:D_MODEL] * _sigmoid(z[:, D_MODEL:])
    o_ref[...] = (x + _mm_exact_lhs(permt_ref[...], mix)).reshape(nb, tt, D_MODEL)
    last_tile, last_row = (l_real - 1) // tt, (l_real - 1) % tt

    @pl.when(ti == last_tile)
    def _():
        rows = slice(last_row * nb, (last_row + 1) * nb)
        for c in range(ntile):
            sor_ref[:, c * LANES:(c + 1) * LANES] = bu_scr[c, rows, :]
            soi_ref[:, c * LANES:(c + 1) * LANES] = bu_scr[ntile + c, rows, :]


def _s5(x, g, sp, s0r, s0i, l_real):
    b, lp, _ = x.shape
    assert b % S5_BATCH == 0
    tt = _largest_tile(lp, 32, SUBLANES)
    m = S5_BATCH * tt
    dst = jnp.arange(m)
    src = (dst % S5_BATCH) * tt + dst // S5_BATCH
    perm = (src[:, None] == jnp.arange(m)[None, :]).astype(BF16)
    consts = [g, sp["wbr"], sp["wbi"], sp["wcr"], sp["wci"], sp["are"], sp["aim"], sp["d"], sp["wglu"],
              perm, perm.T]
    st = pl.BlockSpec((S5_BATCH, S5_STATE), lambda i, j: (i, 0))
    return pl.pallas_call(
        functools.partial(_s5_kernel, tt=tt, l_real=l_real),
        grid=(b // S5_BATCH, lp // tt),
        in_specs=[pl.BlockSpec((S5_BATCH, tt, D_MODEL), lambda i, j: (i, j, 0))]
                 + [_const_spec(c.shape) for c in consts] + [st, st],
        out_specs=[pl.BlockSpec((S5_BATCH, tt, D_MODEL), lambda i, j: (i, j, 0)), st, st],
        out_shape=[jax.ShapeDtypeStruct((b, lp, D_MODEL), F32),
                   jax.ShapeDtypeStruct((b, S5_STATE), F32), jax.ShapeDtypeStruct((b, S5_STATE), F32)],
        scratch_shapes=[pltpu.VMEM((2 * S5_STATE // LANES, m, LANES), F32),
                        pltpu.VMEM((2 * S5_STATE // LANES, S5_BATCH, LANES), F32)],
        compiler_params=_cparams("parallel", "arbitrary"),
        name="s5",
    )(x, *consts, s0r, s0i)


def _prep_params(p):
    j = 0
    head_of = jnp.arange(W_MIX) // HEAD_DIM
    e = (head_of[:, None] == head_of[None, :]).astype(BF16)
    w_in = p["w_in_e"][j]
    wf = jnp.zeros((D_MODEL, LANES), F32).at[:, :N_HEADS].set(w_in[:, A_COLS + 3 * W_MIX:])
    wb = jnp.concatenate([w_in[:, A_COLS:A_COLS + 3 * W_MIX], wf], axis=1).astype(BF16)
    fb = jnp.zeros((1, LANES), F32).at[0, :N_HEADS].set(p["b_fbias"][j])
    w2a = jnp.zeros((R_W + R_A, 2 * W_MIX), F32)
    w2a = w2a.at[:R_W, :W_MIX].set(p["a_w2"][j]).at[R_W:, W_MIX:].set(p["a_a2"][j])
    row = lambda a: a.reshape(1, -1).astype(F32)
    rw = dict(mu=row(p["a_mu"][j]), w0=row(p["a_w0"][j]), a0=row(p["a_a0"][j]), w2a=w2a.astype(BF16),
              g2=p["a_g2"][j].astype(BF16), kk=row(p["a_kk"][j]), ka=row(p["a_ka"][j]),
              rk=row(p["a_rk"][j]), lnw=row(p["a_ln_w"][j]), lnb=row(p["a_ln_b"][j]), e=e)
    hm = (jnp.arange(N_HEADS)[:, None] == head_of[None, :]).astype(BF16)
    l0 = dict(wa=w_in[:, :A_COLS].astype(BF16), wb=wb, fb=fb, e=e, hm=hm,
              qg=row(jnp.tile(p["b_qnorm"][j], N_HEADS)), kg=row(jnp.tile(p["b_knorm"][j], N_HEADS)),
              w_out=p["w_out_e"][j].astype(BF16))
    lam_re, lam_im = p["c_lam_re"][j].astype(F32), p["c_lam_im"][j].astype(F32)
    dt = jnp.exp(p["c_log_dt"][j].astype(F32))[:, None]
    mag = jnp.exp(lam_re * dt)
    ab_re, ab_im = mag * jnp.cos(lam_im * dt), mag * jnp.sin(lam_im * dt)
    den = lam_re * lam_re + lam_im * lam_im
    f_re = ((ab_re - 1.0) * lam_re + ab_im * lam_im) / den
    f_im = (ab_im * lam_re - (ab_re - 1.0) * lam_im) / den
    b_re, b_im = p["c_b_re"][j], p["c_b_im"][j]
    bb_re = f_re[..., None] * b_re - f_im[..., None] * b_im
    bb_im = f_re[..., None] * b_im + f_im[..., None] * b_re
    gpt = LANES // GC
    eye = jnp.eye(gpt, dtype=F32)

    def in_layout(bb):
        t = bb.reshape(G_C // gpt, gpt, P_C, GC).transpose(0, 1, 3, 2)
        return jnp.einsum("qgcp,gh->qgchp", t, eye).reshape(G_C // gpt, gpt * GC, gpt * P_C).astype(BF16)

    def out_layout(cc):
        t = cc.reshape(G_C // gpt, gpt, GC, P_C).transpose(0, 1, 3, 2)
        return jnp.einsum("qgpc,gh->qgphc", t, eye).reshape(G_C // gpt, gpt * P_C, gpt * GC).astype(BF16)

    s5 = dict(wbr=in_layout(bb_re), wbi=in_layout(bb_im), wcr=out_layout(p["c_c_re"][j]),
              wci=out_layout(-p["c_c_im"][j]), are=ab_re.reshape(1, S5_STATE), aim=ab_im.reshape(1, S5_STATE),
              d=row(p["c_d"][j]), wglu=p["w_glu"][j].astype(BF16))
    ffn = [dict(g=row(p["norm_ffn"][i]), wu=p["w_up"][i].astype(BF16), cw=p["conv_w"][i].astype(F32),
                cb=row(p["conv_b"][i]), wd=p["w_down"][i].astype(BF16)) for i in range(2)]
    return dict(l0=l0, rw=rw, s5=s5, ffn=ffn, g_mix=[row(p["norm_mix"][i]) for i in range(2)])


def _trunk(x, l_real, init, pp, past):
    b, lp, _ = x.shape
    l0 = pp["l0"]
    ua, q, kf, kb, vf, vb, lf = _inproj(x, l_real, pp["g_mix"][0], l0["wa"], l0["wb"], l0["e"], l0["qg"],
                                        l0["kg"], l0["fb"])
    y_a, a_shift, a_wkv = _rwkv(ua, init["a_shift"].reshape(b, 1, A_COLS), init["a_wkv"], pp["rw"], l_real)
    lf_row = jnp.transpose(lf[:, :, :N_HEADS], (0, 2, 1)).reshape(b * N_HEADS, lp)
    if past is None:
        lpad = -lp % LANES
        c_row = _cumsum_lanes(jnp.pad(lf_row, ((0, 0), (0, lpad))))[:, :lp].reshape(b, N_HEADS, lp)
        cq = jnp.pad(jnp.transpose(c_row, (0, 2, 1)), ((0, 0), (0, 0), (0, LANES - N_HEADS)))
        y_b = _attn_prompt(q, kb, vb, cq, c_row, min(ATTN_TILE, lp))
    else:
        pk, pv, plf = past
        plen = pk.shape[1]
        plf_row = jnp.transpose(plf, (0, 2, 1)).reshape(b * N_HEADS, plen)
        c_all = _cumsum_lanes(jnp.concatenate([plf_row, jnp.pad(lf_row, ((0, 0), (0, -lp % LANES)))], axis=1))
        c_past = c_all[:, :plen].reshape(b, N_HEADS, plen)
        c_new = c_all[:, plen:plen + lp].reshape(b, N_HEADS, lp)
        cq_stack = jnp.broadcast_to(c_new.reshape(b, N_HEADS * lp, 1), (b, N_HEADS * lp, LANES))
        y_b = _attn_sample(q, cq_stack, pk.reshape(b, plen, W_MIX), pv.reshape(b, plen, W_MIX), c_past,
                           kb, vb, c_new, l0["hm"])
    f0 = pp["ffn"][0]
    x2, conv0 = _ffn(x, f0["g"], f0["wu"], f0["cw"], f0["cb"], f0["wd"], init["ffn_conv"][0], l_real,
                     mixer=(y_a, y_b, l0["w_out"]))
    x3, c_re, c_im = _s5(x2, pp["g_mix"][1], pp["s5"], init["c_re"].reshape(b, S5_STATE),
                         init["c_im"].reshape(b, S5_STATE), l_real)
    f1 = pp["ffn"][1]
    x4, conv1 = _ffn(x3, f1["g"], f1["wu"], f1["cw"], f1["cb"], f1["wd"], init["ffn_conv"][1], l_real)
    states = (a_shift.reshape(1, b, A_COLS), a_wkv[None], kf[None], vf[None],
              lf[None, :, :l_real, :N_HEADS], c_re.reshape(1, b, G_C, P_C), c_im.reshape(1, b, G_C, P_C),
              jnp.stack([conv0, conv1], axis=0))
    return x4, states


def kernel(x_prompt, x_sample, state_a_shift, state_a_wkv, cache_b_k, cache_b_v, cache_b_logf, state_c_re, state_c_im, state_ffn_conv, meta, norm_mix, norm_ffn, w_in_e, a_mu, a_w0, a_w2, a_a0, a_a2, a_g2, a_kk, a_ka, a_rk, a_ln_w, a_ln_b, b_fbias, b_qnorm, b_knorm, w_out_e, c_lam_re, c_lam_im, c_log_dt, c_b_re, c_b_im, c_c_re, c_c_im, c_d, w_glu, w_up, conv_w, conv_b, w_down):
    pp = _prep_params(dict(
        norm_mix=norm_mix, norm_ffn=norm_ffn, w_in_e=w_in_e, a_mu=a_mu, a_w0=a_w0, a_w2=a_w2, a_a0=a_a0,
        a_a2=a_a2, a_g2=a_g2, a_kk=a_kk, a_ka=a_ka, a_rk=a_rk, a_ln_w=a_ln_w, a_ln_b=a_ln_b,
        b_fbias=b_fbias, b_qnorm=b_qnorm, b_knorm=b_knorm, w_out_e=w_out_e, c_lam_re=c_lam_re,
        c_lam_im=c_lam_im, c_log_dt=c_log_dt, c_b_re=c_b_re, c_b_im=c_b_im, c_c_re=c_c_re, c_c_im=c_c_im,
        c_d=c_d, w_glu=w_glu, w_up=w_up, conv_w=conv_w, conv_b=conv_b, w_down=w_down))
    b, seq, _ = x_prompt.shape
    n_meta = meta.shape[0]
    lr = n_meta + seq
    lp = -(-lr // WKV_CHUNK) * WKV_CHUNK
    xp = jnp.concatenate([jnp.broadcast_to(meta.astype(F32)[None], (b, n_meta, D_MODEL)), x_prompt,
                          jnp.zeros((b, lp - lr, D_MODEL), F32)], axis=1)
    init_p = dict(a_shift=jnp.zeros((b, A_COLS), F32),
                  a_wkv=jnp.zeros((b, N_HEADS, HEAD_DIM, HEAD_DIM), F32),
                  c_re=jnp.zeros((b, S5_STATE), F32), c_im=jnp.zeros((b, S5_STATE), F32),
                  ffn_conv=jnp.zeros((2, b, CONV_W - 1, 2 * D_FF), F32))
    yp, st_p = _trunk(xp, lr, init_p, pp, None)
    bs, ls, _ = x_sample.shape
    init_s = dict(a_shift=state_a_shift[0], a_wkv=state_a_wkv[0], c_re=state_c_re[0], c_im=state_c_im[0],
                  ffn_conv=state_ffn_conv)
    past = (cache_b_k[0], cache_b_v[0], cache_b_logf[0])
    ys, st_s = _trunk(x_sample, ls, init_s, pp, past)
    return (yp[:, n_meta:lr], ys, *st_p, *st_s)
```

```python
import functools
import math

import jax
import jax.numpy as jnp
from jax import lax
from jax.experimental import pallas as pl
from jax.experimental.pallas import tpu as pltpu

F32, BF16 = jnp.float32, jnp.bfloat16

D_MODEL = 1024
N_HEADS = 8
HEAD_DIM = 64
W_MIX = N_HEADS * HEAD_DIM
R_W, R_A, R_G = 64, 64, 128
A_COLS = 3 * W_MIX + R_W + R_A + R_G
B_PAD_COLS = 3 * W_MIX + 128
D_FF = 2816
CONV_W = 3
G_C, GC, P_C = 64, 16, 64
S5_STATE = G_C * P_C
EPS = 1e-6
LN_EPS = 64e-5
NEG = -1e30
LOG2E = math.log2(math.e)

LANES = 128
SUBLANES = 8
VMEM_LIMIT = 56 * 1024 * 1024
WKV_CHUNK = 64
WKV_SUB = 16
WKV_GROUP = 4
S5_BATCH = 8
TIME_TILE = 704
FFN_TILE = 352
INPROJ_TILE = 352
ATTN_TILE = 768


def _cparams(*sem):
    return pltpu.CompilerParams(dimension_semantics=sem, vmem_limit_bytes=VMEM_LIMIT)


def _const_spec(shape):
    nd = len(shape)
    return pl.BlockSpec(shape, lambda *_: (0,) * nd, pipeline_mode=pl.Buffered(1))


def _largest_tile(n, cap, mult):
    best = None
    for t in range(mult, min(n, cap) + 1, mult):
        if n % t == 0:
            best = t
    assert best is not None, (n, cap, mult)
    return best


def _mm(a, b):
    return jnp.dot(a.astype(BF16), b.astype(BF16), preferred_element_type=F32)


def _mm_nt(a, b):
    return lax.dot_general(a.astype(BF16), b.astype(BF16), (((1,), (1,)), ((), ())),
                           preferred_element_type=F32)


def _mm_tn(a, b):
    return lax.dot_general(a.astype(BF16), b.astype(BF16), (((0,), (0,)), ((), ())),
                           preferred_element_type=F32)


def _split3(x):
    hi = x.astype(BF16)
    r = x - hi.astype(F32)
    mid = r.astype(BF16)
    lo = (r - mid.astype(F32)).astype(BF16)
    return hi, mid, lo


def _mm_exact_rhs(x, m):
    hi, mid, lo = _split3(x)
    return (jnp.dot(hi, m, preferred_element_type=F32) + jnp.dot(mid, m, preferred_element_type=F32)
            + jnp.dot(lo, m, preferred_element_type=F32))


def _headsum(x, e):
    return jnp.dot(x.astype(BF16), e, preferred_element_type=F32)


def _rms(x, g):
    return x * lax.rsqrt(jnp.mean(x * x, axis=-1, keepdims=True) + EPS) * g


def _sigmoid(x):
    return 1.0 / (1.0 + jnp.exp(-x))


def _log_sigmoid(x):
    return jnp.minimum(x, 0.0) - jnp.log(1.0 + jnp.exp(-jnp.abs(x)))


def _gelu(x):
    return 0.5 * x * (1.0 + jnp.tanh(math.sqrt(2.0 / math.pi) * (x + 0.044715 * (x * x * x))))


def _inproj_kernel(x_ref, g_ref, wa_ref, wb_ref, e_ref, qg_ref, kg_ref, fb_ref,
                   ua_ref, q_ref, kf_ref, kb_ref, vf_ref, vb_ref, lf_ref):
    h = _rms(x_ref[0], g_ref[...]).astype(BF16)
    ua_ref[0] = jnp.dot(h, wa_ref[...], preferred_element_type=F32)
    ub = jnp.dot(h, wb_ref[...], preferred_element_type=F32)
    q = ub[:, 0:W_MIX]
    k = ub[:, W_MIX:2 * W_MIX]
    v = ub[:, 2 * W_MIX:3 * W_MIX]
    f = ub[:, 3 * W_MIX:]
    e = e_ref[...]
    qn = q * lax.rsqrt(_headsum(q * q, e) * (1.0 / HEAD_DIM) + EPS) * qg_ref[...]
    kn = k * lax.rsqrt(_headsum(k * k, e) * (1.0 / HEAD_DIM) + EPS) * kg_ref[...]
    tm = q.shape[0]
    q_ref[0] = (qn * (HEAD_DIM ** -0.5 * LOG2E)).astype(BF16)
    kf_ref[0] = kn.reshape(tm, N_HEADS, HEAD_DIM)
    kb_ref[0] = kn.astype(BF16)
    vf_ref[0] = v.reshape(tm, N_HEADS, HEAD_DIM)
    vb_ref[0] = v.astype(BF16)
    lf = _log_sigmoid(f + fb_ref[...])
    lane = lax.broadcasted_iota(jnp.int32, lf.shape, 1)
    lf_ref[0] = jnp.where(lane < N_HEADS, lf, 0.0)


def _inproj(x, l_real, g, wa, wb, e, qg, kg, fb):
    b, lp, _ = x.shape
    tm = _largest_tile(lp, INPROJ_TILE, SUBLANES)
    row = lambda c: pl.BlockSpec((1, tm, c), lambda i, j: (i, j, 0))
    cache = pl.BlockSpec((1, tm, N_HEADS, HEAD_DIM), lambda i, j: (i, j, 0, 0))
    rowshape = lambda c, dt: jax.ShapeDtypeStruct((b, lp, c), dt)
    cacheshape = jax.ShapeDtypeStruct((b, l_real, N_HEADS, HEAD_DIM), F32)
    return pl.pallas_call(
        _inproj_kernel,
        grid=(b, lp // tm),
        in_specs=[row(D_MODEL), _const_spec(g.shape), _const_spec(wa.shape), _const_spec(wb.shape),
                  _const_spec(e.shape), _const_spec(qg.shape), _const_spec(kg.shape), _const_spec(fb.shape)],
        out_specs=[row(A_COLS), row(W_MIX), cache, row(W_MIX), cache, row(W_MIX), row(LANES)],
        out_shape=[rowshape(A_COLS, F32), rowshape(W_MIX, BF16), cacheshape, rowshape(W_MIX, BF16),
                   cacheshape, rowshape(W_MIX, BF16), rowshape(LANES, F32)],
        compiler_params=_cparams("parallel", "parallel"),
        name="inproj",
    )(x, g, wa, wb, e, qg, kg, fb)


def _cumsum_kernel(x_ref, tri_ref, o_ref, *, nblk):
    tri = tri_ref[...]
    carry = jnp.zeros((x_ref.shape[0], 1), F32)
    for j in range(nblk):
        cs = _mm_exact_rhs(x_ref[:, j * LANES:(j + 1) * LANES], tri) + carry
        o_ref[:, j * LANES:(j + 1) * LANES] = cs * LOG2E
        carry = cs[:, LANES - 1:LANES]


def _cumsum_lanes(x):
    rows, n = x.shape
    assert n % LANES == 0
    tri = (jnp.arange(LANES)[:, None] <= jnp.arange(LANES)[None, :]).astype(BF16)
    return pl.pallas_call(
        functools.partial(_cumsum_kernel, nblk=n // LANES),
        grid=(1,),
        in_specs=[_const_spec(x.shape), _const_spec(tri.shape)],
        out_specs=_const_spec(x.shape),
        out_shape=jax.ShapeDtypeStruct(x.shape, F32),
        compiler_params=_cparams("arbitrary"),
        name="cumsum",
    )(x, tri)


def _attn_prompt_kernel(q_ref, k_ref, v_ref, cq_ref, cr_ref, o_ref, *, bounds):
    p = pl.program_id(1)
    lane = lax.broadcasted_iota(jnp.int32, (1, LANES), 1)
    head_lanes = (lane < HEAD_DIM, lane >= HEAD_DIM)
    for qi in range(len(bounds) - 1):
        r0, r1 = bounds[qi], bounds[qi + 1]
        q2 = q_ref[0, r0:r1, :]
        cqt = cq_ref[0, r0:r1, :]
        lane_q = lax.broadcasted_iota(jnp.int32, cqt.shape, 1)
        res = []
        for hh in range(2):
            h = 2 * p + hh
            qm = jnp.where(head_lanes[hh], q2, jnp.zeros_like(q2))
            cq = jnp.sum(jnp.where(lane_q == h, cqt, 0.0), axis=-1, keepdims=True)
            ck_all = cr_ref[0, pl.ds(h, 1), :]
            m = jnp.full((r1 - r0, 1), NEG, F32)
            l = jnp.zeros((r1 - r0, 1), F32)
            acc = jnp.zeros((r1 - r0, LANES), F32)

            def block(a0, a1, c0, c1, m, l, acc, diag):
                t = lax.dot_general(qm[a0:a1], k_ref[0, c0:c1, :], (((1,), (1,)), ((), ())),
                                    preferred_element_type=F32) - ck_all[:, c0:c1]
                if diag:
                    ri = lax.broadcasted_iota(jnp.int32, t.shape, 0) + (r0 + a0)
                    ci = lax.broadcasted_iota(jnp.int32, t.shape, 1) + c0
                    t = jnp.where(ri >= ci, t, NEG)
                m_new = jnp.maximum(m, cq[a0:a1] + jnp.max(t, axis=-1, keepdims=True))
                alpha = jnp.exp2(m - m_new)
                pe = jnp.exp2(t + (cq[a0:a1] - m_new))
                l = alpha * l + jnp.sum(pe, axis=-1, keepdims=True)
                acc = alpha * acc + jnp.dot(pe.astype(BF16), v_ref[0, c0:c1, :], preferred_element_type=F32)
                return m_new, l, acc

            for ki in range(qi):
                m, l, acc = block(0, r1 - r0, bounds[ki], bounds[ki + 1], m, l, acc, False)
            half = ((r1 - r0) // 2) // LANES * LANES
            if half == 0:
                m, l, acc = block(0, r1 - r0, r0, r1, m, l, acc, True)
            else:
                m, l, acc = block(0, r1 - r0, r0, r0 + half, m, l, acc, True)
                mb, lb, ab = block(half, r1 - r0, r0 + half, r1, m[half:], l[half:], acc[half:], True)
                m = jnp.concatenate([m[:half], mb], axis=0)
                l = jnp.concatenate([l[:half], lb], axis=0)
                acc = jnp.concatenate([acc[:half], ab], axis=0)
            res.append(acc / l)
        o_ref[0, r0:r1, :] = jnp.where(head_lanes[0], res[0], res[1]).astype(BF16)


def _attn_prompt(q, k, v, cq, cr, tile):
    b, lp, _ = q.shape
    bounds = tuple(range(0, lp, tile)) + (lp,)
    col = pl.BlockSpec((1, lp, LANES), lambda i, p: (i, 0, p))
    return pl.pallas_call(
        functools.partial(_attn_prompt_kernel, bounds=bounds),
        grid=(b, N_HEADS // 2),
        in_specs=[col, col, col,
                  pl.BlockSpec((1, lp, LANES), lambda i, p: (i, 0, 0)),
                  pl.BlockSpec((1, N_HEADS, lp), lambda i, p: (i, 0, 0))],
        out_specs=col,
        out_shape=jax.ShapeDtypeStruct((b, lp, W_MIX), BF16),
        compiler_params=_cparams("parallel", "arbitrary"),
        name="attn_prompt",
    )(q, k, v, cq, cr)


def _attn_sample_kernel(q_ref, cq_ref, kp_ref, vp_ref, cp_ref, kn_ref, vn_ref, cn_ref, hm_ref, o_ref,
                        qx_scr, m_scr, l_scr, acc_scr, *, nkp, ls):
    j = pl.program_id(1)
    rows = N_HEADS * ls

    @pl.when(j == 0)
    def _():
        q = q_ref[0]
        qx = jnp.broadcast_to(q[None], (N_HEADS, ls, W_MIX)) * hm_ref[...][:, None, :]
        qx_scr[...] = qx.reshape(rows, W_MIX)
        m_scr[...] = jnp.full(m_scr.shape, NEG, F32)
        l_scr[...] = jnp.zeros(l_scr.shape, F32)
        acc_scr[...] = jnp.zeros(acc_scr.shape, F32)

    def update(k, v, ck, causal):
        tk = k.shape[0]
        s = lax.dot_general(qx_scr[...], k, (((1,), (1,)), ((), ())), preferred_element_type=F32)
        ckx = jnp.broadcast_to(ck[:, None, :], (N_HEADS, ls, tk)).reshape(rows, tk)
        s = s + cq_ref[0][:, 0:1] - ckx
        if causal:
            ri = lax.broadcasted_iota(jnp.int32, (N_HEADS, ls, tk), 1).reshape(rows, tk)
            ci = lax.broadcasted_iota(jnp.int32, (rows, tk), 1)
            s = jnp.where(ri >= ci, s, NEG)
        m = m_scr[...]
        m_new = jnp.maximum(m, jnp.max(s, axis=-1, keepdims=True))
        alpha = jnp.exp2(m - m_new)
        pe = jnp.exp2(s - m_new)
        l_scr[...] = alpha * l_scr[...] + jnp.sum(pe, axis=-1, keepdims=True)
        acc_scr[...] = alpha * acc_scr[...] + jnp.dot(pe.astype(BF16), v, preferred_element_type=F32)
        m_scr[...] = m_new

    @pl.when(j < nkp)
    def _():
        update(kp_ref[0].astype(BF16), vp_ref[0].astype(BF16), cp_ref[0], False)

    @pl.when(j == nkp)
    def _():
        update(kn_ref[0], vn_ref[0], cn_ref[0], True)
        o = (acc_scr[...] / l_scr[...]).reshape(N_HEADS, ls, W_MIX) * hm_ref[...].astype(F32)[:, None, :]
        o_ref[0] = jnp.sum(o, axis=0).astype(BF16)


def _attn_sample(q, cq_stack, kp, vp, cp, kn, vn, cn, hm):
    b, ls, _ = q.shape
    past = kp.shape[1]
    tkp = _largest_tile(past, 1024, LANES)
    nkp = past // tkp
    rows = N_HEADS * ls
    pidx = lambda i, j: (i, jnp.minimum(j, nkp - 1), 0)
    return pl.pallas_call(
        functools.partial(_attn_sample_kernel, nkp=nkp, ls=ls),
        grid=(b, nkp + 1),
        in_specs=[pl.BlockSpec((1, ls, W_MIX), lambda i, j: (i, 0, 0)),
                  pl.BlockSpec((1, rows, LANES), lambda i, j: (i, 0, 0)),
                  pl.BlockSpec((1, tkp, W_MIX), pidx),
                  pl.BlockSpec((1, tkp, W_MIX), pidx),
                  pl.BlockSpec((1, N_HEADS, tkp), lambda i, j: (i, 0, jnp.minimum(j, nkp - 1))),
                  pl.BlockSpec((1, ls, W_MIX), lambda i, j: (i, 0, 0)),
                  pl.BlockSpec((1, ls, W_MIX), lambda i, j: (i, 0, 0)),
                  pl.BlockSpec((1, N_HEADS, ls), lambda i, j: (i, 0, 0)),
                  _const_spec(hm.shape)],
        out_specs=pl.BlockSpec((1, ls, W_MIX), lambda i, j: (i, 0, 0)),
        out_shape=jax.ShapeDtypeStruct((b, ls, W_MIX), BF16),
        scratch_shapes=[pltpu.VMEM((rows, W_MIX), BF16), pltpu.VMEM((rows, 1), F32),
                        pltpu.VMEM((rows, 1), F32), pltpu.VMEM((rows, W_MIX), F32)],
        compiler_params=_cparams("parallel", "arbitrary"),
        name="attn_sample",
    )(q, cq_stack, kp, vp, cp, kn, vn, cn, hm)


def _wkv_chunk_matrices(tops, bots, vs, t):
    hd = range(len(tops))
    ri = lax.broadcasted_iota(jnp.int32, (t, t), 0)
    ci = lax.broadcasted_iota(jnp.int32, (t, t), 1)
    eye = (ri == ci).astype(F32)
    r2 = lax.broadcasted_iota(jnp.int32, (t, 2 * t), 0)
    c2 = lax.broadcasted_iota(jnp.int32, (t, 2 * t), 1)
    c2 = jnp.where(c2 >= t, c2 - t, c2)
    same = (ri // WKV_SUB) == (ci // WKV_SUB)
    aa = [_mm_nt(tops[h], bots[h]) for h in hd]
    n = [jnp.where(ri > ci, aa[h][:t, :t], 0.0) for h in hd]
    a_ak = [jnp.where(ri > ci, aa[h][:t, t:], 0.0) for h in hd]
    a_r = [jnp.where(r2 >= c2, aa[h][t:, :], 0.0) for h in hd]
    av = [_mm(a_ak[h], vs[h]) for h in hd]
    d = [jnp.where(same, n[h], 0.0) for h in hd]
    lo = [n[h] - d[h] for h in hd]
    d2 = [_mm(d[h], d[h]) for h in hd]
    x = [_mm(eye - d[h], eye + d2[h]) for h in hd]
    d4 = [_mm(d2[h], d2[h]) for h in hd]
    x = [_mm(x[h], eye + d4[h]) for h in hd]
    d8 = [_mm(d4[h], d4[h]) for h in hd]
    x = [_mm(x[h], eye + d8[h]) for h in hd]
    mb = [_mm(x[h], lo[h]) for h in hd]
    mb2 = [_mm(mb[h], mb[h]) for h in hd]
    xx = [_mm(eye + mb2[h], x[h]) for h in hd]
    ginv = [_mm(eye - mb[h], xx[h]) for h in hd]
    return ginv, a_r, av


def _rwkv_kernel(u_ref, sh_ref, s0_ref, mu_ref, w0_ref, a0_ref, w2a_ref, g2_ref, kk_ref, ka_ref,
                 rk_ref, lnw_ref, lnb_ref, e_ref, tril_ref,
                 y_ref, sho_ref, so_ref,
                 ubuf, carry, s_scr, r_s, wl_s, k_s, v_s, kk_s, b_s, y_s, bon_s, g_s,
                 top_s, bk_s, vh_s, av_s, gi_s, ar_s, wt_s, *, tl, t, l_real, nt):
    ti = pl.program_id(1)

    @pl.when(ti == 0)
    def _():
        carry[...] = jnp.broadcast_to(sh_ref[0], carry.shape)
        s_scr[...] = s0_ref[0]

    u = u_ref[0]
    ubuf[0:SUBLANES, :] = carry[...]
    ubuf[SUBLANES:SUBLANES + tl, :] = u
    u_prev = ubuf[SUBLANES - 1:SUBLANES - 1 + tl, :]
    carry[...] = u[tl - SUBLANES:tl, :]
    um = u + (u_prev - u) * mu_ref[...]
    r = um[:, 0:W_MIX]
    k = um[:, W_MIX:2 * W_MIX]
    v = um[:, 2 * W_MIX:3 * W_MIX]
    wa = um[:, 3 * W_MIX:3 * W_MIX + R_W + R_A]
    gd = um[:, 3 * W_MIX + R_W + R_A:]
    lane = lax.broadcasted_iota(jnp.int32, wa.shape, 1)
    za = _mm(jnp.where(lane < R_W, jnp.tanh(wa), wa), w2a_ref[...])
    wl = (-math.exp(-0.5)) * _sigmoid(w0_ref[...] + za[:, 0:W_MIX])
    a = _sigmoid(a0_ref[...] + za[:, W_MIX:])
    g_s[...] = _mm(_sigmoid(gd), g2_ref[...])
    e = e_ref[...]
    kk = k * kk_ref[...]
    kkn = kk * lax.rsqrt(_headsum(kk * kk, e) + 1e-12)
    k2 = k * (1.0 + (a - 1.0) * ka_ref[...])
    bon_s[...] = _headsum(r * k2 * rk_ref[...], e) * v
    valid = (ti * tl + lax.broadcasted_iota(jnp.int32, (tl, 1), 0)) < l_real
    r_s[...] = r
    wl_s[...] = jnp.where(valid, wl, 0.0)
    k_s[...] = jnp.where(valid, k2, 0.0)
    v_s[...] = jnp.where(valid, v, 0.0)
    kk_s[...] = jnp.where(valid, kkn, 0.0)
    b_s[...] = jnp.where(valid, kkn * a, 0.0)

    tril = tril_ref[...]

    hd = range(N_HEADS)
    hsl = [slice(h * HEAD_DIM, (h + 1) * HEAD_DIM) for h in hd]

    nc = tl // t
    grp = min(WKV_GROUP, nc)

    def chunk_matrices(cs):
        tops, bots, vs, bks = [], [], [], []
        for c in cs:
            rows = slice(c * t, (c + 1) * t)
            wlc = wl_s[rows, :]
            hi, mid, lo = _split3(wlc)
            cw = (jnp.dot(tril, hi, preferred_element_type=F32) + jnp.dot(tril, mid, preferred_element_type=F32)
                  + jnp.dot(tril, lo, preferred_element_type=F32))
            w_inc = jnp.exp(cw)
            w_inv = jnp.exp(-cw)
            w_prev = jnp.exp(cw - wlc)
            w_t = w_inc[t - 1:t, :]
            top = jnp.concatenate([kk_s[rows, :] * w_prev, r_s[rows, :] * w_inc], axis=0).astype(BF16)
            bot = jnp.concatenate([b_s[rows, :] * w_inv, k_s[rows, :] * w_inv], axis=0)
            bk = (bot * w_t).astype(BF16)
            bot = bot.astype(BF16)
            vc = v_s[rows, :].astype(BF16)
            tops += [top[:, hsl[h]] for h in hd]
            bots += [bot[:, hsl[h]] for h in hd]
            vs += [vc[:, hsl[h]] for h in hd]
            bks += [bk[:, hsl[h]] for h in hd]
            wt_s[c] = jnp.broadcast_to(w_t, (SUBLANES, W_MIX))
        ginv, a_r, av = _wkv_chunk_matrices(tops, bots, vs, t)
        for j, c in enumerate(cs):
            for h in hd:
                k = j * N_HEADS + h
                top_s[c, h] = tops[k]
                bk_s[c, h] = bks[k]
                vh_s[c, h] = vs[k]
                av_s[c, h] = av[k]
                gi_s[c, h] = ginv[k].astype(BF16)
                ar_s[c, h] = a_r[k].astype(BF16)

    def chunk_state(c):
        rows = slice(c * t, (c + 1) * t)
        pp = [_mm_nt(top_s[c, h], s_scr[h]) for h in hd]
        u = [_mm(gi_s[c, h], -(pp[h][:t] + av_s[c, h])) for h in hd]
        uv = [jnp.concatenate([u[h].astype(BF16), vh_s[c, h]], axis=0) for h in hd]
        y = [pp[h][t:] + _mm(ar_s[c, h], uv[h]) for h in hd]
        w_t = wt_s[c]
        for h in hd:
            s_scr[h] = s_scr[h] * w_t[0:1, hsl[h]] + _mm_tn(uv[h], bk_s[c, h])
        y_s[rows, :] = jnp.concatenate(y, axis=1)

    groups = [list(range(g, min(g + grp, nc))) for g in range(0, nc, grp)]
    chunk_matrices(groups[0])
    for gi in range(1, len(groups)):
        for c in groups[gi - 1]:
            chunk_state(c)
        chunk_matrices(groups[gi])
    for c in groups[-1]:
        chunk_state(c)

    y = y_s[...]
    yc = y - _headsum(y, e) * (1.0 / HEAD_DIM)
    var = _headsum(yc * yc, e) * (1.0 / HEAD_DIM)
    yn = yc * lax.rsqrt(var + LN_EPS)
    y_ref[0] = ((yn * lnw_ref[...] + lnb_ref[...] + bon_s[...]) * g_s[...]).astype(BF16)

    last_tile, last_row = (l_real - 1) // tl, (l_real - 1) % tl

    @pl.when(ti == last_tile)
    def _():
        sho_ref[0] = ubuf[SUBLANES + last_row:SUBLANES + last_row + 1, :]

    @pl.when(ti == nt - 1)
    def _():
        so_ref[0] = s_scr[...]


def _rwkv(ua, shift0, wkv0, prm, l_real):
    b, lp, _ = ua.shape
    t = min(WKV_CHUNK, lp)
    tl = _largest_tile(lp, TIME_TILE, t)
    nt = lp // tl
    tril = (jnp.arange(t)[:, None] >= jnp.arange(t)[None, :]).astype(BF16)
    consts = [prm["mu"], prm["w0"], prm["a0"], prm["w2a"], prm["g2"], prm["kk"], prm["ka"], prm["rk"],
              prm["lnw"], prm["lnb"], prm["e"], tril]
    big = lambda: pltpu.VMEM((tl, W_MIX), F32)
    nc = tl // t
    per_head = lambda r, c, dt: pltpu.VMEM((nc, N_HEADS, r, c), dt)
    chunk_scratch = [per_head(2 * t, HEAD_DIM, BF16), per_head(2 * t, HEAD_DIM, BF16),
                     per_head(t, HEAD_DIM, BF16), per_head(t, HEAD_DIM, F32), per_head(t, t, BF16),
                     per_head(t, 2 * t, BF16), pltpu.VMEM((nc, SUBLANES, W_MIX), F32)]
    return pl.pallas_call(
        functools.partial(_rwkv_kernel, tl=tl, t=t, l_real=l_real, nt=nt),
        grid=(b, nt),
        in_specs=[pl.BlockSpec((1, tl, A_COLS), lambda i, j: (i, j, 0)),
                  pl.BlockSpec((1, 1, A_COLS), lambda i, j: (i, 0, 0)),
                  pl.BlockSpec((1, N_HEADS, HEAD_DIM, HEAD_DIM), lambda i, j: (i, 0, 0, 0))]
                 + [_const_spec(c.shape) for c in consts],
        out_specs=[pl.BlockSpec((1, tl, W_MIX), lambda i, j: (i, j, 0)),
                   pl.BlockSpec((1, 1, A_COLS), lambda i, j: (i, 0, 0)),
                   pl.BlockSpec((1, N_HEADS, HEAD_DIM, HEAD_DIM), lambda i, j: (i, 0, 0, 0))],
        out_shape=[jax.ShapeDtypeStruct((b, lp, W_MIX), BF16),
                   jax.ShapeDtypeStruct((b, 1, A_COLS), F32),
                   jax.ShapeDtypeStruct((b, N_HEADS, HEAD_DIM, HEAD_DIM), F32)],
        scratch_shapes=[pltpu.VMEM((tl + SUBLANES, A_COLS), F32), pltpu.VMEM((SUBLANES, A_COLS), F32),
                        pltpu.VMEM((N_HEADS, HEAD_DIM, HEAD_DIM), F32)] + [big() for _ in range(9)]
                       + chunk_scratch,
        compiler_params=_cparams("parallel", "arbitrary"),
        name="rwkv",
    )(ua, shift0, wkv0, *consts)


def _ffn_kernel(*refs, tl, tf, dgrp, l_real, mixer_out, trim):
    if mixer_out:
        x_ref, ya_ref, yb_ref, wo_ref = refs[:4]
        refs = refs[4:]
        y_ab = jnp.concatenate([ya_ref[0], yb_ref[0]], axis=1)
        x = x_ref[0] + jnp.dot(y_ab, wo_ref[...], preferred_element_type=F32)
    else:
        x = refs[0][0]
        refs = refs[1:]
    g_ref, wu_ref, cw_ref, cb_ref, wd_ref, buf_ref, o_ref, st_ref, up_scr, act_scr, carry_scr = refs
    ti = pl.program_id(1)
    h = _rms(x, g_ref[...]).astype(BF16)
    last_tile, last_row = (l_real - 1) // tl, (l_real - 1) % tl

    @pl.when(ti == 0)
    def _():
        carry_scr[SUBLANES - (CONV_W - 1):SUBLANES, :] = buf_ref[0]

    nf = D_FF // tf
    col = lambda part, f: slice(part * D_FF + f * tf, part * D_FF + (f + 1) * tf)

    def up(f):
        for part in range(2):
            cols = col(part, f)
            up_scr[0:SUBLANES, cols] = carry_scr[:, cols]
            up_scr[SUBLANES:SUBLANES + tl, cols] = jnp.dot(h, wu_ref[:, cols], preferred_element_type=F32)
            carry_scr[:, cols] = up_scr[tl:tl + SUBLANES, cols]

    ahead = 2
    for f in range(min(ahead, nf)):
        up(f)
    out = x
    g0 = 0
    for f in range(nf):
        z = [cb_ref[:, col(part, f)]
             + cw_ref[0:1, col(part, f)] * up_scr[SUBLANES - 2:SUBLANES - 2 + tl, col(part, f)]
             + cw_ref[1:2, col(part, f)] * up_scr[SUBLANES - 1:SUBLANES - 1 + tl, col(part, f)]
             + cw_ref[2:3, col(part, f)] * up_scr[SUBLANES:SUBLANES + tl, col(part, f)]
             for part in range(2)]
        val, gate = z
        act_scr[:, f * tf:(f + 1) * tf] = (gate * _sigmoid(gate) * val).astype(BF16)
        if (f + 1) % dgrp == 0 or f + 1 == nf:
            g1 = (f + 1) * tf
            out = out + jnp.dot(act_scr[:, g0:g1], wd_ref[g0:g1, :], preferred_element_type=F32)
            g0 = g1
        if f + ahead < nf:
            up(f + ahead)
    if trim is None:
        o_ref[0] = out
    else:
        row0, nrows = trim
        end_tile, end_n = (row0 + nrows - 1) // tl, (row0 + nrows - 1) % tl + 1
        if end_tile == 0:
            o_ref[0] = out[row0:row0 + nrows]
        else:
            @pl.when(ti == 0)
            def _():
                o_ref[0, 0:tl - row0, :] = out[row0:tl]

            @pl.when((ti > 0) & (ti < end_tile))
            def _():
                o_ref[0, pl.ds(pl.multiple_of(ti * tl - row0, SUBLANES), tl), :] = out

            @pl.when(ti == end_tile)
            def _():
                o_ref[0, end_tile * tl - row0:end_tile * tl - row0 + end_n, :] = out[0:end_n]

    @pl.when(ti == last_tile)
    def _():
        st_ref[0] = up_scr[SUBLANES + last_row - 1:SUBLANES + last_row + 1, :]


def _ffn(x, g, wu, cw, cb, wd, buf, l_real, mixer=None, trim=None):
    b, lp, _ = x.shape
    tl = _largest_tile(lp, FFN_TILE, SUBLANES)
    tf, dgrp = 256, 4
    assert l_real >= CONV_W - 1 and (l_real - 1) % tl >= 1
    row = lambda c: pl.BlockSpec((1, tl, c), lambda i, j: (i, j, 0))
    mix_args, mix_specs = [], []
    if mixer is not None:
        mix_args = list(mixer)
        mix_specs = [row(W_MIX), row(W_MIX), _const_spec(mixer[2].shape)]
    y_spec, y_rows = row(D_MODEL), lp
    if trim is not None:
        assert trim[0] % SUBLANES == 0 and trim[0] < tl and trim[0] + trim[1] <= l_real
        y_spec, y_rows = pl.BlockSpec((1, trim[1], D_MODEL), lambda i, j: (i, 0, 0)), trim[1]
    return pl.pallas_call(
        functools.partial(_ffn_kernel, tl=tl, tf=tf, dgrp=dgrp, l_real=l_real, mixer_out=mixer is not None,
                          trim=trim),
        grid=(b, lp // tl),
        in_specs=[row(D_MODEL)] + mix_specs
                 + [_const_spec(g.shape), _const_spec(wu.shape), _const_spec(cw.shape), _const_spec(cb.shape),
                    _const_spec(wd.shape),
                    pl.BlockSpec((1, CONV_W - 1, 2 * D_FF), lambda i, j: (i, 0, 0))],
        out_specs=[y_spec, pl.BlockSpec((1, CONV_W - 1, 2 * D_FF), lambda i, j: (i, 0, 0))],
        out_shape=[jax.ShapeDtypeStruct((b, y_rows, D_MODEL), F32),
                   jax.ShapeDtypeStruct((b, CONV_W - 1, 2 * D_FF), F32)],
        scratch_shapes=[pltpu.VMEM((tl + SUBLANES, 2 * D_FF), F32), pltpu.VMEM((tl, D_FF), BF16),
                        pltpu.VMEM((SUBLANES, 2 * D_FF), F32)],
        compiler_params=_cparams("parallel", "arbitrary"),
        name="ffn",
    )(x, *mix_args, g, wu, cw, cb, wd, buf)


def _s5_kernel(x_ref, g_ref, wbr_ref, wbi_ref, wcr_ref, wci_ref, are_ref, aim_ref, d_ref, wg_ref,
               s0r_ref, s0i_ref, o_ref, sor_ref, soi_ref, bu_scr, st_scr, x_scr, *, tt, l_real):
    ti = pl.program_id(1)
    nb = S5_BATCH
    m = nb * tt
    ntile = S5_STATE // LANES

    @pl.when(ti == 0)
    def _():
        for c in range(ntile):
            st_scr[c] = s0r_ref[:, c * LANES:(c + 1) * LANES]
            st_scr[ntile + c] = s0i_ref[:, c * LANES:(c + 1) * LANES]

    for t in range(tt):
        x_scr[t * nb:(t + 1) * nb, :] = x_ref[:, t, :]
    x = x_scr[...]
    u = _rms(x, g_ref[...])
    ub = u.astype(BF16)
    tpq = LANES // GC * P_C // LANES
    for q in range(D_MODEL // LANES):
        uq = ub[:, q * LANES:(q + 1) * LANES]
        br = jnp.dot(uq, wbr_ref[q], preferred_element_type=F32)
        bi = jnp.dot(uq, wbi_ref[q], preferred_element_type=F32)
        for c in range(tpq):
            bu_scr[q * tpq + c] = br[:, c * LANES:(c + 1) * LANES]
            bu_scr[ntile + q * tpq + c] = bi[:, c * LANES:(c + 1) * LANES]

    grp = 4
    for cb in range(ntile // grp):
        tiles = list(range(cb * grp, (cb + 1) * grp))
        a_re = [jnp.broadcast_to(are_ref[:, c * LANES:(c + 1) * LANES], (nb, LANES)) for c in tiles]
        a_im = [jnp.broadcast_to(aim_ref[:, c * LANES:(c + 1) * LANES], (nb, LANES)) for c in tiles]
        s_re = [st_scr[c] for c in tiles]
        s_im = [st_scr[ntile + c] for c in tiles]
        for t in range(tt):
            rows = slice(t * nb, (t + 1) * nb)
            for i, c in enumerate(tiles):
                n_re = a_re[i] * s_re[i] - a_im[i] * s_im[i] + bu_scr[c, rows, :]
                n_im = a_re[i] * s_im[i] + a_im[i] * s_re[i] + bu_scr[ntile + c, rows, :]
                bu_scr[c, rows, :] = n_re
                bu_scr[ntile + c, rows, :] = n_im
                s_re[i], s_im[i] = n_re, n_im
        for i, c in enumerate(tiles):
            st_scr[c] = s_re[i]
            st_scr[ntile + c] = s_im[i]

    ys = []
    for q in range(D_MODEL // LANES):
        sr = jnp.concatenate([bu_scr[q * tpq + c] for c in range(tpq)], axis=1).astype(BF16)
        si = jnp.concatenate([bu_scr[ntile + q * tpq + c] for c in range(tpq)], axis=1).astype(BF16)
        ys.append(jnp.dot(sr, wcr_ref[q], preferred_element_type=F32)
                  + jnp.dot(si, wci_ref[q], preferred_element_type=F32))
    yc = jnp.concatenate(ys, axis=1) + d_ref[...] * u
    z = jnp.dot(_gelu(yc).astype(BF16), wg_ref[...], preferred_element_type=F32)
    mix = z[:, 0:D_MODEL] * _sigmoid(z[:, D_MODEL:])
    x_scr[...] = x + mix
    for t in range(tt):
        o_ref[:, t, :] = x_scr[t * nb:(t + 1) * nb, :]
    last_tile, last_row = (l_real - 1) // tt, (l_real - 1) % tt

    @pl.when(ti == last_tile)
    def _():
        rows = slice(last_row * nb, (last_row + 1) * nb)
        for c in range(ntile):
            sor_ref[:, c * LANES:(c + 1) * LANES] = bu_scr[c, rows, :]
            soi_ref[:, c * LANES:(c + 1) * LANES] = bu_scr[ntile + c, rows, :]


def _s5(x, g, sp, s0r, s0i, l_real):
    b, lp, _ = x.shape
    assert b % S5_BATCH == 0
    tt = _largest_tile(lp, 32, SUBLANES)
    m = S5_BATCH * tt
    consts = [g, sp["wbr"], sp["wbi"], sp["wcr"], sp["wci"], sp["are"], sp["aim"], sp["d"], sp["wglu"]]
    st = pl.BlockSpec((S5_BATCH, S5_STATE), lambda i, j: (i, 0))
    return pl.pallas_call(
        functools.partial(_s5_kernel, tt=tt, l_real=l_real),
        grid=(b // S5_BATCH, lp // tt),
        in_specs=[pl.BlockSpec((S5_BATCH, tt, D_MODEL), lambda i, j: (i, j, 0))]
                 + [_const_spec(c.shape) for c in consts] + [st, st],
        out_specs=[pl.BlockSpec((S5_BATCH, tt, D_MODEL), lambda i, j: (i, j, 0)), st, st],
        out_shape=[jax.ShapeDtypeStruct((b, lp, D_MODEL), F32),
                   jax.ShapeDtypeStruct((b, S5_STATE), F32), jax.ShapeDtypeStruct((b, S5_STATE), F32)],
        scratch_shapes=[pltpu.VMEM((2 * S5_STATE // LANES, m, LANES), F32),
                        pltpu.VMEM((2 * S5_STATE // LANES, S5_BATCH, LANES), F32),
                        pltpu.VMEM((m, D_MODEL), F32)],
        compiler_params=_cparams("parallel", "arbitrary"),
        name="s5",
    )(x, *consts, s0r, s0i)


def _prep_params(p):
    j = 0
    head_of = jnp.arange(W_MIX) // HEAD_DIM
    e = (head_of[:, None] == head_of[None, :]).astype(BF16)
    w_in = p["w_in_e"][j]
    wf = jnp.zeros((D_MODEL, LANES), F32).at[:, :N_HEADS].set(w_in[:, A_COLS + 3 * W_MIX:])
    wb = jnp.concatenate([w_in[:, A_COLS:A_COLS + 3 * W_MIX], wf], axis=1).astype(BF16)
    fb = jnp.zeros((1, LANES), F32).at[0, :N_HEADS].set(p["b_fbias"][j])
    w2a = jnp.zeros((R_W + R_A, 2 * W_MIX), F32)
    w2a = w2a.at[:R_W, :W_MIX].set(p["a_w2"][j]).at[R_W:, W_MIX:].set(p["a_a2"][j])
    row = lambda a: a.reshape(1, -1).astype(F32)
    rw = dict(mu=row(p["a_mu"][j]), w0=row(p["a_w0"][j]), a0=row(p["a_a0"][j]), w2a=w2a.astype(BF16),
              g2=p["a_g2"][j].astype(BF16), kk=row(p["a_kk"][j]), ka=row(p["a_ka"][j]),
              rk=row(p["a_rk"][j]), lnw=row(p["a_ln_w"][j]), lnb=row(p["a_ln_b"][j]), e=e)
    hm = (jnp.arange(N_HEADS)[:, None] == head_of[None, :]).astype(BF16)
    l0 = dict(wa=w_in[:, :A_COLS].astype(BF16), wb=wb, fb=fb, e=e, hm=hm,
              qg=row(jnp.tile(p["b_qnorm"][j], N_HEADS)), kg=row(jnp.tile(p["b_knorm"][j], N_HEADS)),
              w_out=p["w_out_e"][j].astype(BF16))
    lam_re, lam_im = p["c_lam_re"][j].astype(F32), p["c_lam_im"][j].astype(F32)
    dt = jnp.exp(p["c_log_dt"][j].astype(F32))[:, None]
    mag = jnp.exp(lam_re * dt)
    ab_re, ab_im = mag * jnp.cos(lam_im * dt), mag * jnp.sin(lam_im * dt)
    den = lam_re * lam_re + lam_im * lam_im
    f_re = ((ab_re - 1.0) * lam_re + ab_im * lam_im) / den
    f_im = (ab_im * lam_re - (ab_re - 1.0) * lam_im) / den
    b_re, b_im = p["c_b_re"][j], p["c_b_im"][j]
    bb_re = f_re[..., None] * b_re - f_im[..., None] * b_im
    bb_im = f_re[..., None] * b_im + f_im[..., None] * b_re
    gpt = LANES // GC
    eye = jnp.eye(gpt, dtype=F32)

    def in_layout(bb):
        t = bb.reshape(G_C // gpt, gpt, P_C, GC).transpose(0, 1, 3, 2)
        return jnp.einsum("qgcp,gh->qgchp", t, eye).reshape(G_C // gpt, gpt * GC, gpt * P_C).astype(BF16)

    def out_layout(cc):
        t = cc.reshape(G_C // gpt, gpt, GC, P_C).transpose(0, 1, 3, 2)
        return jnp.einsum("qgpc,gh->qgphc", t, eye).reshape(G_C // gpt, gpt * P_C, gpt * GC).astype(BF16)

    s5 = dict(wbr=in_layout(bb_re), wbi=in_layout(bb_im), wcr=out_layout(p["c_c_re"][j]),
              wci=out_layout(-p["c_c_im"][j]), are=ab_re.reshape(1, S5_STATE), aim=ab_im.reshape(1, S5_STATE),
              d=row(p["c_d"][j]), wglu=p["w_glu"][j].astype(BF16))
    ffn = [dict(g=row(p["norm_ffn"][i]), wu=p["w_up"][i].astype(BF16), cw=p["conv_w"][i].astype(F32),
                cb=row(p["conv_b"][i]), wd=p["w_down"][i].astype(BF16)) for i in range(2)]
    return dict(l0=l0, rw=rw, s5=s5, ffn=ffn, g_mix=[row(p["norm_mix"][i]) for i in range(2)])


def _trunk(x, l_real, init, pp, past, y_trim=None):
    b, lp, _ = x.shape
    l0 = pp["l0"]
    ua, q, kf, kb, vf, vb, lf = _inproj(x, l_real, pp["g_mix"][0], l0["wa"], l0["wb"], l0["e"], l0["qg"],
                                        l0["kg"], l0["fb"])
    y_a, a_shift, a_wkv = _rwkv(ua, init["a_shift"].reshape(b, 1, A_COLS), init["a_wkv"], pp["rw"], l_real)
    lf_row = jnp.transpose(lf[:, :, :N_HEADS], (0, 2, 1)).reshape(b * N_HEADS, lp)
    if past is None:
        lpad = -lp % LANES
        c_row = _cumsum_lanes(jnp.pad(lf_row, ((0, 0), (0, lpad))))[:, :lp].reshape(b, N_HEADS, lp)
        cq = jnp.pad(jnp.transpose(c_row, (0, 2, 1)), ((0, 0), (0, 0), (0, LANES - N_HEADS)))
        y_b = _attn_prompt(q, kb, vb, cq, c_row, min(ATTN_TILE, lp))
    else:
        pk, pv, plf = past
        plen = pk.shape[1]
        plf_row = jnp.transpose(plf, (0, 2, 1)).reshape(b * N_HEADS, plen)
        c_all = _cumsum_lanes(jnp.concatenate([plf_row, jnp.pad(lf_row, ((0, 0), (0, -lp % LANES)))], axis=1))
        c_past = c_all[:, :plen].reshape(b, N_HEADS, plen)
        c_new = c_all[:, plen:plen + lp].reshape(b, N_HEADS, lp)
        cq_stack = jnp.broadcast_to(c_new.reshape(b, N_HEADS * lp, 1), (b, N_HEADS * lp, LANES))
        y_b = _attn_sample(q, cq_stack, pk.reshape(b, plen, W_MIX), pv.reshape(b, plen, W_MIX), c_past,
                           kb, vb, c_new, l0["hm"])
    f0 = pp["ffn"][0]
    x2, conv0 = _ffn(x, f0["g"], f0["wu"], f0["cw"], f0["cb"], f0["wd"], init["ffn_conv"][0], l_real,
                     mixer=(y_a, y_b, l0["w_out"]))
    x3, c_re, c_im = _s5(x2, pp["g_mix"][1], pp["s5"], init["c_re"].reshape(b, S5_STATE),
                         init["c_im"].reshape(b, S5_STATE), l_real)
    f1 = pp["ffn"][1]
    x4, conv1 = _ffn(x3, f1["g"], f1["wu"], f1["cw"], f1["cb"], f1["wd"], init["ffn_conv"][1], l_real,
                     trim=y_trim)
    states = (a_shift.reshape(1, b, A_COLS), a_wkv[None], kf[None], vf[None],
              lf[None, :, :l_real, :N_HEADS], c_re.reshape(1, b, G_C, P_C), c_im.reshape(1, b, G_C, P_C),
              jnp.stack([conv0, conv1], axis=0))
    return x4, states


def kernel(x_prompt, x_sample, state_a_shift, state_a_wkv, cache_b_k, cache_b_v, cache_b_logf, state_c_re, state_c_im, state_ffn_conv, meta, norm_mix, norm_ffn, w_in_e, a_mu, a_w0, a_w2, a_a0, a_a2, a_g2, a_kk, a_ka, a_rk, a_ln_w, a_ln_b, b_fbias, b_qnorm, b_knorm, w_out_e, c_lam_re, c_lam_im, c_log_dt, c_b_re, c_b_im, c_c_re, c_c_im, c_d, w_glu, w_up, conv_w, conv_b, w_down):
    pp = _prep_params(dict(
        norm_mix=norm_mix, norm_ffn=norm_ffn, w_in_e=w_in_e, a_mu=a_mu, a_w0=a_w0, a_w2=a_w2, a_a0=a_a0,
        a_a2=a_a2, a_g2=a_g2, a_kk=a_kk, a_ka=a_ka, a_rk=a_rk, a_ln_w=a_ln_w, a_ln_b=a_ln_b,
        b_fbias=b_fbias, b_qnorm=b_qnorm, b_knorm=b_knorm, w_out_e=w_out_e, c_lam_re=c_lam_re,
        c_lam_im=c_lam_im, c_log_dt=c_log_dt, c_b_re=c_b_re, c_b_im=c_b_im, c_c_re=c_c_re, c_c_im=c_c_im,
        c_d=c_d, w_glu=w_glu, w_up=w_up, conv_w=conv_w, conv_b=conv_b, w_down=w_down))
    b, seq, _ = x_prompt.shape
    n_meta = meta.shape[0]
    lr = n_meta + seq
    lp = -(-lr // WKV_CHUNK) * WKV_CHUNK
    xp = jnp.concatenate([jnp.broadcast_to(meta.astype(F32)[None], (b, n_meta, D_MODEL)), x_prompt,
                          jnp.zeros((b, lp - lr, D_MODEL), F32)], axis=1)
    init_p = dict(a_shift=jnp.zeros((b, A_COLS), F32),
                  a_wkv=jnp.zeros((b, N_HEADS, HEAD_DIM, HEAD_DIM), F32),
                  c_re=jnp.zeros((b, S5_STATE), F32), c_im=jnp.zeros((b, S5_STATE), F32),
                  ffn_conv=jnp.zeros((2, b, CONV_W - 1, 2 * D_FF), F32))
    yp, st_p = _trunk(xp, lr, init_p, pp, None, y_trim=(n_meta, seq))
    bs, ls, _ = x_sample.shape
    init_s = dict(a_shift=state_a_shift[0], a_wkv=state_a_wkv[0], c_re=state_c_re[0], c_im=state_c_im[0],
                  ffn_conv=state_ffn_conv)
    past = (cache_b_k[0], cache_b_v[0], cache_b_logf[0])
    ys, st_s = _trunk(x_sample, ls, init_s, pp, past)
    return (yp, ys, *st_p, *st_s)
```

```python
import functools
import math

import jax
import jax.numpy as jnp
from jax import lax
from jax.experimental import pallas as pl
from jax.experimental.pallas import tpu as pltpu

F32, BF16 = jnp.float32, jnp.bfloat16

D_MODEL = 1024
N_HEADS = 8
HEAD_DIM = 64
W_MIX = N_HEADS * HEAD_DIM
R_W, R_A, R_G = 64, 64, 128
A_COLS = 3 * W_MIX + R_W + R_A + R_G
B_PAD_COLS = 3 * W_MIX + 128
D_FF = 2816
CONV_W = 3
G_C, GC, P_C = 64, 16, 64
S5_STATE = G_C * P_C
EPS = 1e-6
LN_EPS = 64e-5
NEG = -1e30
LOG2E = math.log2(math.e)

LANES = 128
SUBLANES = 8
VMEM_LIMIT = 56 * 1024 * 1024
WKV_CHUNK = 64
WKV_SUB = 16
WKV_GROUP = 4
S5_BATCH = 8
TIME_TILE = 704
FFN_TILE = 528
FFN_TILE_TRIM = 352
INPROJ_TILE = 352
ATTN_TILE = 768


def _cparams(*sem):
    return pltpu.CompilerParams(dimension_semantics=sem, vmem_limit_bytes=VMEM_LIMIT)


def _const_spec(shape):
    nd = len(shape)
    return pl.BlockSpec(shape, lambda *_: (0,) * nd, pipeline_mode=pl.Buffered(1))


def _largest_tile(n, cap, mult):
    best = None
    for t in range(mult, min(n, cap) + 1, mult):
        if n % t == 0:
            best = t
    assert best is not None, (n, cap, mult)
    return best


def _mm(a, b):
    return jnp.dot(a.astype(BF16), b.astype(BF16), preferred_element_type=F32)


def _mm_nt(a, b):
    return lax.dot_general(a.astype(BF16), b.astype(BF16), (((1,), (1,)), ((), ())),
                           preferred_element_type=F32)


def _mm_tn(a, b):
    return lax.dot_general(a.astype(BF16), b.astype(BF16), (((0,), (0,)), ((), ())),
                           preferred_element_type=F32)


def _split3(x):
    hi = x.astype(BF16)
    r = x - hi.astype(F32)
    mid = r.astype(BF16)
    lo = (r - mid.astype(F32)).astype(BF16)
    return hi, mid, lo


def _mm_exact_rhs(x, m):
    hi, mid, lo = _split3(x)
    return (jnp.dot(hi, m, preferred_element_type=F32) + jnp.dot(mid, m, preferred_element_type=F32)
            + jnp.dot(lo, m, preferred_element_type=F32))


def _headsum(x, e):
    return jnp.dot(x.astype(BF16), e, preferred_element_type=F32)


def _rms(x, g):
    return x * lax.rsqrt(jnp.mean(x * x, axis=-1, keepdims=True) + EPS) * g


def _sigmoid(x):
    return 1.0 / (1.0 + jnp.exp(-x))


def _log_sigmoid(x):
    return jnp.minimum(x, 0.0) - jnp.log(1.0 + jnp.exp(-jnp.abs(x)))


def _gelu(x):
    return 0.5 * x * (1.0 + jnp.tanh(math.sqrt(2.0 / math.pi) * (x + 0.044715 * (x * x * x))))


def _inproj_kernel(x_ref, g_ref, wa_ref, wb_ref, e_ref, qg_ref, kg_ref, fb_ref,
                   ua_ref, q_ref, kf_ref, kb_ref, vf_ref, vb_ref, lf_ref):
    h = _rms(x_ref[0], g_ref[...]).astype(BF16)
    ua_ref[0] = jnp.dot(h, wa_ref[...], preferred_element_type=F32)
    ub = jnp.dot(h, wb_ref[...], preferred_element_type=F32)
    q = ub[:, 0:W_MIX]
    k = ub[:, W_MIX:2 * W_MIX]
    v = ub[:, 2 * W_MIX:3 * W_MIX]
    f = ub[:, 3 * W_MIX:]
    e = e_ref[...]
    qn = q * lax.rsqrt(_headsum(q * q, e) * (1.0 / HEAD_DIM) + EPS) * qg_ref[...]
    kn = k * lax.rsqrt(_headsum(k * k, e) * (1.0 / HEAD_DIM) + EPS) * kg_ref[...]
    tm = q.shape[0]
    q_ref[0] = (qn * (HEAD_DIM ** -0.5 * LOG2E)).astype(BF16)
    kf_ref[0] = kn.reshape(tm, N_HEADS, HEAD_DIM)
    kb_ref[0] = kn.astype(BF16)
    vf_ref[0] = v.reshape(tm, N_HEADS, HEAD_DIM)
    vb_ref[0] = v.astype(BF16)
    lf = _log_sigmoid(f + fb_ref[...])
    lane = lax.broadcasted_iota(jnp.int32, lf.shape, 1)
    lf_ref[0] = jnp.where(lane < N_HEADS, lf, 0.0)


def _inproj(x, l_real, g, wa, wb, e, qg, kg, fb):
    b, lp, _ = x.shape
    tm = _largest_tile(lp, INPROJ_TILE, SUBLANES)
    row = lambda c: pl.BlockSpec((1, tm, c), lambda i, j: (i, j, 0))
    cache = pl.BlockSpec((1, tm, N_HEADS, HEAD_DIM), lambda i, j: (i, j, 0, 0))
    rowshape = lambda c, dt: jax.ShapeDtypeStruct((b, lp, c), dt)
    cacheshape = jax.ShapeDtypeStruct((b, l_real, N_HEADS, HEAD_DIM), F32)
    return pl.pallas_call(
        _inproj_kernel,
        grid=(b, lp // tm),
        in_specs=[row(D_MODEL), _const_spec(g.shape), _const_spec(wa.shape), _const_spec(wb.shape),
                  _const_spec(e.shape), _const_spec(qg.shape), _const_spec(kg.shape), _const_spec(fb.shape)],
        out_specs=[row(A_COLS), row(W_MIX), cache, row(W_MIX), cache, row(W_MIX), row(LANES)],
        out_shape=[rowshape(A_COLS, F32), rowshape(W_MIX, BF16), cacheshape, rowshape(W_MIX, BF16),
                   cacheshape, rowshape(W_MIX, BF16), rowshape(LANES, F32)],
        compiler_params=_cparams("parallel", "parallel"),
        name="inproj",
    )(x, g, wa, wb, e, qg, kg, fb)


def _cumsum_kernel(x_ref, tri_ref, o_ref, *, nblk):
    tri = tri_ref[...]
    carry = jnp.zeros((x_ref.shape[0], 1), F32)
    for j in range(nblk):
        cs = _mm_exact_rhs(x_ref[:, j * LANES:(j + 1) * LANES], tri) + carry
        o_ref[:, j * LANES:(j + 1) * LANES] = cs * LOG2E
        carry = cs[:, LANES - 1:LANES]


def _cumsum_lanes(x):
    rows, n = x.shape
    assert n % LANES == 0
    tri = (jnp.arange(LANES)[:, None] <= jnp.arange(LANES)[None, :]).astype(BF16)
    return pl.pallas_call(
        functools.partial(_cumsum_kernel, nblk=n // LANES),
        grid=(1,),
        in_specs=[_const_spec(x.shape), _const_spec(tri.shape)],
        out_specs=_const_spec(x.shape),
        out_shape=jax.ShapeDtypeStruct(x.shape, F32),
        compiler_params=_cparams("arbitrary"),
        name="cumsum",
    )(x, tri)


def _attn_prompt_kernel(q_ref, k_ref, v_ref, cq_ref, cr_ref, o_ref, *, bounds):
    p = pl.program_id(1)
    lane = lax.broadcasted_iota(jnp.int32, (1, LANES), 1)
    head_lanes = (lane < HEAD_DIM, lane >= HEAD_DIM)
    for qi in range(len(bounds) - 1):
        r0, r1 = bounds[qi], bounds[qi + 1]
        q2 = q_ref[0, r0:r1, :]
        cqt = cq_ref[0, r0:r1, :]
        lane_q = lax.broadcasted_iota(jnp.int32, cqt.shape, 1)
        res = []
        for hh in range(2):
            h = 2 * p + hh
            qm = jnp.where(head_lanes[hh], q2, jnp.zeros_like(q2))
            cq = jnp.sum(jnp.where(lane_q == h, cqt, 0.0), axis=-1, keepdims=True)
            ck_all = cr_ref[0, pl.ds(h, 1), :]
            m = jnp.full((r1 - r0, 1), NEG, F32)
            l = jnp.zeros((r1 - r0, 1), F32)
            acc = jnp.zeros((r1 - r0, LANES), F32)

            def block(a0, a1, c0, c1, m, l, acc, diag):
                t = lax.dot_general(qm[a0:a1], k_ref[0, c0:c1, :], (((1,), (1,)), ((), ())),
                                    preferred_element_type=F32) - ck_all[:, c0:c1]
                if diag:
                    ri = lax.broadcasted_iota(jnp.int32, t.shape, 0) + (r0 + a0)
                    ci = lax.broadcasted_iota(jnp.int32, t.shape, 1) + c0
                    t = jnp.where(ri >= ci, t, NEG)
                m_new = jnp.maximum(m, cq[a0:a1] + jnp.max(t, axis=-1, keepdims=True))
                alpha = jnp.exp2(m - m_new)
                pe = jnp.exp2(t + (cq[a0:a1] - m_new))
                l = alpha * l + jnp.sum(pe, axis=-1, keepdims=True)
                acc = alpha * acc + jnp.dot(pe.astype(BF16), v_ref[0, c0:c1, :], preferred_element_type=F32)
                return m_new, l, acc

            for ki in range(qi):
                m, l, acc = block(0, r1 - r0, bounds[ki], bounds[ki + 1], m, l, acc, False)
            half = ((r1 - r0) // 2) // LANES * LANES
            if half == 0:
                m, l, acc = block(0, r1 - r0, r0, r1, m, l, acc, True)
            else:
                m, l, acc = block(0, r1 - r0, r0, r0 + half, m, l, acc, True)
                mb, lb, ab = block(half, r1 - r0, r0 + half, r1, m[half:], l[half:], acc[half:], True)
                m = jnp.concatenate([m[:half], mb], axis=0)
                l = jnp.concatenate([l[:half], lb], axis=0)
                acc = jnp.concatenate([acc[:half], ab], axis=0)
            res.append(acc / l)
        o_ref[0, r0:r1, :] = jnp.where(head_lanes[0], res[0], res[1]).astype(BF16)


def _attn_prompt(q, k, v, cq, cr, tile):
    b, lp, _ = q.shape
    bounds = tuple(range(0, lp, tile)) + (lp,)
    col = pl.BlockSpec((1, lp, LANES), lambda i, p: (i, 0, p))
    return pl.pallas_call(
        functools.partial(_attn_prompt_kernel, bounds=bounds),
        grid=(b, N_HEADS // 2),
        in_specs=[col, col, col,
                  pl.BlockSpec((1, lp, LANES), lambda i, p: (i, 0, 0)),
                  pl.BlockSpec((1, N_HEADS, lp), lambda i, p: (i, 0, 0))],
        out_specs=col,
        out_shape=jax.ShapeDtypeStruct((b, lp, W_MIX), BF16),
        compiler_params=_cparams("parallel", "arbitrary"),
        name="attn_prompt",
    )(q, k, v, cq, cr)


def _attn_sample_kernel(q_ref, cq_ref, kp_ref, vp_ref, cp_ref, kn_ref, vn_ref, cn_ref, hm_ref, o_ref,
                        qx_scr, m_scr, l_scr, acc_scr, *, nkp, ls):
    j = pl.program_id(1)
    rows = N_HEADS * ls

    @pl.when(j == 0)
    def _():
        q = q_ref[0]
        qx = jnp.broadcast_to(q[None], (N_HEADS, ls, W_MIX)) * hm_ref[...][:, None, :]
        qx_scr[...] = qx.reshape(rows, W_MIX)
        m_scr[...] = jnp.full(m_scr.shape, NEG, F32)
        l_scr[...] = jnp.zeros(l_scr.shape, F32)
        acc_scr[...] = jnp.zeros(acc_scr.shape, F32)

    def update(k, v, ck, causal):
        tk = k.shape[0]
        s = lax.dot_general(qx_scr[...], k, (((1,), (1,)), ((), ())), preferred_element_type=F32)
        ckx = jnp.broadcast_to(ck[:, None, :], (N_HEADS, ls, tk)).reshape(rows, tk)
        s = s + cq_ref[0][:, 0:1] - ckx
        if causal:
            ri = lax.broadcasted_iota(jnp.int32, (N_HEADS, ls, tk), 1).reshape(rows, tk)
            ci = lax.broadcasted_iota(jnp.int32, (rows, tk), 1)
            s = jnp.where(ri >= ci, s, NEG)
        m = m_scr[...]
        m_new = jnp.maximum(m, jnp.max(s, axis=-1, keepdims=True))
        alpha = jnp.exp2(m - m_new)
        pe = jnp.exp2(s - m_new)
        l_scr[...] = alpha * l_scr[...] + jnp.sum(pe, axis=-1, keepdims=True)
        acc_scr[...] = alpha * acc_scr[...] + jnp.dot(pe.astype(BF16), v, preferred_element_type=F32)
        m_scr[...] = m_new

    @pl.when(j < nkp)
    def _():
        update(kp_ref[0].astype(BF16), vp_ref[0].astype(BF16), cp_ref[0], False)

    @pl.when(j == nkp)
    def _():
        update(kn_ref[0], vn_ref[0], cn_ref[0], True)
        o = (acc_scr[...] / l_scr[...]).reshape(N_HEADS, ls, W_MIX) * hm_ref[...].astype(F32)[:, None, :]
        o_ref[0] = jnp.sum(o, axis=0).astype(BF16)


def _attn_sample(q, cq_stack, kp, vp, cp, kn, vn, cn, hm):
    b, ls, _ = q.shape
    past = kp.shape[1]
    tkp = _largest_tile(past, 1024, LANES)
    nkp = past // tkp
    rows = N_HEADS * ls
    pidx = lambda i, j: (i, jnp.minimum(j, nkp - 1), 0)
    return pl.pallas_call(
        functools.partial(_attn_sample_kernel, nkp=nkp, ls=ls),
        grid=(b, nkp + 1),
        in_specs=[pl.BlockSpec((1, ls, W_MIX), lambda i, j: (i, 0, 0)),
                  pl.BlockSpec((1, rows, LANES), lambda i, j: (i, 0, 0)),
                  pl.BlockSpec((1, tkp, W_MIX), pidx),
                  pl.BlockSpec((1, tkp, W_MIX), pidx),
                  pl.BlockSpec((1, N_HEADS, tkp), lambda i, j: (i, 0, jnp.minimum(j, nkp - 1))),
                  pl.BlockSpec((1, ls, W_MIX), lambda i, j: (i, 0, 0)),
                  pl.BlockSpec((1, ls, W_MIX), lambda i, j: (i, 0, 0)),
                  pl.BlockSpec((1, N_HEADS, ls), lambda i, j: (i, 0, 0)),
                  _const_spec(hm.shape)],
        out_specs=pl.BlockSpec((1, ls, W_MIX), lambda i, j: (i, 0, 0)),
        out_shape=jax.ShapeDtypeStruct((b, ls, W_MIX), BF16),
        scratch_shapes=[pltpu.VMEM((rows, W_MIX), BF16), pltpu.VMEM((rows, 1), F32),
                        pltpu.VMEM((rows, 1), F32), pltpu.VMEM((rows, W_MIX), F32)],
        compiler_params=_cparams("parallel", "arbitrary"),
        name="attn_sample",
    )(q, cq_stack, kp, vp, cp, kn, vn, cn, hm)


def _wkv_chunk_matrices(tops, bots, vs, t):
    hd = range(len(tops))
    ri = lax.broadcasted_iota(jnp.int32, (t, t), 0)
    ci = lax.broadcasted_iota(jnp.int32, (t, t), 1)
    eye = (ri == ci).astype(F32)
    r2 = lax.broadcasted_iota(jnp.int32, (t, 2 * t), 0)
    c2 = lax.broadcasted_iota(jnp.int32, (t, 2 * t), 1)
    c2 = jnp.where(c2 >= t, c2 - t, c2)
    same = (ri // WKV_SUB) == (ci // WKV_SUB)
    aa = [_mm_nt(tops[h], bots[h]) for h in hd]
    n = [jnp.where(ri > ci, aa[h][:t, :t], 0.0) for h in hd]
    a_ak = [jnp.where(ri > ci, aa[h][:t, t:], 0.0) for h in hd]
    a_r = [jnp.where(r2 >= c2, aa[h][t:, :], 0.0) for h in hd]
    av = [_mm(a_ak[h], vs[h]) for h in hd]
    d = [jnp.where(same, n[h], 0.0) for h in hd]
    lo = [n[h] - d[h] for h in hd]
    d2 = [_mm(d[h], d[h]) for h in hd]
    x = [_mm(eye - d[h], eye + d2[h]) for h in hd]
    d4 = [_mm(d2[h], d2[h]) for h in hd]
    x = [_mm(x[h], eye + d4[h]) for h in hd]
    d8 = [_mm(d4[h], d4[h]) for h in hd]
    x = [_mm(x[h], eye + d8[h]) for h in hd]
    mb = [_mm(x[h], lo[h]) for h in hd]
    mb2 = [_mm(mb[h], mb[h]) for h in hd]
    xx = [_mm(eye + mb2[h], x[h]) for h in hd]
    ginv = [_mm(eye - mb[h], xx[h]) for h in hd]
    return ginv, a_r, av


def _rwkv_kernel(u_ref, sh_ref, s0_ref, mu_ref, w0_ref, a0_ref, w2a_ref, g2_ref, kk_ref, ka_ref,
                 rk_ref, lnw_ref, lnb_ref, e_ref, tril_ref,
                 y_ref, sho_ref, so_ref,
                 ubuf, carry, s_scr, r_s, wl_s, k_s, v_s, kk_s, b_s, y_s, bon_s, g_s,
                 top_s, bk_s, vh_s, av_s, gi_s, ar_s, wt_s, *, tl, t, l_real, nt):
    ti = pl.program_id(1)

    @pl.when(ti == 0)
    def _():
        carry[...] = jnp.broadcast_to(sh_ref[0], carry.shape)
        s_scr[...] = s0_ref[0]

    u = u_ref[0]
    ubuf[0:SUBLANES, :] = carry[...]
    ubuf[SUBLANES:SUBLANES + tl, :] = u
    u_prev = ubuf[SUBLANES - 1:SUBLANES - 1 + tl, :]
    carry[...] = u[tl - SUBLANES:tl, :]
    um = u + (u_prev - u) * mu_ref[...]
    r = um[:, 0:W_MIX]
    k = um[:, W_MIX:2 * W_MIX]
    v = um[:, 2 * W_MIX:3 * W_MIX]
    wa = um[:, 3 * W_MIX:3 * W_MIX + R_W + R_A]
    gd = um[:, 3 * W_MIX + R_W + R_A:]
    lane = lax.broadcasted_iota(jnp.int32, wa.shape, 1)
    za = _mm(jnp.where(lane < R_W, jnp.tanh(wa), wa), w2a_ref[...])
    wl = (-math.exp(-0.5)) * _sigmoid(w0_ref[...] + za[:, 0:W_MIX])
    a = _sigmoid(a0_ref[...] + za[:, W_MIX:])
    g_s[...] = _mm(_sigmoid(gd), g2_ref[...])
    e = e_ref[...]
    kk = k * kk_ref[...]
    kkn = kk * lax.rsqrt(_headsum(kk * kk, e) + 1e-12)
    k2 = k * (1.0 + (a - 1.0) * ka_ref[...])
    bon_s[...] = _headsum(r * k2 * rk_ref[...], e) * v
    valid = (ti * tl + lax.broadcasted_iota(jnp.int32, (tl, 1), 0)) < l_real
    r_s[...] = r
    wl_s[...] = jnp.where(valid, wl, 0.0)
    k_s[...] = jnp.where(valid, k2, 0.0)
    v_s[...] = jnp.where(valid, v, 0.0)
    kk_s[...] = jnp.where(valid, kkn, 0.0)
    b_s[...] = jnp.where(valid, kkn * a, 0.0)

    tril = tril_ref[...]

    hd = range(N_HEADS)
    hsl = [slice(h * HEAD_DIM, (h + 1) * HEAD_DIM) for h in hd]

    nc = tl // t
    grp = min(WKV_GROUP, nc)

    def chunk_matrices(cs):
        tops, bots, vs, bks = [], [], [], []
        for c in cs:
            rows = slice(c * t, (c + 1) * t)
            wlc = wl_s[rows, :]
            hi, mid, lo = _split3(wlc)
            cw = (jnp.dot(tril, hi, preferred_element_type=F32) + jnp.dot(tril, mid, preferred_element_type=F32)
                  + jnp.dot(tril, lo, preferred_element_type=F32))
            w_inc = jnp.exp(cw)
            w_inv = jnp.exp(-cw)
            w_prev = jnp.exp(cw - wlc)
            w_t = w_inc[t - 1:t, :]
            top = jnp.concatenate([kk_s[rows, :] * w_prev, r_s[rows, :] * w_inc], axis=0).astype(BF16)
            bot = jnp.concatenate([b_s[rows, :] * w_inv, k_s[rows, :] * w_inv], axis=0)
            bk = (bot * w_t).astype(BF16)
            bot = bot.astype(BF16)
            vc = v_s[rows, :].astype(BF16)
            tops += [top[:, hsl[h]] for h in hd]
            bots += [bot[:, hsl[h]] for h in hd]
            vs += [vc[:, hsl[h]] for h in hd]
            bks += [bk[:, hsl[h]] for h in hd]
            wt_s[c] = jnp.broadcast_to(w_t, (SUBLANES, W_MIX))
        ginv, a_r, av = _wkv_chunk_matrices(tops, bots, vs, t)
        for j, c in enumerate(cs):
            for h in hd:
                k = j * N_HEADS + h
                top_s[c, h] = tops[k]
                bk_s[c, h] = bks[k]
                vh_s[c, h] = vs[k]
                av_s[c, h] = av[k]
                gi_s[c, h] = ginv[k].astype(BF16)
                ar_s[c, h] = a_r[k].astype(BF16)

    def chunk_state(c):
        rows = slice(c * t, (c + 1) * t)
        pp = [_mm_nt(top_s[c, h], s_scr[h]) for h in hd]
        u = [_mm(gi_s[c, h], -(pp[h][:t] + av_s[c, h])) for h in hd]
        uv = [jnp.concatenate([u[h].astype(BF16), vh_s[c, h]], axis=0) for h in hd]
        y = [pp[h][t:] + _mm(ar_s[c, h], uv[h]) for h in hd]
        w_t = wt_s[c]
        for h in hd:
            s_scr[h] = s_scr[h] * w_t[0:1, hsl[h]] + _mm_tn(uv[h], bk_s[c, h])
        y_s[rows, :] = jnp.concatenate(y, axis=1)

    groups = [list(range(g, min(g + grp, nc))) for g in range(0, nc, grp)]
    chunk_matrices(groups[0])
    for gi in range(1, len(groups)):
        for c in groups[gi - 1]:
            chunk_state(c)
        chunk_matrices(groups[gi])
    for c in groups[-1]:
        chunk_state(c)

    y = y_s[...]
    yc = y - _headsum(y, e) * (1.0 / HEAD_DIM)
    var = _headsum(yc * yc, e) * (1.0 / HEAD_DIM)
    yn = yc * lax.rsqrt(var + LN_EPS)
    y_ref[0] = ((yn * lnw_ref[...] + lnb_ref[...] + bon_s[...]) * g_s[...]).astype(BF16)

    last_tile, last_row = (l_real - 1) // tl, (l_real - 1) % tl

    @pl.when(ti == last_tile)
    def _():
        sho_ref[0] = ubuf[SUBLANES + last_row:SUBLANES + last_row + 1, :]

    @pl.when(ti == nt - 1)
    def _():
        so_ref[0] = s_scr[...]


def _rwkv(ua, shift0, wkv0, prm, l_real):
    b, lp, _ = ua.shape
    t = min(WKV_CHUNK, lp)
    tl = _largest_tile(lp, TIME_TILE, t)
    nt = lp // tl
    tril = (jnp.arange(t)[:, None] >= jnp.arange(t)[None, :]).astype(BF16)
    consts = [prm["mu"], prm["w0"], prm["a0"], prm["w2a"], prm["g2"], prm["kk"], prm["ka"], prm["rk"],
              prm["lnw"], prm["lnb"], prm["e"], tril]
    big = lambda: pltpu.VMEM((tl, W_MIX), F32)
    nc = tl // t
    per_head = lambda r, c, dt: pltpu.VMEM((nc, N_HEADS, r, c), dt)
    chunk_scratch = [per_head(2 * t, HEAD_DIM, BF16), per_head(2 * t, HEAD_DIM, BF16),
                     per_head(t, HEAD_DIM, BF16), per_head(t, HEAD_DIM, F32), per_head(t, t, BF16),
                     per_head(t, 2 * t, BF16), pltpu.VMEM((nc, SUBLANES, W_MIX), F32)]
    return pl.pallas_call(
        functools.partial(_rwkv_kernel, tl=tl, t=t, l_real=l_real, nt=nt),
        grid=(b, nt),
        in_specs=[pl.BlockSpec((1, tl, A_COLS), lambda i, j: (i, j, 0)),
                  pl.BlockSpec((1, 1, A_COLS), lambda i, j: (i, 0, 0)),
                  pl.BlockSpec((1, N_HEADS, HEAD_DIM, HEAD_DIM), lambda i, j: (i, 0, 0, 0))]
                 + [_const_spec(c.shape) for c in consts],
        out_specs=[pl.BlockSpec((1, tl, W_MIX), lambda i, j: (i, j, 0)),
                   pl.BlockSpec((1, 1, A_COLS), lambda i, j: (i, 0, 0)),
                   pl.BlockSpec((1, N_HEADS, HEAD_DIM, HEAD_DIM), lambda i, j: (i, 0, 0, 0))],
        out_shape=[jax.ShapeDtypeStruct((b, lp, W_MIX), BF16),
                   jax.ShapeDtypeStruct((b, 1, A_COLS), F32),
                   jax.ShapeDtypeStruct((b, N_HEADS, HEAD_DIM, HEAD_DIM), F32)],
        scratch_shapes=[pltpu.VMEM((tl + SUBLANES, A_COLS), F32), pltpu.VMEM((SUBLANES, A_COLS), F32),
                        pltpu.VMEM((N_HEADS, HEAD_DIM, HEAD_DIM), F32)] + [big() for _ in range(9)]
                       + chunk_scratch,
        compiler_params=_cparams("parallel", "arbitrary"),
        name="rwkv",
    )(ua, shift0, wkv0, *consts)


def _ffn_kernel(*refs, tl, tf, dgrp, l_real, mixer_out, trim):
    if mixer_out:
        x_ref, ya_ref, yb_ref, wo_ref = refs[:4]
        refs = refs[4:]
        y_ab = jnp.concatenate([ya_ref[0], yb_ref[0]], axis=1)
        x = x_ref[0] + jnp.dot(y_ab, wo_ref[...], preferred_element_type=F32)
    else:
        x = refs[0][0]
        refs = refs[1:]
    g_ref, wu_ref, cw_ref, cb_ref, wd_ref, buf_ref, o_ref, st_ref, up_scr, act_scr, carry_scr = refs
    ti = pl.program_id(1)
    h = _rms(x, g_ref[...]).astype(BF16)
    last_tile, last_row = (l_real - 1) // tl, (l_real - 1) % tl

    @pl.when(ti == 0)
    def _():
        carry_scr[SUBLANES - (CONV_W - 1):SUBLANES, :] = buf_ref[0]

    nf = D_FF // tf
    col = lambda part, f: slice(part * D_FF + f * tf, part * D_FF + (f + 1) * tf)

    def up(f):
        for part in range(2):
            cols = col(part, f)
            up_scr[0:SUBLANES, cols] = carry_scr[:, cols]
            up_scr[SUBLANES:SUBLANES + tl, cols] = jnp.dot(h, wu_ref[:, cols], preferred_element_type=F32)
            carry_scr[:, cols] = up_scr[tl:tl + SUBLANES, cols]

    ahead = 2
    for f in range(min(ahead, nf)):
        up(f)
    out = x
    g0 = 0
    for f in range(nf):
        z = [cb_ref[:, col(part, f)]
             + cw_ref[0:1, col(part, f)] * up_scr[SUBLANES - 2:SUBLANES - 2 + tl, col(part, f)]
             + cw_ref[1:2, col(part, f)] * up_scr[SUBLANES - 1:SUBLANES - 1 + tl, col(part, f)]
             + cw_ref[2:3, col(part, f)] * up_scr[SUBLANES:SUBLANES + tl, col(part, f)]
             for part in range(2)]
        val, gate = z
        act_scr[:, f * tf:(f + 1) * tf] = (gate * _sigmoid(gate) * val).astype(BF16)
        if (f + 1) % dgrp == 0 or f + 1 == nf:
            g1 = (f + 1) * tf
            out = out + jnp.dot(act_scr[:, g0:g1], wd_ref[g0:g1, :], preferred_element_type=F32)
            g0 = g1
        if f + ahead < nf:
            up(f + ahead)
    if trim is None:
        o_ref[0] = out
    else:
        row0, nrows = trim
        end_tile, end_n = (row0 + nrows - 1) // tl, (row0 + nrows - 1) % tl + 1
        if end_tile == 0:
            o_ref[0] = out[row0:row0 + nrows]
        else:
            @pl.when(ti == 0)
            def _():
                o_ref[0, 0:tl - row0, :] = out[row0:tl]

            @pl.when((ti > 0) & (ti < end_tile))
            def _():
                o_ref[0, pl.ds(pl.multiple_of(ti * tl - row0, SUBLANES), tl), :] = out

            @pl.when(ti == end_tile)
            def _():
                o_ref[0, end_tile * tl - row0:end_tile * tl - row0 + end_n, :] = out[0:end_n]

    @pl.when(ti == last_tile)
    def _():
        st_ref[0] = up_scr[SUBLANES + last_row - 1:SUBLANES + last_row + 1, :]


def _ffn_streams_kernel(*refs, nb, tl, tf, mixer_out):
    if mixer_out:
        x_ref, ya_ref, yb_ref, wo_ref = refs[:4]
        refs = refs[4:]
    else:
        x_ref = refs[0]
        refs = refs[1:]
    g_ref, wu_ref, cw_ref, cb_ref, wd_ref, buf_ref, o_ref, st_ref, x3_scr, x_scr, up_scr, act_scr = refs
    m = nb * tl
    x = x_ref[...].reshape(m, D_MODEL)
    if mixer_out:
        y_ab = jnp.concatenate([ya_ref[...].reshape(m, W_MIX), yb_ref[...].reshape(m, W_MIX)], axis=1)
        x = x + jnp.dot(y_ab, wo_ref[...], preferred_element_type=F32)
    x3_scr[...] = x.reshape(nb, tl, D_MODEL)
    for t in range(tl):
        x_scr[t * nb:(t + 1) * nb, :] = x3_scr[:, t, :]
    x = x_scr[...]
    h = _rms(x, g_ref[...]).astype(BF16)
    for i in range(CONV_W - 1):
        up_scr[i * nb:(i + 1) * nb, :] = buf_ref[:, i, :]
    pre = (CONV_W - 1) * nb
    for c in range(2 * D_FF // tf):
        up_scr[pre:pre + m, c * tf:(c + 1) * tf] = jnp.dot(h, wu_ref[:, c * tf:(c + 1) * tf],
                                                            preferred_element_type=F32)
    for i in range(CONV_W - 1):
        st_ref[:, i, :] = up_scr[m + i * nb:m + (i + 1) * nb, :]
    for f in range(D_FF // tf):
        z = []
        for part in range(2):
            cols = slice(part * D_FF + f * tf, part * D_FF + (f + 1) * tf)
            z.append(cb_ref[:, cols] + cw_ref[0:1, cols] * up_scr[0:m, cols]
                     + cw_ref[1:2, cols] * up_scr[nb:nb + m, cols] + cw_ref[2:3, cols] * up_scr[pre:pre + m, cols])
        val, gate = z
        act_scr[:, f * tf:(f + 1) * tf] = (gate * _sigmoid(gate) * val).astype(BF16)
    x_scr[...] = x + jnp.dot(act_scr[...], wd_ref[...], preferred_element_type=F32)
    for t in range(tl):
        o_ref[:, t, :] = x_scr[t * nb:(t + 1) * nb, :]


def _ffn_streams(x, g, wu, cw, cb, wd, buf, mixer):
    b, tl, _ = x.shape
    nb, tf = S5_BATCH, 256
    m = nb * tl
    blk = lambda r, c: pl.BlockSpec((nb, r, c), lambda i: (i, 0, 0))
    mix_args, mix_specs = [], []
    if mixer is not None:
        mix_args = list(mixer)
        mix_specs = [blk(tl, W_MIX), blk(tl, W_MIX), _const_spec(mixer[2].shape)]
    return pl.pallas_call(
        functools.partial(_ffn_streams_kernel, nb=nb, tl=tl, tf=tf, mixer_out=mixer is not None),
        grid=(b // nb,),
        in_specs=[blk(tl, D_MODEL)] + mix_specs
                 + [_const_spec(g.shape), _const_spec(wu.shape), _const_spec(cw.shape), _const_spec(cb.shape),
                    _const_spec(wd.shape), blk(CONV_W - 1, 2 * D_FF)],
        out_specs=[blk(tl, D_MODEL), blk(CONV_W - 1, 2 * D_FF)],
        out_shape=[jax.ShapeDtypeStruct((b, tl, D_MODEL), F32),
                   jax.ShapeDtypeStruct((b, CONV_W - 1, 2 * D_FF), F32)],
        scratch_shapes=[pltpu.VMEM((nb, tl, D_MODEL), F32), pltpu.VMEM((m, D_MODEL), F32),
                        pltpu.VMEM((m + (CONV_W - 1) * nb, 2 * D_FF), F32), pltpu.VMEM((m, D_FF), BF16)],
        compiler_params=_cparams("parallel"),
        name="ffn_streams",
    )(x, *mix_args, g, wu, cw, cb, wd, buf)


def _ffn(x, g, wu, cw, cb, wd, buf, l_real, mixer=None, trim=None):
    b, lp, _ = x.shape
    if lp == l_real and lp <= WKV_CHUNK and lp % SUBLANES == 0 and b % S5_BATCH == 0 and trim is None:
        return _ffn_streams(x, g, wu, cw, cb, wd, buf, mixer)
    tl = _largest_tile(lp, FFN_TILE if trim is None else FFN_TILE_TRIM, SUBLANES)
    tf, dgrp = 256, 4
    assert l_real >= CONV_W - 1 and (l_real - 1) % tl >= 1
    row = lambda c: pl.BlockSpec((1, tl, c), lambda i, j: (i, j, 0))
    mix_args, mix_specs = [], []
    if mixer is not None:
        mix_args = list(mixer)
        mix_specs = [row(W_MIX), row(W_MIX), _const_spec(mixer[2].shape)]
    y_spec, y_rows = row(D_MODEL), lp
    if trim is not None:
        assert trim[0] % SUBLANES == 0 and trim[0] < tl and trim[0] + trim[1] <= l_real
        y_spec, y_rows = pl.BlockSpec((1, trim[1], D_MODEL), lambda i, j: (i, 0, 0)), trim[1]
    return pl.pallas_call(
        functools.partial(_ffn_kernel, tl=tl, tf=tf, dgrp=dgrp, l_real=l_real, mixer_out=mixer is not None,
                          trim=trim),
        grid=(b, lp // tl),
        in_specs=[row(D_MODEL)] + mix_specs
                 + [_const_spec(g.shape), _const_spec(wu.shape), _const_spec(cw.shape), _const_spec(cb.shape),
                    _const_spec(wd.shape),
                    pl.BlockSpec((1, CONV_W - 1, 2 * D_FF), lambda i, j: (i, 0, 0))],
        out_specs=[y_spec, pl.BlockSpec((1, CONV_W - 1, 2 * D_FF), lambda i, j: (i, 0, 0))],
        out_shape=[jax.ShapeDtypeStruct((b, y_rows, D_MODEL), F32),
                   jax.ShapeDtypeStruct((b, CONV_W - 1, 2 * D_FF), F32)],
        scratch_shapes=[pltpu.VMEM((tl + SUBLANES, 2 * D_FF), F32), pltpu.VMEM((tl, D_FF), BF16),
                        pltpu.VMEM((SUBLANES, 2 * D_FF), F32)],
        compiler_params=_cparams("parallel", "arbitrary"),
        name="ffn",
    )(x, *mix_args, g, wu, cw, cb, wd, buf)


def _s5_kernel(x_ref, g_ref, wbr_ref, wbi_ref, wcr_ref, wci_ref, are_ref, aim_ref, d_ref, wg_ref,
               s0r_ref, s0i_ref, o_ref, sor_ref, soi_ref, bu_scr, st_scr, x_scr, *, tt, l_real):
    ti = pl.program_id(1)
    nb = S5_BATCH
    m = nb * tt
    ntile = S5_STATE // LANES

    @pl.when(ti == 0)
    def _():
        for c in range(ntile):
            st_scr[c] = s0r_ref[:, c * LANES:(c + 1) * LANES]
            st_scr[ntile + c] = s0i_ref[:, c * LANES:(c + 1) * LANES]

    for t in range(tt):
        x_scr[t * nb:(t + 1) * nb, :] = x_ref[:, t, :]
    x = x_scr[...]
    u = _rms(x, g_ref[...])
    ub = u.astype(BF16)
    tpq = LANES // GC * P_C // LANES
    for q in range(D_MODEL // LANES):
        uq = ub[:, q * LANES:(q + 1) * LANES]
        br = jnp.dot(uq, wbr_ref[q], preferred_element_type=F32)
        bi = jnp.dot(uq, wbi_ref[q], preferred_element_type=F32)
        for c in range(tpq):
            bu_scr[q * tpq + c] = br[:, c * LANES:(c + 1) * LANES]
            bu_scr[ntile + q * tpq + c] = bi[:, c * LANES:(c + 1) * LANES]

    grp = 4
    for cb in range(ntile // grp):
        tiles = list(range(cb * grp, (cb + 1) * grp))
        a_re = [jnp.broadcast_to(are_ref[:, c * LANES:(c + 1) * LANES], (nb, LANES)) for c in tiles]
        a_im = [jnp.broadcast_to(aim_ref[:, c * LANES:(c + 1) * LANES], (nb, LANES)) for c in tiles]
        s_re = [st_scr[c] for c in tiles]
        s_im = [st_scr[ntile + c] for c in tiles]
        for t in range(tt):
            rows = slice(t * nb, (t + 1) * nb)
            for i, c in enumerate(tiles):
                n_re = a_re[i] * s_re[i] - a_im[i] * s_im[i] + bu_scr[c, rows, :]
                n_im = a_re[i] * s_im[i] + a_im[i] * s_re[i] + bu_scr[ntile + c, rows, :]
                bu_scr[c, rows, :] = n_re
                bu_scr[ntile + c, rows, :] = n_im
                s_re[i], s_im[i] = n_re, n_im
        for i, c in enumerate(tiles):
            st_scr[c] = s_re[i]
            st_scr[ntile + c] = s_im[i]

    ys = []
    for q in range(D_MODEL // LANES):
        sr = jnp.concatenate([bu_scr[q * tpq + c] for c in range(tpq)], axis=1).astype(BF16)
        si = jnp.concatenate([bu_scr[ntile + q * tpq + c] for c in range(tpq)], axis=1).astype(BF16)
        ys.append(jnp.dot(sr, wcr_ref[q], preferred_element_type=F32)
                  + jnp.dot(si, wci_ref[q], preferred_element_type=F32))
    yc = jnp.concatenate(ys, axis=1) + d_ref[...] * u
    z = jnp.dot(_gelu(yc).astype(BF16), wg_ref[...], preferred_element_type=F32)
    mix = z[:, 0:D_MODEL] * _sigmoid(z[:, D_MODEL:])
    x_scr[...] = x + mix
    for t in range(tt):
        o_ref[:, t, :] = x_scr[t * nb:(t + 1) * nb, :]
    last_tile, last_row = (l_real - 1) // tt, (l_real - 1) % tt

    @pl.when(ti == last_tile)
    def _():
        rows = slice(last_row * nb, (last_row + 1) * nb)
        for c in range(ntile):
            sor_ref[:, c * LANES:(c + 1) * LANES] = bu_scr[c, rows, :]
            soi_ref[:, c * LANES:(c + 1) * LANES] = bu_scr[ntile + c, rows, :]


def _s5(x, g, sp, s0r, s0i, l_real):
    b, lp, _ = x.shape
    assert b % S5_BATCH == 0
    tt = _largest_tile(lp, 32, SUBLANES)
    m = S5_BATCH * tt
    consts = [g, sp["wbr"], sp["wbi"], sp["wcr"], sp["wci"], sp["are"], sp["aim"], sp["d"], sp["wglu"]]
    st = pl.BlockSpec((S5_BATCH, S5_STATE), lambda i, j: (i, 0))
    return pl.pallas_call(
        functools.partial(_s5_kernel, tt=tt, l_real=l_real),
        grid=(b // S5_BATCH, lp // tt),
        in_specs=[pl.BlockSpec((S5_BATCH, tt, D_MODEL), lambda i, j: (i, j, 0))]
                 + [_const_spec(c.shape) for c in consts] + [st, st],
        out_specs=[pl.BlockSpec((S5_BATCH, tt, D_MODEL), lambda i, j: (i, j, 0)), st, st],
        out_shape=[jax.ShapeDtypeStruct((b, lp, D_MODEL), F32),
                   jax.ShapeDtypeStruct((b, S5_STATE), F32), jax.ShapeDtypeStruct((b, S5_STATE), F32)],
        scratch_shapes=[pltpu.VMEM((2 * S5_STATE // LANES, m, LANES), F32),
                        pltpu.VMEM((2 * S5_STATE // LANES, S5_BATCH, LANES), F32),
                        pltpu.VMEM((m, D_MODEL), F32)],
        compiler_params=_cparams("parallel", "arbitrary"),
        name="s5",
    )(x, *consts, s0r, s0i)


def _prep_params(p):
    j = 0
    head_of = jnp.arange(W_MIX) // HEAD_DIM
    e = (head_of[:, None] == head_of[None, :]).astype(BF16)
    w_in = p["w_in_e"][j]
    wf = jnp.zeros((D_MODEL, LANES), F32).at[:, :N_HEADS].set(w_in[:, A_COLS + 3 * W_MIX:])
    wb = jnp.concatenate([w_in[:, A_COLS:A_COLS + 3 * W_MIX], wf], axis=1).astype(BF16)
    fb = jnp.zeros((1, LANES), F32).at[0, :N_HEADS].set(p["b_fbias"][j])
    w2a = jnp.zeros((R_W + R_A, 2 * W_MIX), F32)
    w2a = w2a.at[:R_W, :W_MIX].set(p["a_w2"][j]).at[R_W:, W_MIX:].set(p["a_a2"][j])
    row = lambda a: a.reshape(1, -1).astype(F32)
    rw = dict(mu=row(p["a_mu"][j]), w0=row(p["a_w0"][j]), a0=row(p["a_a0"][j]), w2a=w2a.astype(BF16),
              g2=p["a_g2"][j].astype(BF16), kk=row(p["a_kk"][j]), ka=row(p["a_ka"][j]),
              rk=row(p["a_rk"][j]), lnw=row(p["a_ln_w"][j]), lnb=row(p["a_ln_b"][j]), e=e)
    hm = (jnp.arange(N_HEADS)[:, None] == head_of[None, :]).astype(BF16)
    l0 = dict(wa=w_in[:, :A_COLS].astype(BF16), wb=wb, fb=fb, e=e, hm=hm,
              qg=row(jnp.tile(p["b_qnorm"][j], N_HEADS)), kg=row(jnp.tile(p["b_knorm"][j], N_HEADS)),
              w_out=p["w_out_e"][j].astype(BF16))
    lam_re, lam_im = p["c_lam_re"][j].astype(F32), p["c_lam_im"][j].astype(F32)
    dt = jnp.exp(p["c_log_dt"][j].astype(F32))[:, None]
    mag = jnp.exp(lam_re * dt)
    ab_re, ab_im = mag * jnp.cos(lam_im * dt), mag * jnp.sin(lam_im * dt)
    den = lam_re * lam_re + lam_im * lam_im
    f_re = ((ab_re - 1.0) * lam_re + ab_im * lam_im) / den
    f_im = (ab_im * lam_re - (ab_re - 1.0) * lam_im) / den
    b_re, b_im = p["c_b_re"][j], p["c_b_im"][j]
    bb_re = f_re[..., None] * b_re - f_im[..., None] * b_im
    bb_im = f_re[..., None] * b_im + f_im[..., None] * b_re
    gpt = LANES // GC
    eye = jnp.eye(gpt, dtype=F32)

    def in_layout(bb):
        t = bb.reshape(G_C // gpt, gpt, P_C, GC).transpose(0, 1, 3, 2)
        return jnp.einsum("qgcp,gh->qgchp", t, eye).reshape(G_C // gpt, gpt * GC, gpt * P_C).astype(BF16)

    def out_layout(cc):
        t = cc.reshape(G_C // gpt, gpt, GC, P_C).transpose(0, 1, 3, 2)
        return jnp.einsum("qgpc,gh->qgphc", t, eye).reshape(G_C // gpt, gpt * P_C, gpt * GC).astype(BF16)

    s5 = dict(wbr=in_layout(bb_re), wbi=in_layout(bb_im), wcr=out_layout(p["c_c_re"][j]),
              wci=out_layout(-p["c_c_im"][j]), are=ab_re.reshape(1, S5_STATE), aim=ab_im.reshape(1, S5_STATE),
              d=row(p["c_d"][j]), wglu=p["w_glu"][j].astype(BF16))
    ffn = [dict(g=row(p["norm_ffn"][i]), wu=p["w_up"][i].astype(BF16), cw=p["conv_w"][i].astype(F32),
                cb=row(p["conv_b"][i]), wd=p["w_down"][i].astype(BF16)) for i in range(2)]
    return dict(l0=l0, rw=rw, s5=s5, ffn=ffn, g_mix=[row(p["norm_mix"][i]) for i in range(2)])


def _trunk(x, l_real, init, pp, past, y_trim=None):
    b, lp, _ = x.shape
    l0 = pp["l0"]
    ua, q, kf, kb, vf, vb, lf = _inproj(x, l_real, pp["g_mix"][0], l0["wa"], l0["wb"], l0["e"], l0["qg"],
                                        l0["kg"], l0["fb"])
    y_a, a_shift, a_wkv = _rwkv(ua, init["a_shift"].reshape(b, 1, A_COLS), init["a_wkv"], pp["rw"], l_real)
    lf_row = jnp.transpose(lf[:, :, :N_HEADS], (0, 2, 1)).reshape(b * N_HEADS, lp)
    if past is None:
        lpad = -lp % LANES
        c_row = _cumsum_lanes(jnp.pad(lf_row, ((0, 0), (0, lpad))))[:, :lp].reshape(b, N_HEADS, lp)
        cq = jnp.pad(jnp.transpose(c_row, (0, 2, 1)), ((0, 0), (0, 0), (0, LANES - N_HEADS)))
        y_b = _attn_prompt(q, kb, vb, cq, c_row, min(ATTN_TILE, lp))
    else:
        pk, pv, plf = past
        plen = pk.shape[1]
        plf_row = jnp.transpose(plf, (0, 2, 1)).reshape(b * N_HEADS, plen)
        c_all = _cumsum_lanes(jnp.concatenate([plf_row, jnp.pad(lf_row, ((0, 0), (0, -lp % LANES)))], axis=1))
        c_past = c_all[:, :plen].reshape(b, N_HEADS, plen)
        c_new = c_all[:, plen:plen + lp].reshape(b, N_HEADS, lp)
        cq_stack = jnp.broadcast_to(c_new.reshape(b, N_HEADS * lp, 1), (b, N_HEADS * lp, LANES))
        y_b = _attn_sample(q, cq_stack, pk.reshape(b, plen, W_MIX), pv.reshape(b, plen, W_MIX), c_past,
                           kb, vb, c_new, l0["hm"])
    f0 = pp["ffn"][0]
    x2, conv0 = _ffn(x, f0["g"], f0["wu"], f0["cw"], f0["cb"], f0["wd"], init["ffn_conv"][0], l_real,
                     mixer=(y_a, y_b, l0["w_out"]))
    x3, c_re, c_im = _s5(x2, pp["g_mix"][1], pp["s5"], init["c_re"].reshape(b, S5_STATE),
                         init["c_im"].reshape(b, S5_STATE), l_real)
    f1 = pp["ffn"][1]
    x4, conv1 = _ffn(x3, f1["g"], f1["wu"], f1["cw"], f1["cb"], f1["wd"], init["ffn_conv"][1], l_real,
                     trim=y_trim)
    states = (a_shift.reshape(1, b, A_COLS), a_wkv[None], kf[None], vf[None],
              lf[None, :, :l_real, :N_HEADS], c_re.reshape(1, b, G_C, P_C), c_im.reshape(1, b, G_C, P_C),
              jnp.stack([conv0, conv1], axis=0))
    return x4, states


def kernel(x_prompt, x_sample, state_a_shift, state_a_wkv, cache_b_k, cache_b_v, cache_b_logf, state_c_re, state_c_im, state_ffn_conv, meta, norm_mix, norm_ffn, w_in_e, a_mu, a_w0, a_w2, a_a0, a_a2, a_g2, a_kk, a_ka, a_rk, a_ln_w, a_ln_b, b_fbias, b_qnorm, b_knorm, w_out_e, c_lam_re, c_lam_im, c_log_dt, c_b_re, c_b_im, c_c_re, c_c_im, c_d, w_glu, w_up, conv_w, conv_b, w_down):
    pp = _prep_params(dict(
        norm_mix=norm_mix, norm_ffn=norm_ffn, w_in_e=w_in_e, a_mu=a_mu, a_w0=a_w0, a_w2=a_w2, a_a0=a_a0,
        a_a2=a_a2, a_g2=a_g2, a_kk=a_kk, a_ka=a_ka, a_rk=a_rk, a_ln_w=a_ln_w, a_ln_b=a_ln_b,
        b_fbias=b_fbias, b_qnorm=b_qnorm, b_knorm=b_knorm, w_out_e=w_out_e, c_lam_re=c_lam_re,
        c_lam_im=c_lam_im, c_log_dt=c_log_dt, c_b_re=c_b_re, c_b_im=c_b_im, c_c_re=c_c_re, c_c_im=c_c_im,
        c_d=c_d, w_glu=w_glu, w_up=w_up, conv_w=conv_w, conv_b=conv_b, w_down=w_down))
    b, seq, _ = x_prompt.shape
    n_meta = meta.shape[0]
    lr = n_meta + seq
    lp = -(-lr // WKV_CHUNK) * WKV_CHUNK
    xp = jnp.concatenate([jnp.broadcast_to(meta.astype(F32)[None], (b, n_meta, D_MODEL)), x_prompt,
                          jnp.zeros((b, lp - lr, D_MODEL), F32)], axis=1)
    init_p = dict(a_shift=jnp.zeros((b, A_COLS), F32),
                  a_wkv=jnp.zeros((b, N_HEADS, HEAD_DIM, HEAD_DIM), F32),
                  c_re=jnp.zeros((b, S5_STATE), F32), c_im=jnp.zeros((b, S5_STATE), F32),
                  ffn_conv=jnp.zeros((2, b, CONV_W - 1, 2 * D_FF), F32))
    yp, st_p = _trunk(xp, lr, init_p, pp, None, y_trim=(n_meta, seq))
    bs, ls, _ = x_sample.shape
    init_s = dict(a_shift=state_a_shift[0], a_wkv=state_a_wkv[0], c_re=state_c_re[0], c_im=state_c_im[0],
                  ffn_conv=state_ffn_conv)
    past = (cache_b_k[0], cache_b_v[0], cache_b_logf[0])
    ys, st_s = _trunk(x_sample, ls, init_s, pp, past)
    return (yp, ys, *st_p, *st_s)
```

```python
import functools
import math

import jax
import jax.numpy as jnp
from jax import lax
from jax.experimental import pallas as pl
from jax.experimental.pallas import tpu as pltpu

F32, BF16 = jnp.float32, jnp.bfloat16

D_MODEL = 1024
N_HEADS = 8
HEAD_DIM = 64
W_MIX = N_HEADS * HEAD_DIM
R_W, R_A, R_G = 64, 64, 128
A_COLS = 3 * W_MIX + R_W + R_A + R_G
B_PAD_COLS = 3 * W_MIX + 128
D_FF = 2816
CONV_W = 3
G_C, GC, P_C = 64, 16, 64
S5_STATE = G_C * P_C
EPS = 1e-6
LN_EPS = 64e-5
NEG = -1e30
LOG2E = math.log2(math.e)

LANES = 128
SUBLANES = 8
VMEM_LIMIT = 56 * 1024 * 1024
WKV_CHUNK = 64
WKV_SUB = 16
WKV_GROUP = 4
S5_BATCH = 8
TIME_TILE = 704
FFN_TILE = 528
FFN_TILE_TRIM = 352
INPROJ_TILE = 352
ATTN_TILE = 768


def _cparams(*sem):
    return pltpu.CompilerParams(dimension_semantics=sem, vmem_limit_bytes=VMEM_LIMIT)


def _const_spec(shape):
    nd = len(shape)
    return pl.BlockSpec(shape, lambda *_: (0,) * nd, pipeline_mode=pl.Buffered(1))


def _largest_tile(n, cap, mult):
    best = None
    for t in range(mult, min(n, cap) + 1, mult):
        if n % t == 0:
            best = t
    assert best is not None, (n, cap, mult)
    return best


def _mm(a, b):
    return jnp.dot(a.astype(BF16), b.astype(BF16), preferred_element_type=F32)


def _mm_nt(a, b):
    return lax.dot_general(a.astype(BF16), b.astype(BF16), (((1,), (1,)), ((), ())),
                           preferred_element_type=F32)


def _mm_tn(a, b):
    return lax.dot_general(a.astype(BF16), b.astype(BF16), (((0,), (0,)), ((), ())),
                           preferred_element_type=F32)


def _split3(x):
    hi = x.astype(BF16)
    r = x - hi.astype(F32)
    mid = r.astype(BF16)
    lo = (r - mid.astype(F32)).astype(BF16)
    return hi, mid, lo


def _mm_exact_rhs(x, m):
    hi, mid, lo = _split3(x)
    return (jnp.dot(hi, m, preferred_element_type=F32) + jnp.dot(mid, m, preferred_element_type=F32)
            + jnp.dot(lo, m, preferred_element_type=F32))


def _headsum(x, e):
    return jnp.dot(x.astype(BF16), e, preferred_element_type=F32)


def _rms(x, g):
    return x * lax.rsqrt(jnp.mean(x * x, axis=-1, keepdims=True) + EPS) * g


def _sigmoid(x):
    return 1.0 / (1.0 + jnp.exp(-x))


def _log_sigmoid(x):
    return jnp.minimum(x, 0.0) - jnp.log(1.0 + jnp.exp(-jnp.abs(x)))


def _gelu(x):
    return 0.5 * x * (1.0 + jnp.tanh(math.sqrt(2.0 / math.pi) * (x + 0.044715 * (x * x * x))))


def _inproj_kernel(x_ref, g_ref, wa_ref, wb_ref, e_ref, qg_ref, kg_ref, fb_ref,
                   ua_ref, q_ref, kf_ref, kb_ref, vf_ref, vb_ref, lf_ref):
    h = _rms(x_ref[0], g_ref[...]).astype(BF16)
    ua_ref[0] = jnp.dot(h, wa_ref[...], preferred_element_type=F32)
    ub = jnp.dot(h, wb_ref[...], preferred_element_type=F32)
    q = ub[:, 0:W_MIX]
    k = ub[:, W_MIX:2 * W_MIX]
    v = ub[:, 2 * W_MIX:3 * W_MIX]
    f = ub[:, 3 * W_MIX:]
    e = e_ref[...]
    qn = q * lax.rsqrt(_headsum(q * q, e) * (1.0 / HEAD_DIM) + EPS) * qg_ref[...]
    kn = k * lax.rsqrt(_headsum(k * k, e) * (1.0 / HEAD_DIM) + EPS) * kg_ref[...]
    tm = q.shape[0]
    q_ref[0] = (qn * (HEAD_DIM ** -0.5 * LOG2E)).astype(BF16)
    kf_ref[0] = kn.reshape(tm, N_HEADS, HEAD_DIM)
    kb_ref[0] = kn.astype(BF16)
    vf_ref[0] = v.reshape(tm, N_HEADS, HEAD_DIM)
    vb_ref[0] = v.astype(BF16)
    lf = _log_sigmoid(f + fb_ref[...])
    lane = lax.broadcasted_iota(jnp.int32, lf.shape, 1)
    lf_ref[0] = jnp.where(lane < N_HEADS, lf, 0.0)


def _inproj(x, l_real, g, wa, wb, e, qg, kg, fb):
    b, lp, _ = x.shape
    tm = _largest_tile(lp, INPROJ_TILE, SUBLANES)
    row = lambda c: pl.BlockSpec((1, tm, c), lambda i, j: (i, j, 0))
    cache = pl.BlockSpec((1, tm, N_HEADS, HEAD_DIM), lambda i, j: (i, j, 0, 0))
    rowshape = lambda c, dt: jax.ShapeDtypeStruct((b, lp, c), dt)
    cacheshape = jax.ShapeDtypeStruct((b, l_real, N_HEADS, HEAD_DIM), F32)
    return pl.pallas_call(
        _inproj_kernel,
        grid=(b, lp // tm),
        in_specs=[row(D_MODEL), _const_spec(g.shape), _const_spec(wa.shape), _const_spec(wb.shape),
                  _const_spec(e.shape), _const_spec(qg.shape), _const_spec(kg.shape), _const_spec(fb.shape)],
        out_specs=[row(A_COLS), row(W_MIX), cache, row(W_MIX), cache, row(W_MIX), row(LANES)],
        out_shape=[rowshape(A_COLS, F32), rowshape(W_MIX, BF16), cacheshape, rowshape(W_MIX, BF16),
                   cacheshape, rowshape(W_MIX, BF16), rowshape(LANES, F32)],
        compiler_params=_cparams("parallel", "parallel"),
        name="inproj",
    )(x, g, wa, wb, e, qg, kg, fb)


def _cumsum_kernel(x_ref, tri_ref, o_ref, *, nblk):
    tri = tri_ref[...]
    carry = jnp.zeros((x_ref.shape[0], 1), F32)
    for j in range(nblk):
        cs = _mm_exact_rhs(x_ref[:, j * LANES:(j + 1) * LANES], tri) + carry
        o_ref[:, j * LANES:(j + 1) * LANES] = cs * LOG2E
        carry = cs[:, LANES - 1:LANES]


def _cumsum_lanes(x):
    rows, n = x.shape
    assert n % LANES == 0
    tri = (jnp.arange(LANES)[:, None] <= jnp.arange(LANES)[None, :]).astype(BF16)
    return pl.pallas_call(
        functools.partial(_cumsum_kernel, nblk=n // LANES),
        grid=(1,),
        in_specs=[_const_spec(x.shape), _const_spec(tri.shape)],
        out_specs=_const_spec(x.shape),
        out_shape=jax.ShapeDtypeStruct(x.shape, F32),
        compiler_params=_cparams("arbitrary"),
        name="cumsum",
    )(x, tri)


def _attn_prompt_kernel(q_ref, k_ref, v_ref, cq_ref, cr_ref, o_ref, *, bounds):
    p = pl.program_id(1)
    lane = lax.broadcasted_iota(jnp.int32, (1, LANES), 1)
    head_lanes = (lane < HEAD_DIM, lane >= HEAD_DIM)
    for qi in range(len(bounds) - 1):
        r0, r1 = bounds[qi], bounds[qi + 1]
        q2 = q_ref[0, r0:r1, :]
        cqt = cq_ref[0, r0:r1, :]
        lane_q = lax.broadcasted_iota(jnp.int32, cqt.shape, 1)
        res = []
        for hh in range(2):
            h = 2 * p + hh
            qm = jnp.where(head_lanes[hh], q2, jnp.zeros_like(q2))
            cq = jnp.sum(jnp.where(lane_q == h, cqt, 0.0), axis=-1, keepdims=True)
            ck_all = cr_ref[0, pl.ds(h, 1), :]
            m = jnp.full((r1 - r0, 1), NEG, F32)
            l = jnp.zeros((r1 - r0, 1), F32)
            acc = jnp.zeros((r1 - r0, LANES), F32)

            def block(a0, a1, c0, c1, m, l, acc, diag):
                t = lax.dot_general(qm[a0:a1], k_ref[0, c0:c1, :], (((1,), (1,)), ((), ())),
                                    preferred_element_type=F32) - ck_all[:, c0:c1]
                if diag:
                    ri = lax.broadcasted_iota(jnp.int32, t.shape, 0) + (r0 + a0)
                    ci = lax.broadcasted_iota(jnp.int32, t.shape, 1) + c0
                    t = jnp.where(ri >= ci, t, NEG)
                m_new = jnp.maximum(m, cq[a0:a1] + jnp.max(t, axis=-1, keepdims=True))
                alpha = jnp.exp2(m - m_new)
                pe = jnp.exp2(t + (cq[a0:a1] - m_new))
                l = alpha * l + jnp.sum(pe, axis=-1, keepdims=True)
                acc = alpha * acc + jnp.dot(pe.astype(BF16), v_ref[0, c0:c1, :], preferred_element_type=F32)
                return m_new, l, acc

            for ki in range(qi):
                m, l, acc = block(0, r1 - r0, bounds[ki], bounds[ki + 1], m, l, acc, False)
            half = ((r1 - r0) // 2) // LANES * LANES
            if half == 0:
                m, l, acc = block(0, r1 - r0, r0, r1, m, l, acc, True)
            else:
                m, l, acc = block(0, r1 - r0, r0, r0 + half, m, l, acc, True)
                mb, lb, ab = block(half, r1 - r0, r0 + half, r1, m[half:], l[half:], acc[half:], True)
                m = jnp.concatenate([m[:half], mb], axis=0)
                l = jnp.concatenate([l[:half], lb], axis=0)
                acc = jnp.concatenate([acc[:half], ab], axis=0)
            res.append(acc / l)
        o_ref[0, r0:r1, :] = jnp.where(head_lanes[0], res[0], res[1]).astype(BF16)


def _attn_prompt(q, k, v, cq, cr, tile):
    b, lp, _ = q.shape
    bounds = tuple(range(0, lp, tile)) + (lp,)
    col = pl.BlockSpec((1, lp, LANES), lambda i, p: (i, 0, p))
    return pl.pallas_call(
        functools.partial(_attn_prompt_kernel, bounds=bounds),
        grid=(b, N_HEADS // 2),
        in_specs=[col, col, col,
                  pl.BlockSpec((1, lp, LANES), lambda i, p: (i, 0, 0)),
                  pl.BlockSpec((1, N_HEADS, lp), lambda i, p: (i, 0, 0))],
        out_specs=col,
        out_shape=jax.ShapeDtypeStruct((b, lp, W_MIX), BF16),
        compiler_params=_cparams("parallel", "arbitrary"),
        name="attn_prompt",
    )(q, k, v, cq, cr)


def _attn_sample_kernel(q_ref, cq_ref, kp_ref, vp_ref, cp_ref, kn_ref, vn_ref, cn_ref, hm_ref, o_ref,
                        qx_scr, m_scr, l_scr, acc_scr, *, nkp, ls):
    j = pl.program_id(1)
    rows = N_HEADS * ls

    @pl.when(j == 0)
    def _():
        q = q_ref[0]
        qx = jnp.broadcast_to(q[None], (N_HEADS, ls, W_MIX)) * hm_ref[...][:, None, :]
        qx_scr[...] = qx.reshape(rows, W_MIX)
        m_scr[...] = jnp.full(m_scr.shape, NEG, F32)
        l_scr[...] = jnp.zeros(l_scr.shape, F32)
        acc_scr[...] = jnp.zeros(acc_scr.shape, F32)

    def update(k, v, ck, causal):
        tk = k.shape[0]
        s = lax.dot_general(qx_scr[...], k, (((1,), (1,)), ((), ())), preferred_element_type=F32)
        ckx = jnp.broadcast_to(ck[:, None, :], (N_HEADS, ls, tk)).reshape(rows, tk)
        s = s + cq_ref[0][:, 0:1] - ckx
        if causal:
            ri = lax.broadcasted_iota(jnp.int32, (N_HEADS, ls, tk), 1).reshape(rows, tk)
            ci = lax.broadcasted_iota(jnp.int32, (rows, tk), 1)
            s = jnp.where(ri >= ci, s, NEG)
        m = m_scr[...]
        m_new = jnp.maximum(m, jnp.max(s, axis=-1, keepdims=True))
        alpha = jnp.exp2(m - m_new)
        pe = jnp.exp2(s - m_new)
        l_scr[...] = alpha * l_scr[...] + jnp.sum(pe, axis=-1, keepdims=True)
        acc_scr[...] = alpha * acc_scr[...] + jnp.dot(pe.astype(BF16), v, preferred_element_type=F32)
        m_scr[...] = m_new

    @pl.when(j < nkp)
    def _():
        update(kp_ref[0].astype(BF16), vp_ref[0].astype(BF16), cp_ref[0], False)

    @pl.when(j == nkp)
    def _():
        update(kn_ref[0], vn_ref[0], cn_ref[0], True)
        o = (acc_scr[...] / l_scr[...]).reshape(N_HEADS, ls, W_MIX) * hm_ref[...].astype(F32)[:, None, :]
        o_ref[0] = jnp.sum(o, axis=0).astype(BF16)


def _attn_sample(q, cq_stack, kp, vp, cp, kn, vn, cn, hm):
    b, ls, _ = q.shape
    past = kp.shape[1]
    tkp = _largest_tile(past, 1024, LANES)
    nkp = past // tkp
    rows = N_HEADS * ls
    pidx = lambda i, j: (i, jnp.minimum(j, nkp - 1), 0)
    return pl.pallas_call(
        functools.partial(_attn_sample_kernel, nkp=nkp, ls=ls),
        grid=(b, nkp + 1),
        in_specs=[pl.BlockSpec((1, ls, W_MIX), lambda i, j: (i, 0, 0)),
                  pl.BlockSpec((1, rows, LANES), lambda i, j: (i, 0, 0)),
                  pl.BlockSpec((1, tkp, W_MIX), pidx),
                  pl.BlockSpec((1, tkp, W_MIX), pidx),
                  pl.BlockSpec((1, N_HEADS, tkp), lambda i, j: (i, 0, jnp.minimum(j, nkp - 1))),
                  pl.BlockSpec((1, ls, W_MIX), lambda i, j: (i, 0, 0)),
                  pl.BlockSpec((1, ls, W_MIX), lambda i, j: (i, 0, 0)),
                  pl.BlockSpec((1, N_HEADS, ls), lambda i, j: (i, 0, 0)),
                  _const_spec(hm.shape)],
        out_specs=pl.BlockSpec((1, ls, W_MIX), lambda i, j: (i, 0, 0)),
        out_shape=jax.ShapeDtypeStruct((b, ls, W_MIX), BF16),
        scratch_shapes=[pltpu.VMEM((rows, W_MIX), BF16), pltpu.VMEM((rows, 1), F32),
                        pltpu.VMEM((rows, 1), F32), pltpu.VMEM((rows, W_MIX), F32)],
        compiler_params=_cparams("parallel", "arbitrary"),
        name="attn_sample",
    )(q, cq_stack, kp, vp, cp, kn, vn, cn, hm)


def _wkv_chunk_matrices(tops, bots, vs, t):
    hd = range(len(tops))
    ri = lax.broadcasted_iota(jnp.int32, (t, t), 0)
    ci = lax.broadcasted_iota(jnp.int32, (t, t), 1)
    eye = (ri == ci).astype(F32)
    r2 = lax.broadcasted_iota(jnp.int32, (t, 2 * t), 0)
    c2 = lax.broadcasted_iota(jnp.int32, (t, 2 * t), 1)
    c2 = jnp.where(c2 >= t, c2 - t, c2)
    same = (ri // WKV_SUB) == (ci // WKV_SUB)
    aa = [_mm_nt(tops[h], bots[h]) for h in hd]
    n = [jnp.where(ri > ci, aa[h][:t, :t], 0.0) for h in hd]
    a_ak = [jnp.where(ri > ci, aa[h][:t, t:], 0.0) for h in hd]
    a_r = [jnp.where(r2 >= c2, aa[h][t:, :], 0.0) for h in hd]
    av = [_mm(a_ak[h], vs[h]) for h in hd]
    d = [jnp.where(same, n[h], 0.0) for h in hd]
    lo = [n[h] - d[h] for h in hd]
    d2 = [_mm(d[h], d[h]) for h in hd]
    x = [_mm(eye - d[h], eye + d2[h]) for h in hd]
    d4 = [_mm(d2[h], d2[h]) for h in hd]
    x = [_mm(x[h], eye + d4[h]) for h in hd]
    d8 = [_mm(d4[h], d4[h]) for h in hd]
    x = [_mm(x[h], eye + d8[h]) for h in hd]
    mb = [_mm(x[h], lo[h]) for h in hd]
    mb2 = [_mm(mb[h], mb[h]) for h in hd]
    xx = [_mm(eye + mb2[h], x[h]) for h in hd]
    ginv = [_mm(eye - mb[h], xx[h]) for h in hd]
    return ginv, a_r, av


def _rwkv_kernel(u_ref, sh_ref, s0_ref, mu_ref, w0_ref, a0_ref, w2a_ref, g2_ref, kk_ref, ka_ref,
                 rk_ref, lnw_ref, lnb_ref, e_ref, tril_ref,
                 y_ref, sho_ref, so_ref,
                 ubuf, carry, s_scr, r_s, wl_s, k_s, v_s, kk_s, b_s, y_s, bon_s, g_s,
                 top_s, bk_s, vh_s, av_s, gi_s, ar_s, wt_s, *, tl, t, l_real, nt):
    ti = pl.program_id(1)

    @pl.when(ti == 0)
    def _():
        carry[...] = jnp.broadcast_to(sh_ref[0], carry.shape)
        s_scr[...] = s0_ref[0]

    u = u_ref[0]
    ubuf[0:SUBLANES, :] = carry[...]
    ubuf[SUBLANES:SUBLANES + tl, :] = u
    u_prev = ubuf[SUBLANES - 1:SUBLANES - 1 + tl, :]
    carry[...] = u[tl - SUBLANES:tl, :]
    um = u + (u_prev - u) * mu_ref[...]
    r = um[:, 0:W_MIX]
    k = um[:, W_MIX:2 * W_MIX]
    v = um[:, 2 * W_MIX:3 * W_MIX]
    wa = um[:, 3 * W_MIX:3 * W_MIX + R_W + R_A]
    gd = um[:, 3 * W_MIX + R_W + R_A:]
    lane = lax.broadcasted_iota(jnp.int32, wa.shape, 1)
    za = _mm(jnp.where(lane < R_W, jnp.tanh(wa), wa), w2a_ref[...])
    wl = (-math.exp(-0.5)) * _sigmoid(w0_ref[...] + za[:, 0:W_MIX])
    a = _sigmoid(a0_ref[...] + za[:, W_MIX:])
    g_s[...] = _mm(_sigmoid(gd), g2_ref[...])
    e = e_ref[...]
    kk = k * kk_ref[...]
    kkn = kk * lax.rsqrt(_headsum(kk * kk, e) + 1e-12)
    k2 = k * (1.0 + (a - 1.0) * ka_ref[...])
    bon_s[...] = _headsum(r * k2 * rk_ref[...], e) * v
    valid = (ti * tl + lax.broadcasted_iota(jnp.int32, (tl, 1), 0)) < l_real
    r_s[...] = r
    wl_s[...] = jnp.where(valid, wl, 0.0)
    k_s[...] = jnp.where(valid, k2, 0.0)
    v_s[...] = jnp.where(valid, v, 0.0)
    kk_s[...] = jnp.where(valid, kkn, 0.0)
    b_s[...] = jnp.where(valid, kkn * a, 0.0)

    tril = tril_ref[...]

    hd = range(N_HEADS)
    hsl = [slice(h * HEAD_DIM, (h + 1) * HEAD_DIM) for h in hd]

    nc = tl // t
    grp = min(WKV_GROUP, nc)

    def chunk_matrices(cs):
        tops, bots, vs, bks = [], [], [], []
        for c in cs:
            rows = slice(c * t, (c + 1) * t)
            wlc = wl_s[rows, :]
            hi, mid, lo = _split3(wlc)
            cw = (jnp.dot(tril, hi, preferred_element_type=F32) + jnp.dot(tril, mid, preferred_element_type=F32)
                  + jnp.dot(tril, lo, preferred_element_type=F32))
            w_inc = jnp.exp(cw)
            w_inv = jnp.exp(-cw)
            w_prev = jnp.exp(cw - wlc)
            w_t = w_inc[t - 1:t, :]
            top = jnp.concatenate([kk_s[rows, :] * w_prev, r_s[rows, :] * w_inc], axis=0).astype(BF16)
            bot = jnp.concatenate([b_s[rows, :] * w_inv, k_s[rows, :] * w_inv], axis=0)
            bk = (bot * w_t).astype(BF16)
            bot = bot.astype(BF16)
            vc = v_s[rows, :].astype(BF16)
            tops += [top[:, hsl[h]] for h in hd]
            bots += [bot[:, hsl[h]] for h in hd]
            vs += [vc[:, hsl[h]] for h in hd]
            bks += [bk[:, hsl[h]] for h in hd]
            wt_s[c] = jnp.broadcast_to(w_t, (SUBLANES, W_MIX))
        ginv, a_r, av = _wkv_chunk_matrices(tops, bots, vs, t)
        for j, c in enumerate(cs):
            for h in hd:
                k = j * N_HEADS + h
                top_s[c, h] = tops[k]
                bk_s[c, h] = bks[k]
                vh_s[c, h] = vs[k]
                av_s[c, h] = av[k]
                gi_s[c, h] = ginv[k].astype(BF16)
                ar_s[c, h] = a_r[k].astype(BF16)

    def chunk_state(c):
        rows = slice(c * t, (c + 1) * t)
        pp = [_mm_nt(top_s[c, h], s_scr[h]) for h in hd]
        u = [_mm(gi_s[c, h], -(pp[h][:t] + av_s[c, h])) for h in hd]
        uv = [jnp.concatenate([u[h].astype(BF16), vh_s[c, h]], axis=0) for h in hd]
        y = [pp[h][t:] + _mm(ar_s[c, h], uv[h]) for h in hd]
        w_t = wt_s[c]
        for h in hd:
            s_scr[h] = s_scr[h] * w_t[0:1, hsl[h]] + _mm_tn(uv[h], bk_s[c, h])
        y_s[rows, :] = jnp.concatenate(y, axis=1)

    groups = [list(range(g, min(g + grp, nc))) for g in range(0, nc, grp)]
    chunk_matrices(groups[0])
    for gi in range(1, len(groups)):
        for c in groups[gi - 1]:
            chunk_state(c)
        chunk_matrices(groups[gi])
    for c in groups[-1]:
        chunk_state(c)

    y = y_s[...]
    yc = y - _headsum(y, e) * (1.0 / HEAD_DIM)
    var = _headsum(yc * yc, e) * (1.0 / HEAD_DIM)
    yn = yc * lax.rsqrt(var + LN_EPS)
    y_ref[0] = ((yn * lnw_ref[...] + lnb_ref[...] + bon_s[...]) * g_s[...]).astype(BF16)

    last_tile, last_row = (l_real - 1) // tl, (l_real - 1) % tl

    @pl.when(ti == last_tile)
    def _():
        sho_ref[0] = ubuf[SUBLANES + last_row:SUBLANES + last_row + 1, :]

    @pl.when(ti == nt - 1)
    def _():
        so_ref[0] = s_scr[...]


def _rwkv(ua, shift0, wkv0, prm, l_real):
    b, lp, _ = ua.shape
    t = min(WKV_CHUNK, lp)
    tl = _largest_tile(lp, TIME_TILE, t)
    nt = lp // tl
    tril = (jnp.arange(t)[:, None] >= jnp.arange(t)[None, :]).astype(BF16)
    consts = [prm["mu"], prm["w0"], prm["a0"], prm["w2a"], prm["g2"], prm["kk"], prm["ka"], prm["rk"],
              prm["lnw"], prm["lnb"], prm["e"], tril]
    big = lambda: pltpu.VMEM((tl, W_MIX), F32)
    nc = tl // t
    per_head = lambda r, c, dt: pltpu.VMEM((nc, N_HEADS, r, c), dt)
    chunk_scratch = [per_head(2 * t, HEAD_DIM, BF16), per_head(2 * t, HEAD_DIM, BF16),
                     per_head(t, HEAD_DIM, BF16), per_head(t, HEAD_DIM, F32), per_head(t, t, BF16),
                     per_head(t, 2 * t, BF16), pltpu.VMEM((nc, SUBLANES, W_MIX), F32)]
    return pl.pallas_call(
        functools.partial(_rwkv_kernel, tl=tl, t=t, l_real=l_real, nt=nt),
        grid=(b, nt),
        in_specs=[pl.BlockSpec((1, tl, A_COLS), lambda i, j: (i, j, 0)),
                  pl.BlockSpec((1, 1, A_COLS), lambda i, j: (i, 0, 0)),
                  pl.BlockSpec((1, N_HEADS, HEAD_DIM, HEAD_DIM), lambda i, j: (i, 0, 0, 0))]
                 + [_const_spec(c.shape) for c in consts],
        out_specs=[pl.BlockSpec((1, tl, W_MIX), lambda i, j: (i, j, 0)),
                   pl.BlockSpec((1, 1, A_COLS), lambda i, j: (i, 0, 0)),
                   pl.BlockSpec((1, N_HEADS, HEAD_DIM, HEAD_DIM), lambda i, j: (i, 0, 0, 0))],
        out_shape=[jax.ShapeDtypeStruct((b, lp, W_MIX), BF16),
                   jax.ShapeDtypeStruct((b, 1, A_COLS), F32),
                   jax.ShapeDtypeStruct((b, N_HEADS, HEAD_DIM, HEAD_DIM), F32)],
        scratch_shapes=[pltpu.VMEM((tl + SUBLANES, A_COLS), F32), pltpu.VMEM((SUBLANES, A_COLS), F32),
                        pltpu.VMEM((N_HEADS, HEAD_DIM, HEAD_DIM), F32)] + [big() for _ in range(9)]
                       + chunk_scratch,
        compiler_params=_cparams("parallel", "arbitrary"),
        name="rwkv",
    )(ua, shift0, wkv0, *consts)


def _ffn_kernel(*refs, tl, tf, dgrp, l_real, mixer_out, trim):
    if mixer_out:
        x_ref, ya_ref, yb_ref, wo_ref = refs[:4]
        refs = refs[4:]
        y_ab = jnp.concatenate([ya_ref[0], yb_ref[0]], axis=1)
        x = x_ref[0] + jnp.dot(y_ab, wo_ref[...], preferred_element_type=F32)
    else:
        x = refs[0][0]
        refs = refs[1:]
    g_ref, wu_ref, cw_ref, cb_ref, wd_ref, buf_ref, o_ref, st_ref, up_scr, act_scr, carry_scr = refs
    ti = pl.program_id(1)
    h = _rms(x, g_ref[...]).astype(BF16)
    last_tile, last_row = (l_real - 1) // tl, (l_real - 1) % tl

    @pl.when(ti == 0)
    def _():
        carry_scr[SUBLANES - (CONV_W - 1):SUBLANES, :] = buf_ref[0]

    nf = D_FF // tf
    col = lambda part, f: slice(part * D_FF + f * tf, part * D_FF + (f + 1) * tf)

    def up(f):
        for part in range(2):
            cols = col(part, f)
            up_scr[0:SUBLANES, cols] = carry_scr[:, cols]
            up_scr[SUBLANES:SUBLANES + tl, cols] = jnp.dot(h, wu_ref[:, cols], preferred_element_type=F32)
            carry_scr[:, cols] = up_scr[tl:tl + SUBLANES, cols]

    ahead = 2
    for f in range(min(ahead, nf)):
        up(f)
    out = x
    g0 = 0
    for f in range(nf):
        z = [cb_ref[:, col(part, f)]
             + cw_ref[0:1, col(part, f)] * up_scr[SUBLANES - 2:SUBLANES - 2 + tl, col(part, f)]
             + cw_ref[1:2, col(part, f)] * up_scr[SUBLANES - 1:SUBLANES - 1 + tl, col(part, f)]
             + cw_ref[2:3, col(part, f)] * up_scr[SUBLANES:SUBLANES + tl, col(part, f)]
             for part in range(2)]
        val, gate = z
        act_scr[:, f * tf:(f + 1) * tf] = (gate * _sigmoid(gate) * val).astype(BF16)
        if (f + 1) % dgrp == 0 or f + 1 == nf:
            g1 = (f + 1) * tf
            out = out + jnp.dot(act_scr[:, g0:g1], wd_ref[g0:g1, :], preferred_element_type=F32)
            g0 = g1
        if f + ahead < nf:
            up(f + ahead)
    if trim is None:
        o_ref[0] = out
    else:
        row0, nrows = trim
        end_tile, end_n = (row0 + nrows - 1) // tl, (row0 + nrows - 1) % tl + 1
        if end_tile == 0:
            o_ref[0] = out[row0:row0 + nrows]
        else:
            @pl.when(ti == 0)
            def _():
                o_ref[0, 0:tl - row0, :] = out[row0:tl]

            @pl.when((ti > 0) & (ti < end_tile))
            def _():
                o_ref[0, pl.ds(pl.multiple_of(ti * tl - row0, SUBLANES), tl), :] = out

            @pl.when(ti == end_tile)
            def _():
                o_ref[0, end_tile * tl - row0:end_tile * tl - row0 + end_n, :] = out[0:end_n]

    @pl.when(ti == last_tile)
    def _():
        st_ref[0] = up_scr[SUBLANES + last_row - 1:SUBLANES + last_row + 1, :]


def _ffn_streams_kernel(*refs, nb, tl, tf, mixer_out):
    if mixer_out:
        x_ref, ya_ref, yb_ref, wo_ref = refs[:4]
        refs = refs[4:]
    else:
        x_ref = refs[0]
        refs = refs[1:]
    g_ref, wu_ref, cw_ref, cb_ref, wd_ref, buf_ref, o_ref, st_ref, x3_scr, x_scr, up_scr, act_scr = refs
    m = nb * tl
    x = x_ref[...].reshape(m, D_MODEL)
    if mixer_out:
        y_ab = jnp.concatenate([ya_ref[...].reshape(m, W_MIX), yb_ref[...].reshape(m, W_MIX)], axis=1)
        x = x + jnp.dot(y_ab, wo_ref[...], preferred_element_type=F32)
    x3_scr[...] = x.reshape(nb, tl, D_MODEL)
    for t in range(tl):
        x_scr[t * nb:(t + 1) * nb, :] = x3_scr[:, t, :]
    x = x_scr[...]
    h = _rms(x, g_ref[...]).astype(BF16)
    for i in range(CONV_W - 1):
        up_scr[i * nb:(i + 1) * nb, :] = buf_ref[:, i, :]
    pre = (CONV_W - 1) * nb
    for c in range(2 * D_FF // tf):
        up_scr[pre:pre + m, c * tf:(c + 1) * tf] = jnp.dot(h, wu_ref[:, c * tf:(c + 1) * tf],
                                                            preferred_element_type=F32)
    for i in range(CONV_W - 1):
        st_ref[:, i, :] = up_scr[m + i * nb:m + (i + 1) * nb, :]
    for f in range(D_FF // tf):
        z = []
        for part in range(2):
            cols = slice(part * D_FF + f * tf, part * D_FF + (f + 1) * tf)
            z.append(cb_ref[:, cols] + cw_ref[0:1, cols] * up_scr[0:m, cols]
                     + cw_ref[1:2, cols] * up_scr[nb:nb + m, cols] + cw_ref[2:3, cols] * up_scr[pre:pre + m, cols])
        val, gate = z
        act_scr[:, f * tf:(f + 1) * tf] = (gate * _sigmoid(gate) * val).astype(BF16)
    x_scr[...] = x + jnp.dot(act_scr[...], wd_ref[...], preferred_element_type=F32)
    for t in range(tl):
        o_ref[:, t, :] = x_scr[t * nb:(t + 1) * nb, :]


def _ffn_streams(x, g, wu, cw, cb, wd, buf, mixer):
    b, tl, _ = x.shape
    nb, tf = S5_BATCH, 256
    m = nb * tl
    blk = lambda r, c: pl.BlockSpec((nb, r, c), lambda i: (i, 0, 0))
    mix_args, mix_specs = [], []
    if mixer is not None:
        mix_args = list(mixer)
        mix_specs = [blk(tl, W_MIX), blk(tl, W_MIX), _const_spec(mixer[2].shape)]
    return pl.pallas_call(
        functools.partial(_ffn_streams_kernel, nb=nb, tl=tl, tf=tf, mixer_out=mixer is not None),
        grid=(b // nb,),
        in_specs=[blk(tl, D_MODEL)] + mix_specs
                 + [_const_spec(g.shape), _const_spec(wu.shape), _const_spec(cw.shape), _const_spec(cb.shape),
                    _const_spec(wd.shape), blk(CONV_W - 1, 2 * D_FF)],
        out_specs=[blk(tl, D_MODEL), blk(CONV_W - 1, 2 * D_FF)],
        out_shape=[jax.ShapeDtypeStruct((b, tl, D_MODEL), F32),
                   jax.ShapeDtypeStruct((b, CONV_W - 1, 2 * D_FF), F32)],
        scratch_shapes=[pltpu.VMEM((nb, tl, D_MODEL), F32), pltpu.VMEM((m, D_MODEL), F32),
                        pltpu.VMEM((m + (CONV_W - 1) * nb, 2 * D_FF), F32), pltpu.VMEM((m, D_FF), BF16)],
        compiler_params=_cparams("parallel"),
        name="ffn_streams",
    )(x, *mix_args, g, wu, cw, cb, wd, buf)


def _ffn(x, g, wu, cw, cb, wd, buf, l_real, mixer=None, trim=None):
    b, lp, _ = x.shape
    if lp == l_real and lp <= WKV_CHUNK and lp % SUBLANES == 0 and b % S5_BATCH == 0 and trim is None:
        return _ffn_streams(x, g, wu, cw, cb, wd, buf, mixer)
    tl = _largest_tile(lp, FFN_TILE if trim is None else FFN_TILE_TRIM, SUBLANES)
    tf, dgrp = 256, 4
    assert l_real >= CONV_W - 1 and (l_real - 1) % tl >= 1
    row = lambda c: pl.BlockSpec((1, tl, c), lambda i, j: (i, j, 0))
    mix_args, mix_specs = [], []
    if mixer is not None:
        mix_args = list(mixer)
        mix_specs = [row(W_MIX), row(W_MIX), _const_spec(mixer[2].shape)]
    y_spec, y_rows = row(D_MODEL), lp
    if trim is not None:
        assert trim[0] % SUBLANES == 0 and trim[0] < tl and trim[0] + trim[1] <= l_real
        y_spec, y_rows = pl.BlockSpec((1, trim[1], D_MODEL), lambda i, j: (i, 0, 0)), trim[1]
    return pl.pallas_call(
        functools.partial(_ffn_kernel, tl=tl, tf=tf, dgrp=dgrp, l_real=l_real, mixer_out=mixer is not None,
                          trim=trim),
        grid=(b, lp // tl),
        in_specs=[row(D_MODEL)] + mix_specs
                 + [_const_spec(g.shape), _const_spec(wu.shape), _const_spec(cw.shape), _const_spec(cb.shape),
                    _const_spec(wd.shape),
                    pl.BlockSpec((1, CONV_W - 1, 2 * D_FF), lambda i, j: (i, 0, 0))],
        out_specs=[y_spec, pl.BlockSpec((1, CONV_W - 1, 2 * D_FF), lambda i, j: (i, 0, 0))],
        out_shape=[jax.ShapeDtypeStruct((b, y_rows, D_MODEL), F32),
                   jax.ShapeDtypeStruct((b, CONV_W - 1, 2 * D_FF), F32)],
        scratch_shapes=[pltpu.VMEM((tl + SUBLANES, 2 * D_FF), F32), pltpu.VMEM((tl, D_FF), BF16),
                        pltpu.VMEM((SUBLANES, 2 * D_FF), F32)],
        compiler_params=_cparams("parallel", "arbitrary"),
        name="ffn",
    )(x, *mix_args, g, wu, cw, cb, wd, buf)


def _s5_kernel(x_ref, g_ref, wbr_ref, wbi_ref, wcr_ref, wci_ref, are_ref, aim_ref, d_ref, wg_ref,
               s0r_ref, s0i_ref, o_ref, sor_ref, soi_ref, bu_scr, st_scr, x_scr, *, tt, l_real):
    ti = pl.program_id(1)
    nb = S5_BATCH
    m = nb * tt
    ntile = S5_STATE // LANES

    @pl.when(ti == 0)
    def _():
        for c in range(ntile):
            st_scr[c] = s0r_ref[:, c * LANES:(c + 1) * LANES]
            st_scr[ntile + c] = s0i_ref[:, c * LANES:(c + 1) * LANES]

    for t in range(tt):
        x_scr[t * nb:(t + 1) * nb, :] = x_ref[:, t, :]
    x = x_scr[...]
    u = _rms(x, g_ref[...])
    ub = u.astype(BF16)
    tpq = LANES // GC * P_C // LANES
    for q in range(D_MODEL // LANES):
        uq = ub[:, q * LANES:(q + 1) * LANES]
        br = jnp.dot(uq, wbr_ref[q], preferred_element_type=F32)
        bi = jnp.dot(uq, wbi_ref[q], preferred_element_type=F32)
        for c in range(tpq):
            bu_scr[q * tpq + c] = br[:, c * LANES:(c + 1) * LANES]
            bu_scr[ntile + q * tpq + c] = bi[:, c * LANES:(c + 1) * LANES]

    grp = 4
    for cb in range(ntile // grp):
        tiles = list(range(cb * grp, (cb + 1) * grp))
        a_re = [jnp.broadcast_to(are_ref[:, c * LANES:(c + 1) * LANES], (nb, LANES)) for c in tiles]
        a_im = [jnp.broadcast_to(aim_ref[:, c * LANES:(c + 1) * LANES], (nb, LANES)) for c in tiles]
        s_re = [st_scr[c] for c in tiles]
        s_im = [st_scr[ntile + c] for c in tiles]
        for t in range(tt):
            rows = slice(t * nb, (t + 1) * nb)
            for i, c in enumerate(tiles):
                n_re = a_re[i] * s_re[i] - a_im[i] * s_im[i] + bu_scr[c, rows, :]
                n_im = a_re[i] * s_im[i] + a_im[i] * s_re[i] + bu_scr[ntile + c, rows, :]
                bu_scr[c, rows, :] = n_re
                bu_scr[ntile + c, rows, :] = n_im
                s_re[i], s_im[i] = n_re, n_im
        for i, c in enumerate(tiles):
            st_scr[c] = s_re[i]
            st_scr[ntile + c] = s_im[i]

    ys = []
    for q in range(D_MODEL // LANES):
        sr = jnp.concatenate([bu_scr[q * tpq + c] for c in range(tpq)], axis=1).astype(BF16)
        si = jnp.concatenate([bu_scr[ntile + q * tpq + c] for c in range(tpq)], axis=1).astype(BF16)
        ys.append(jnp.dot(sr, wcr_ref[q], preferred_element_type=F32)
                  + jnp.dot(si, wci_ref[q], preferred_element_type=F32))
    yc = jnp.concatenate(ys, axis=1) + d_ref[...] * u
    z = jnp.dot(_gelu(yc).astype(BF16), wg_ref[...], preferred_element_type=F32)
    mix = z[:, 0:D_MODEL] * _sigmoid(z[:, D_MODEL:])
    x_scr[...] = x + mix
    for t in range(tt):
        o_ref[:, t, :] = x_scr[t * nb:(t + 1) * nb, :]
    last_tile, last_row = (l_real - 1) // tt, (l_real - 1) % tt

    @pl.when(ti == last_tile)
    def _():
        rows = slice(last_row * nb, (last_row + 1) * nb)
        for c in range(ntile):
            sor_ref[:, c * LANES:(c + 1) * LANES] = bu_scr[c, rows, :]
            soi_ref[:, c * LANES:(c + 1) * LANES] = bu_scr[ntile + c, rows, :]


def _s5(x, g, sp, s0r, s0i, l_real):
    b, lp, _ = x.shape
    assert b % S5_BATCH == 0
    tt = _largest_tile(lp, 32, SUBLANES)
    m = S5_BATCH * tt
    consts = [g, sp["wbr"], sp["wbi"], sp["wcr"], sp["wci"], sp["are"], sp["aim"], sp["d"], sp["wglu"]]
    st = pl.BlockSpec((S5_BATCH, S5_STATE), lambda i, j: (i, 0))
    return pl.pallas_call(
        functools.partial(_s5_kernel, tt=tt, l_real=l_real),
        grid=(b // S5_BATCH, lp // tt),
        in_specs=[pl.BlockSpec((S5_BATCH, tt, D_MODEL), lambda i, j: (i, j, 0))]
                 + [_const_spec(c.shape) for c in consts] + [st, st],
        out_specs=[pl.BlockSpec((S5_BATCH, tt, D_MODEL), lambda i, j: (i, j, 0)), st, st],
        out_shape=[jax.ShapeDtypeStruct((b, lp, D_MODEL), F32),
                   jax.ShapeDtypeStruct((b, S5_STATE), F32), jax.ShapeDtypeStruct((b, S5_STATE), F32)],
        scratch_shapes=[pltpu.VMEM((2 * S5_STATE // LANES, m, LANES), F32),
                        pltpu.VMEM((2 * S5_STATE // LANES, S5_BATCH, LANES), F32),
                        pltpu.VMEM((m, D_MODEL), F32)],
        compiler_params=_cparams("parallel", "arbitrary"),
        name="s5",
    )(x, *consts, s0r, s0i)


def _prep_params(p):
    j = 0
    head_of = jnp.arange(W_MIX) // HEAD_DIM
    e = (head_of[:, None] == head_of[None, :]).astype(BF16)
    w_in = p["w_in_e"][j]
    wf = jnp.zeros((D_MODEL, LANES), F32).at[:, :N_HEADS].set(w_in[:, A_COLS + 3 * W_MIX:])
    wb = jnp.concatenate([w_in[:, A_COLS:A_COLS + 3 * W_MIX], wf], axis=1).astype(BF16)
    fb = jnp.zeros((1, LANES), F32).at[0, :N_HEADS].set(p["b_fbias"][j])
    w2a = jnp.zeros((R_W + R_A, 2 * W_MIX), F32)
    w2a = w2a.at[:R_W, :W_MIX].set(p["a_w2"][j]).at[R_W:, W_MIX:].set(p["a_a2"][j])
    row = lambda a: a.reshape(1, -1).astype(F32)
    rw = dict(mu=row(p["a_mu"][j]), w0=row(p["a_w0"][j]), a0=row(p["a_a0"][j]), w2a=w2a.astype(BF16),
              g2=p["a_g2"][j].astype(BF16), kk=row(p["a_kk"][j]), ka=row(p["a_ka"][j]),
              rk=row(p["a_rk"][j]), lnw=row(p["a_ln_w"][j]), lnb=row(p["a_ln_b"][j]), e=e)
    hm = (jnp.arange(N_HEADS)[:, None] == head_of[None, :]).astype(BF16)
    l0 = dict(wa=w_in[:, :A_COLS].astype(BF16), wb=wb, fb=fb, e=e, hm=hm,
              qg=row(jnp.tile(p["b_qnorm"][j], N_HEADS)), kg=row(jnp.tile(p["b_knorm"][j], N_HEADS)),
              w_out=p["w_out_e"][j].astype(BF16))
    lam_re, lam_im = p["c_lam_re"][j].astype(F32), p["c_lam_im"][j].astype(F32)
    dt = jnp.exp(p["c_log_dt"][j].astype(F32))[:, None]
    mag = jnp.exp(lam_re * dt)
    ab_re, ab_im = mag * jnp.cos(lam_im * dt), mag * jnp.sin(lam_im * dt)
    den = lam_re * lam_re + lam_im * lam_im
    f_re = ((ab_re - 1.0) * lam_re + ab_im * lam_im) / den
    f_im = (ab_im * lam_re - (ab_re - 1.0) * lam_im) / den
    b_re, b_im = p["c_b_re"][j], p["c_b_im"][j]
    bb_re = f_re[..., None] * b_re - f_im[..., None] * b_im
    bb_im = f_re[..., None] * b_im + f_im[..., None] * b_re
    gpt = LANES // GC
    eye = jnp.eye(gpt, dtype=F32)

    def in_layout(bb):
        t = bb.reshape(G_C // gpt, gpt, P_C, GC).transpose(0, 1, 3, 2)
        return jnp.einsum("qgcp,gh->qgchp", t, eye).reshape(G_C // gpt, gpt * GC, gpt * P_C).astype(BF16)

    def out_layout(cc):
        t = cc.reshape(G_C // gpt, gpt, GC, P_C).transpose(0, 1, 3, 2)
        return jnp.einsum("qgpc,gh->qgphc", t, eye).reshape(G_C // gpt, gpt * P_C, gpt * GC).astype(BF16)

    s5 = dict(wbr=in_layout(bb_re), wbi=in_layout(bb_im), wcr=out_layout(p["c_c_re"][j]),
              wci=out_layout(-p["c_c_im"][j]), are=ab_re.reshape(1, S5_STATE), aim=ab_im.reshape(1, S5_STATE),
              d=row(p["c_d"][j]), wglu=p["w_glu"][j].astype(BF16))
    ffn = [dict(g=row(p["norm_ffn"][i]), wu=p["w_up"][i].astype(BF16), cw=p["conv_w"][i].astype(F32),
                cb=row(p["conv_b"][i]), wd=p["w_down"][i].astype(BF16)) for i in range(2)]
    return dict(l0=l0, rw=rw, s5=s5, ffn=ffn, g_mix=[row(p["norm_mix"][i]) for i in range(2)])


def _trunk(x, l_real, init, pp, past, y_trim=None):
    b, lp, _ = x.shape
    l0 = pp["l0"]
    flat = lp == l_real and lp < INPROJ_TILE
    xin = x.reshape(1, b * lp, D_MODEL) if flat else x
    outs = _inproj(xin, xin.shape[1] if flat else l_real, pp["g_mix"][0], l0["wa"], l0["wb"], l0["e"], l0["qg"],
                   l0["kg"], l0["fb"])
    ua, q, kf, kb, vf, vb, lf = [a.reshape((b, lp) + a.shape[2:]) for a in outs] if flat else outs
    y_a, a_shift, a_wkv = _rwkv(ua, init["a_shift"].reshape(b, 1, A_COLS), init["a_wkv"], pp["rw"], l_real)
    lf_row = jnp.transpose(lf[:, :, :N_HEADS], (0, 2, 1)).reshape(b * N_HEADS, lp)
    if past is None:
        lpad = -lp % LANES
        c_row = _cumsum_lanes(jnp.pad(lf_row, ((0, 0), (0, lpad))))[:, :lp].reshape(b, N_HEADS, lp)
        cq = jnp.pad(jnp.transpose(c_row, (0, 2, 1)), ((0, 0), (0, 0), (0, LANES - N_HEADS)))
        y_b = _attn_prompt(q, kb, vb, cq, c_row, min(ATTN_TILE, lp))
    else:
        pk, pv, plf = past
        plen = pk.shape[1]
        plf_row = jnp.transpose(plf, (0, 2, 1)).reshape(b * N_HEADS, plen)
        c_all = _cumsum_lanes(jnp.concatenate([plf_row, jnp.pad(lf_row, ((0, 0), (0, -lp % LANES)))], axis=1))
        c_past = c_all[:, :plen].reshape(b, N_HEADS, plen)
        c_new = c_all[:, plen:plen + lp].reshape(b, N_HEADS, lp)
        cq_stack = jnp.broadcast_to(c_new.reshape(b, N_HEADS * lp, 1), (b, N_HEADS * lp, LANES))
        y_b = _attn_sample(q, cq_stack, pk.reshape(b, plen, W_MIX), pv.reshape(b, plen, W_MIX), c_past,
                           kb, vb, c_new, l0["hm"])
    f0 = pp["ffn"][0]
    x2, conv0 = _ffn(x, f0["g"], f0["wu"], f0["cw"], f0["cb"], f0["wd"], init["ffn_conv"][0], l_real,
                     mixer=(y_a, y_b, l0["w_out"]))
    x3, c_re, c_im = _s5(x2, pp["g_mix"][1], pp["s5"], init["c_re"].reshape(b, S5_STATE),
                         init["c_im"].reshape(b, S5_STATE), l_real)
    f1 = pp["ffn"][1]
    x4, conv1 = _ffn(x3, f1["g"], f1["wu"], f1["cw"], f1["cb"], f1["wd"], init["ffn_conv"][1], l_real,
                     trim=y_trim)
    states = (a_shift.reshape(1, b, A_COLS), a_wkv[None], kf[None], vf[None],
              lf[None, :, :l_real, :N_HEADS], c_re.reshape(1, b, G_C, P_C), c_im.reshape(1, b, G_C, P_C),
              jnp.stack([conv0, conv1], axis=0))
    return x4, states


def kernel(x_prompt, x_sample, state_a_shift, state_a_wkv, cache_b_k, cache_b_v, cache_b_logf, state_c_re, state_c_im, state_ffn_conv, meta, norm_mix, norm_ffn, w_in_e, a_mu, a_w0, a_w2, a_a0, a_a2, a_g2, a_kk, a_ka, a_rk, a_ln_w, a_ln_b, b_fbias, b_qnorm, b_knorm, w_out_e, c_lam_re, c_lam_im, c_log_dt, c_b_re, c_b_im, c_c_re, c_c_im, c_d, w_glu, w_up, conv_w, conv_b, w_down):
    pp = _prep_params(dict(
        norm_mix=norm_mix, norm_ffn=norm_ffn, w_in_e=w_in_e, a_mu=a_mu, a_w0=a_w0, a_w2=a_w2, a_a0=a_a0,
        a_a2=a_a2, a_g2=a_g2, a_kk=a_kk, a_ka=a_ka, a_rk=a_rk, a_ln_w=a_ln_w, a_ln_b=a_ln_b,
        b_fbias=b_fbias, b_qnorm=b_qnorm, b_knorm=b_knorm, w_out_e=w_out_e, c_lam_re=c_lam_re,
        c_lam_im=c_lam_im, c_log_dt=c_log_dt, c_b_re=c_b_re, c_b_im=c_b_im, c_c_re=c_c_re, c_c_im=c_c_im,
        c_d=c_d, w_glu=w_glu, w_up=w_up, conv_w=conv_w, conv_b=conv_b, w_down=w_down))
    b, seq, _ = x_prompt.shape
    n_meta = meta.shape[0]
    lr = n_meta + seq
    lp = -(-lr // WKV_CHUNK) * WKV_CHUNK
    xp = jnp.concatenate([jnp.broadcast_to(meta.astype(F32)[None], (b, n_meta, D_MODEL)), x_prompt,
                          jnp.zeros((b, lp - lr, D_MODEL), F32)], axis=1)
    init_p = dict(a_shift=jnp.zeros((b, A_COLS), F32),
                  a_wkv=jnp.zeros((b, N_HEADS, HEAD_DIM, HEAD_DIM), F32),
                  c_re=jnp.zeros((b, S5_STATE), F32), c_im=jnp.zeros((b, S5_STATE), F32),
                  ffn_conv=jnp.zeros((2, b, CONV_W - 1, 2 * D_FF), F32))
    yp, st_p = _trunk(xp, lr, init_p, pp, None, y_trim=(n_meta, seq))
    bs, ls, _ = x_sample.shape
    init_s = dict(a_shift=state_a_shift[0], a_wkv=state_a_wkv[0], c_re=state_c_re[0], c_im=state_c_im[0],
                  ffn_conv=state_ffn_conv)
    past = (cache_b_k[0], cache_b_v[0], cache_b_logf[0])
    ys, st_s = _trunk(x_sample, ls, init_s, pp, past)
    return (yp, ys, *st_p, *st_s)
```

```python
import functools
import math

import jax
import jax.numpy as jnp
from jax import lax
from jax.experimental import pallas as pl
from jax.experimental.pallas import tpu as pltpu

F32, BF16 = jnp.float32, jnp.bfloat16

D_MODEL = 1024
N_HEADS = 8
HEAD_DIM = 64
W_MIX = N_HEADS * HEAD_DIM
R_W, R_A, R_G = 64, 64, 128
A_COLS = 3 * W_MIX + R_W + R_A + R_G
B_PAD_COLS = 3 * W_MIX + 128
D_FF = 2816
CONV_W = 3
G_C, GC, P_C = 64, 16, 64
S5_STATE = G_C * P_C
EPS = 1e-6
LN_EPS = 64e-5
NEG = -1e30
LOG2E = math.log2(math.e)

LANES = 128
SUBLANES = 8
VMEM_LIMIT = 56 * 1024 * 1024
WKV_CHUNK = 64
WKV_SUB = 16
WKV_GROUP = 4
S5_BATCH = 8
S5_TILE = 32
TIME_TILE = 704
FFN_TILE = 528
FFN_TILE_TRIM = 352
INPROJ_TILE = 352
ATTN_TILE = 1024


def _cparams(*sem):
    return pltpu.CompilerParams(dimension_semantics=sem, vmem_limit_bytes=VMEM_LIMIT)


def _const_spec(shape):
    nd = len(shape)
    return pl.BlockSpec(shape, lambda *_: (0,) * nd, pipeline_mode=pl.Buffered(1))


def _largest_tile(n, cap, mult):
    best = None
    for t in range(mult, min(n, cap) + 1, mult):
        if n % t == 0:
            best = t
    assert best is not None, (n, cap, mult)
    return best


def _mm(a, b):
    return jnp.dot(a.astype(BF16), b.astype(BF16), preferred_element_type=F32)


def _mm_nt(a, b):
    return lax.dot_general(a.astype(BF16), b.astype(BF16), (((1,), (1,)), ((), ())),
                           preferred_element_type=F32)


def _mm_tn(a, b):
    return lax.dot_general(a.astype(BF16), b.astype(BF16), (((0,), (0,)), ((), ())),
                           preferred_element_type=F32)


def _split3(x):
    hi = x.astype(BF16)
    r = x - hi.astype(F32)
    mid = r.astype(BF16)
    lo = (r - mid.astype(F32)).astype(BF16)
    return hi, mid, lo


def _mm_exact_rhs(x, m):
    hi, mid, lo = _split3(x)
    return (jnp.dot(hi, m, preferred_element_type=F32) + jnp.dot(mid, m, preferred_element_type=F32)
            + jnp.dot(lo, m, preferred_element_type=F32))


def _headsum(x, e):
    return jnp.dot(x.astype(BF16), e, preferred_element_type=F32)


def _rms(x, g):
    return x * lax.rsqrt(jnp.mean(x * x, axis=-1, keepdims=True) + EPS) * g


def _sigmoid(x):
    return 1.0 / (1.0 + jnp.exp(-x))


def _log_sigmoid(x):
    return jnp.minimum(x, 0.0) - jnp.log(1.0 + jnp.exp(-jnp.abs(x)))


def _gelu(x):
    return 0.5 * x * (1.0 + jnp.tanh(math.sqrt(2.0 / math.pi) * (x + 0.044715 * (x * x * x))))


def _inproj_kernel(x_ref, g_ref, wa_ref, wb_ref, e_ref, qg_ref, kg_ref, fb_ref,
                   ua_ref, q_ref, kf_ref, kb_ref, vf_ref, vb_ref, lf_ref):
    h = _rms(x_ref[0], g_ref[...]).astype(BF16)
    ua_ref[0] = jnp.dot(h, wa_ref[...], preferred_element_type=F32)
    ub = jnp.dot(h, wb_ref[...], preferred_element_type=F32)
    q = ub[:, 0:W_MIX]
    k = ub[:, W_MIX:2 * W_MIX]
    v = ub[:, 2 * W_MIX:3 * W_MIX]
    f = ub[:, 3 * W_MIX:]
    e = e_ref[...]
    qn = q * lax.rsqrt(_headsum(q * q, e) * (1.0 / HEAD_DIM) + EPS) * qg_ref[...]
    kn = k * lax.rsqrt(_headsum(k * k, e) * (1.0 / HEAD_DIM) + EPS) * kg_ref[...]
    tm = q.shape[0]
    q_ref[0] = (qn * (HEAD_DIM ** -0.5 * LOG2E)).astype(BF16)
    kf_ref[0] = kn.reshape(tm, N_HEADS, HEAD_DIM)
    kb_ref[0] = kn.astype(BF16)
    vf_ref[0] = v.reshape(tm, N_HEADS, HEAD_DIM)
    vb_ref[0] = v.astype(BF16)
    lf = _log_sigmoid(f + fb_ref[...])
    lane = lax.broadcasted_iota(jnp.int32, lf.shape, 1)
    lf_ref[0] = jnp.where(lane < N_HEADS, lf, 0.0)


def _inproj(x, l_real, g, wa, wb, e, qg, kg, fb):
    b, lp, _ = x.shape
    tm = _largest_tile(lp, INPROJ_TILE, SUBLANES)
    row = lambda c: pl.BlockSpec((1, tm, c), lambda i, j: (i, j, 0))
    cache = pl.BlockSpec((1, tm, N_HEADS, HEAD_DIM), lambda i, j: (i, j, 0, 0))
    rowshape = lambda c, dt: jax.ShapeDtypeStruct((b, lp, c), dt)
    cacheshape = jax.ShapeDtypeStruct((b, l_real, N_HEADS, HEAD_DIM), F32)
    return pl.pallas_call(
        _inproj_kernel,
        grid=(b, lp // tm),
        in_specs=[row(D_MODEL), _const_spec(g.shape), _const_spec(wa.shape), _const_spec(wb.shape),
                  _const_spec(e.shape), _const_spec(qg.shape), _const_spec(kg.shape), _const_spec(fb.shape)],
        out_specs=[row(A_COLS), row(W_MIX), cache, row(W_MIX), cache, row(W_MIX), row(LANES)],
        out_shape=[rowshape(A_COLS, F32), rowshape(W_MIX, BF16), cacheshape, rowshape(W_MIX, BF16),
                   cacheshape, rowshape(W_MIX, BF16), rowshape(LANES, F32)],
        compiler_params=_cparams("parallel", "parallel"),
        name="inproj",
    )(x, g, wa, wb, e, qg, kg, fb)


def _cumsum_kernel(x_ref, tri_ref, o_ref, *, nblk):
    tri = tri_ref[...]
    carry = jnp.zeros((x_ref.shape[0], 1), F32)
    for j in range(nblk):
        cs = _mm_exact_rhs(x_ref[:, j * LANES:(j + 1) * LANES], tri) + carry
        o_ref[:, j * LANES:(j + 1) * LANES] = cs * LOG2E
        carry = cs[:, LANES - 1:LANES]


def _cumsum_lanes(x):
    rows, n = x.shape
    assert n % LANES == 0
    tri = (jnp.arange(LANES)[:, None] <= jnp.arange(LANES)[None, :]).astype(BF16)
    return pl.pallas_call(
        functools.partial(_cumsum_kernel, nblk=n // LANES),
        grid=(1,),
        in_specs=[_const_spec(x.shape), _const_spec(tri.shape)],
        out_specs=_const_spec(x.shape),
        out_shape=jax.ShapeDtypeStruct(x.shape, F32),
        compiler_params=_cparams("arbitrary"),
        name="cumsum",
    )(x, tri)


def _attn_prompt_kernel(q_ref, k_ref, v_ref, cq_ref, cr_ref, o_ref, *, bounds):
    p = pl.program_id(1)
    lane = lax.broadcasted_iota(jnp.int32, (1, LANES), 1)
    head_lanes = (lane < HEAD_DIM, lane >= HEAD_DIM)
    for qi in range(len(bounds) - 1):
        r0, r1 = bounds[qi], bounds[qi + 1]
        q2 = q_ref[0, r0:r1, :]
        cqt = cq_ref[0, r0:r1, :]
        lane_q = lax.broadcasted_iota(jnp.int32, cqt.shape, 1)
        res = []
        for hh in range(2):
            h = 2 * p + hh
            qm = jnp.where(head_lanes[hh], q2, jnp.zeros_like(q2))
            cq = jnp.sum(jnp.where(lane_q == h, cqt, 0.0), axis=-1, keepdims=True)
            ck_all = cr_ref[0, pl.ds(h, 1), :]
            m = jnp.full((r1 - r0, 1), NEG, F32)
            l = jnp.zeros((r1 - r0, 1), F32)
            acc = jnp.zeros((r1 - r0, LANES), F32)

            def block(a0, a1, c0, c1, m, l, acc, diag):
                t = lax.dot_general(qm[a0:a1], k_ref[0, c0:c1, :], (((1,), (1,)), ((), ())),
                                    preferred_element_type=F32) - ck_all[:, c0:c1]
                if diag:
                    ri = lax.broadcasted_iota(jnp.int32, t.shape, 0) + (r0 + a0)
                    ci = lax.broadcasted_iota(jnp.int32, t.shape, 1) + c0
                    t = jnp.where(ri >= ci, t, NEG)
                m_new = jnp.maximum(m, cq[a0:a1] + jnp.max(t, axis=-1, keepdims=True))
                alpha = jnp.exp2(m - m_new)
                pe = jnp.exp2(t + (cq[a0:a1] - m_new))
                l = alpha * l + jnp.sum(pe, axis=-1, keepdims=True)
                acc = alpha * acc + jnp.dot(pe.astype(BF16), v_ref[0, c0:c1, :], preferred_element_type=F32)
                return m_new, l, acc

            for ki in range(qi):
                m, l, acc = block(0, r1 - r0, bounds[ki], bounds[ki + 1], m, l, acc, False)
            half = ((r1 - r0) // 2) // LANES * LANES
            if half == 0:
                m, l, acc = block(0, r1 - r0, r0, r1, m, l, acc, True)
            else:
                m, l, acc = block(0, r1 - r0, r0, r0 + half, m, l, acc, True)
                mb, lb, ab = block(half, r1 - r0, r0 + half, r1, m[half:], l[half:], acc[half:], True)
                m = jnp.concatenate([m[:half], mb], axis=0)
                l = jnp.concatenate([l[:half], lb], axis=0)
                acc = jnp.concatenate([acc[:half], ab], axis=0)
            res.append(acc / l)
        o_ref[0, r0:r1, :] = jnp.where(head_lanes[0], res[0], res[1]).astype(BF16)


def _attn_prompt(q, k, v, cq, cr, tile):
    b, lp, _ = q.shape
    bounds = tuple(range(0, lp, tile)) + (lp,)
    col = pl.BlockSpec((1, lp, LANES), lambda i, p: (i, 0, p))
    return pl.pallas_call(
        functools.partial(_attn_prompt_kernel, bounds=bounds),
        grid=(b, N_HEADS // 2),
        in_specs=[col, col, col,
                  pl.BlockSpec((1, lp, LANES), lambda i, p: (i, 0, 0)),
                  pl.BlockSpec((1, N_HEADS, lp), lambda i, p: (i, 0, 0))],
        out_specs=col,
        out_shape=jax.ShapeDtypeStruct((b, lp, W_MIX), BF16),
        compiler_params=_cparams("parallel", "arbitrary"),
        name="attn_prompt",
    )(q, k, v, cq, cr)


def _attn_sample_kernel(q_ref, cq_ref, kp_ref, vp_ref, cp_ref, kn_ref, vn_ref, cn_ref, hm_ref, o_ref,
                        qx_scr, m_scr, l_scr, acc_scr, *, nkp, ls):
    j = pl.program_id(1)
    rows = N_HEADS * ls

    @pl.when(j == 0)
    def _():
        q = q_ref[0]
        qx = jnp.broadcast_to(q[None], (N_HEADS, ls, W_MIX)) * hm_ref[...][:, None, :]
        qx_scr[...] = qx.reshape(rows, W_MIX)
        m_scr[...] = jnp.full(m_scr.shape, NEG, F32)
        l_scr[...] = jnp.zeros(l_scr.shape, F32)
        acc_scr[...] = jnp.zeros(acc_scr.shape, F32)

    def update(k, v, ck, causal):
        tk = k.shape[0]
        s = lax.dot_general(qx_scr[...], k, (((1,), (1,)), ((), ())), preferred_element_type=F32)
        ckx = jnp.broadcast_to(ck[:, None, :], (N_HEADS, ls, tk)).reshape(rows, tk)
        s = s + cq_ref[0][:, 0:1] - ckx
        if causal:
            ri = lax.broadcasted_iota(jnp.int32, (N_HEADS, ls, tk), 1).reshape(rows, tk)
            ci = lax.broadcasted_iota(jnp.int32, (rows, tk), 1)
            s = jnp.where(ri >= ci, s, NEG)
        m = m_scr[...]
        m_new = jnp.maximum(m, jnp.max(s, axis=-1, keepdims=True))
        alpha = jnp.exp2(m - m_new)
        pe = jnp.exp2(s - m_new)
        l_scr[...] = alpha * l_scr[...] + jnp.sum(pe, axis=-1, keepdims=True)
        acc_scr[...] = alpha * acc_scr[...] + jnp.dot(pe.astype(BF16), v, preferred_element_type=F32)
        m_scr[...] = m_new

    @pl.when(j < nkp)
    def _():
        update(kp_ref[0].astype(BF16), vp_ref[0].astype(BF16), cp_ref[0], False)

    @pl.when(j == nkp)
    def _():
        update(kn_ref[0], vn_ref[0], cn_ref[0], True)
        o = (acc_scr[...] / l_scr[...]).reshape(N_HEADS, ls, W_MIX) * hm_ref[...].astype(F32)[:, None, :]
        o_ref[0] = jnp.sum(o, axis=0).astype(BF16)


def _attn_sample(q, cq_stack, kp, vp, cp, kn, vn, cn, hm):
    b, ls, _ = q.shape
    past = kp.shape[1]
    tkp = _largest_tile(past, 1024, LANES)
    nkp = past // tkp
    rows = N_HEADS * ls
    pidx = lambda i, j: (i, jnp.minimum(j, nkp - 1), 0)
    return pl.pallas_call(
        functools.partial(_attn_sample_kernel, nkp=nkp, ls=ls),
        grid=(b, nkp + 1),
        in_specs=[pl.BlockSpec((1, ls, W_MIX), lambda i, j: (i, 0, 0)),
                  pl.BlockSpec((1, rows, LANES), lambda i, j: (i, 0, 0)),
                  pl.BlockSpec((1, tkp, W_MIX), pidx),
                  pl.BlockSpec((1, tkp, W_MIX), pidx),
                  pl.BlockSpec((1, N_HEADS, tkp), lambda i, j: (i, 0, jnp.minimum(j, nkp - 1))),
                  pl.BlockSpec((1, ls, W_MIX), lambda i, j: (i, 0, 0)),
                  pl.BlockSpec((1, ls, W_MIX), lambda i, j: (i, 0, 0)),
                  pl.BlockSpec((1, N_HEADS, ls), lambda i, j: (i, 0, 0)),
                  _const_spec(hm.shape)],
        out_specs=pl.BlockSpec((1, ls, W_MIX), lambda i, j: (i, 0, 0)),
        out_shape=jax.ShapeDtypeStruct((b, ls, W_MIX), BF16),
        scratch_shapes=[pltpu.VMEM((rows, W_MIX), BF16), pltpu.VMEM((rows, 1), F32),
                        pltpu.VMEM((rows, 1), F32), pltpu.VMEM((rows, W_MIX), F32)],
        compiler_params=_cparams("parallel", "arbitrary"),
        name="attn_sample",
    )(q, cq_stack, kp, vp, cp, kn, vn, cn, hm)


def _wkv_chunk_matrices(tops, bots, vs, t):
    hd = range(len(tops))
    ri = lax.broadcasted_iota(jnp.int32, (t, t), 0)
    ci = lax.broadcasted_iota(jnp.int32, (t, t), 1)
    eye = (ri == ci).astype(F32)
    r2 = lax.broadcasted_iota(jnp.int32, (t, 2 * t), 0)
    c2 = lax.broadcasted_iota(jnp.int32, (t, 2 * t), 1)
    c2 = jnp.where(c2 >= t, c2 - t, c2)
    same = (ri // WKV_SUB) == (ci // WKV_SUB)
    aa = [_mm_nt(tops[h], bots[h]) for h in hd]
    n = [jnp.where(ri > ci, aa[h][:t, :t], 0.0) for h in hd]
    a_ak = [jnp.where(ri > ci, aa[h][:t, t:], 0.0) for h in hd]
    a_r = [jnp.where(r2 >= c2, aa[h][t:, :], 0.0) for h in hd]
    av = [_mm(a_ak[h], vs[h]) for h in hd]
    d = [jnp.where(same, n[h], 0.0) for h in hd]
    lo = [n[h] - d[h] for h in hd]
    d2 = [_mm(d[h], d[h]) for h in hd]
    x = [_mm(eye - d[h], eye + d2[h]) for h in hd]
    d4 = [_mm(d2[h], d2[h]) for h in hd]
    x = [_mm(x[h], eye + d4[h]) for h in hd]
    d8 = [_mm(d4[h], d4[h]) for h in hd]
    x = [_mm(x[h], eye + d8[h]) for h in hd]
    mb = [_mm(x[h], lo[h]) for h in hd]
    mb2 = [_mm(mb[h], mb[h]) for h in hd]
    xx = [_mm(eye + mb2[h], x[h]) for h in hd]
    ginv = [_mm(eye - mb[h], xx[h]) for h in hd]
    return ginv, a_r, av


def _rwkv_kernel(u_ref, sh_ref, s0_ref, mu_ref, w0_ref, a0_ref, w2a_ref, g2_ref, kk_ref, ka_ref,
                 rk_ref, lnw_ref, lnb_ref, e_ref, tril_ref,
                 y_ref, sho_ref, so_ref,
                 ubuf, carry, s_scr, r_s, wl_s, k_s, v_s, kk_s, b_s, y_s, bon_s, g_s,
                 top_s, bk_s, vh_s, av_s, gi_s, ar_s, wt_s, *, tl, t, l_real, nt):
    ti = pl.program_id(1)

    @pl.when(ti == 0)
    def _():
        carry[...] = jnp.broadcast_to(sh_ref[0], carry.shape)
        s_scr[...] = s0_ref[0]

    u = u_ref[0]
    ubuf[0:SUBLANES, :] = carry[...]
    ubuf[SUBLANES:SUBLANES + tl, :] = u
    u_prev = ubuf[SUBLANES - 1:SUBLANES - 1 + tl, :]
    carry[...] = u[tl - SUBLANES:tl, :]
    um = u + (u_prev - u) * mu_ref[...]
    r = um[:, 0:W_MIX]
    k = um[:, W_MIX:2 * W_MIX]
    v = um[:, 2 * W_MIX:3 * W_MIX]
    wa = um[:, 3 * W_MIX:3 * W_MIX + R_W + R_A]
    gd = um[:, 3 * W_MIX + R_W + R_A:]
    lane = lax.broadcasted_iota(jnp.int32, wa.shape, 1)
    za = _mm(jnp.where(lane < R_W, jnp.tanh(wa), wa), w2a_ref[...])
    wl = (-math.exp(-0.5)) * _sigmoid(w0_ref[...] + za[:, 0:W_MIX])
    a = _sigmoid(a0_ref[...] + za[:, W_MIX:])
    g_s[...] = _mm(_sigmoid(gd), g2_ref[...])
    e = e_ref[...]
    kk = k * kk_ref[...]
    kkn = kk * lax.rsqrt(_headsum(kk * kk, e) + 1e-12)
    k2 = k * (1.0 + (a - 1.0) * ka_ref[...])
    bon_s[...] = _headsum(r * k2 * rk_ref[...], e) * v
    valid = (ti * tl + lax.broadcasted_iota(jnp.int32, (tl, 1), 0)) < l_real
    r_s[...] = r
    wl_s[...] = jnp.where(valid, wl, 0.0)
    k_s[...] = jnp.where(valid, k2, 0.0)
    v_s[...] = jnp.where(valid, v, 0.0)
    kk_s[...] = jnp.where(valid, kkn, 0.0)
    b_s[...] = jnp.where(valid, kkn * a, 0.0)

    tril = tril_ref[...]

    hd = range(N_HEADS)
    hsl = [slice(h * HEAD_DIM, (h + 1) * HEAD_DIM) for h in hd]

    nc = tl // t
    grp = min(WKV_GROUP, nc)

    def chunk_matrices(cs):
        tops, bots, vs, bks = [], [], [], []
        for c in cs:
            rows = slice(c * t, (c + 1) * t)
            wlc = wl_s[rows, :]
            hi, mid, lo = _split3(wlc)
            cw = (jnp.dot(tril, hi, preferred_element_type=F32) + jnp.dot(tril, mid, preferred_element_type=F32)
                  + jnp.dot(tril, lo, preferred_element_type=F32))
            w_inc = jnp.exp(cw)
            w_inv = jnp.exp(-cw)
            w_prev = jnp.exp(cw - wlc)
            w_t = w_inc[t - 1:t, :]
            top = jnp.concatenate([kk_s[rows, :] * w_prev, r_s[rows, :] * w_inc], axis=0).astype(BF16)
            bot = jnp.concatenate([b_s[rows, :] * w_inv, k_s[rows, :] * w_inv], axis=0)
            bk = (bot * w_t).astype(BF16)
            bot = bot.astype(BF16)
            vc = v_s[rows, :].astype(BF16)
            tops += [top[:, hsl[h]] for h in hd]
            bots += [bot[:, hsl[h]] for h in hd]
            vs += [vc[:, hsl[h]] for h in hd]
            bks += [bk[:, hsl[h]] for h in hd]
            wt_s[c] = jnp.broadcast_to(w_t, (SUBLANES, W_MIX))
        ginv, a_r, av = _wkv_chunk_matrices(tops, bots, vs, t)
        for j, c in enumerate(cs):
            for h in hd:
                k = j * N_HEADS + h
                top_s[c, h] = tops[k]
                bk_s[c, h] = bks[k]
                vh_s[c, h] = vs[k]
                av_s[c, h] = av[k]
                gi_s[c, h] = ginv[k].astype(BF16)
                ar_s[c, h] = a_r[k].astype(BF16)

    def chunk_state(c):
        rows = slice(c * t, (c + 1) * t)
        pp = [_mm_nt(top_s[c, h], s_scr[h]) for h in hd]
        u = [_mm(gi_s[c, h], -(pp[h][:t] + av_s[c, h])) for h in hd]
        uv = [jnp.concatenate([u[h].astype(BF16), vh_s[c, h]], axis=0) for h in hd]
        y = [pp[h][t:] + _mm(ar_s[c, h], uv[h]) for h in hd]
        w_t = wt_s[c]
        for h in hd:
            s_scr[h] = s_scr[h] * w_t[0:1, hsl[h]] + _mm_tn(uv[h], bk_s[c, h])
        y_s[rows, :] = jnp.concatenate(y, axis=1)

    groups = [list(range(g, min(g + grp, nc))) for g in range(0, nc, grp)]
    chunk_matrices(groups[0])
    for gi in range(1, len(groups)):
        for c in groups[gi - 1]:
            chunk_state(c)
        chunk_matrices(groups[gi])
    for c in groups[-1]:
        chunk_state(c)

    y = y_s[...]
    yc = y - _headsum(y, e) * (1.0 / HEAD_DIM)
    var = _headsum(yc * yc, e) * (1.0 / HEAD_DIM)
    yn = yc * lax.rsqrt(var + LN_EPS)
    y_ref[0] = ((yn * lnw_ref[...] + lnb_ref[...] + bon_s[...]) * g_s[...]).astype(BF16)

    last_tile, last_row = (l_real - 1) // tl, (l_real - 1) % tl

    @pl.when(ti == last_tile)
    def _():
        sho_ref[0] = ubuf[SUBLANES + last_row:SUBLANES + last_row + 1, :]

    @pl.when(ti == nt - 1)
    def _():
        so_ref[0] = s_scr[...]


def _rwkv(ua, shift0, wkv0, prm, l_real):
    b, lp, _ = ua.shape
    t = min(WKV_CHUNK, lp)
    tl = _largest_tile(lp, TIME_TILE, t)
    nt = lp // tl
    tril = (jnp.arange(t)[:, None] >= jnp.arange(t)[None, :]).astype(BF16)
    consts = [prm["mu"], prm["w0"], prm["a0"], prm["w2a"], prm["g2"], prm["kk"], prm["ka"], prm["rk"],
              prm["lnw"], prm["lnb"], prm["e"], tril]
    big = lambda: pltpu.VMEM((tl, W_MIX), F32)
    nc = tl // t
    per_head = lambda r, c, dt: pltpu.VMEM((nc, N_HEADS, r, c), dt)
    chunk_scratch = [per_head(2 * t, HEAD_DIM, BF16), per_head(2 * t, HEAD_DIM, BF16),
                     per_head(t, HEAD_DIM, BF16), per_head(t, HEAD_DIM, F32), per_head(t, t, BF16),
                     per_head(t, 2 * t, BF16), pltpu.VMEM((nc, SUBLANES, W_MIX), F32)]
    return pl.pallas_call(
        functools.partial(_rwkv_kernel, tl=tl, t=t, l_real=l_real, nt=nt),
        grid=(b, nt),
        in_specs=[pl.BlockSpec((1, tl, A_COLS), lambda i, j: (i, j, 0)),
                  pl.BlockSpec((1, 1, A_COLS), lambda i, j: (i, 0, 0)),
                  pl.BlockSpec((1, N_HEADS, HEAD_DIM, HEAD_DIM), lambda i, j: (i, 0, 0, 0))]
                 + [_const_spec(c.shape) for c in consts],
        out_specs=[pl.BlockSpec((1, tl, W_MIX), lambda i, j: (i, j, 0)),
                   pl.BlockSpec((1, 1, A_COLS), lambda i, j: (i, 0, 0)),
                   pl.BlockSpec((1, N_HEADS, HEAD_DIM, HEAD_DIM), lambda i, j: (i, 0, 0, 0))],
        out_shape=[jax.ShapeDtypeStruct((b, lp, W_MIX), BF16),
                   jax.ShapeDtypeStruct((b, 1, A_COLS), F32),
                   jax.ShapeDtypeStruct((b, N_HEADS, HEAD_DIM, HEAD_DIM), F32)],
        scratch_shapes=[pltpu.VMEM((tl + SUBLANES, A_COLS), F32), pltpu.VMEM((SUBLANES, A_COLS), F32),
                        pltpu.VMEM((N_HEADS, HEAD_DIM, HEAD_DIM), F32)] + [big() for _ in range(9)]
                       + chunk_scratch,
        compiler_params=_cparams("parallel", "arbitrary"),
        name="rwkv",
    )(ua, shift0, wkv0, *consts)


def _ffn_kernel(*refs, tl, tf, dgrp, l_real, mixer_out, trim):
    if mixer_out:
        x_ref, ya_ref, yb_ref, wo_ref = refs[:4]
        refs = refs[4:]
        y_ab = jnp.concatenate([ya_ref[0], yb_ref[0]], axis=1)
        x = x_ref[0] + jnp.dot(y_ab, wo_ref[...], preferred_element_type=F32)
    else:
        x = refs[0][0]
        refs = refs[1:]
    g_ref, wu_ref, cw_ref, cb_ref, wd_ref, buf_ref, o_ref, st_ref, up_scr, act_scr, carry_scr = refs
    ti = pl.program_id(1)
    h = _rms(x, g_ref[...]).astype(BF16)
    last_tile, last_row = (l_real - 1) // tl, (l_real - 1) % tl

    @pl.when(ti == 0)
    def _():
        carry_scr[SUBLANES - (CONV_W - 1):SUBLANES, :] = buf_ref[0]

    nf = D_FF // tf
    col = lambda part, f: slice(part * D_FF + f * tf, part * D_FF + (f + 1) * tf)

    def up(f):
        for part in range(2):
            cols = col(part, f)
            up_scr[0:SUBLANES, cols] = carry_scr[:, cols]
            up_scr[SUBLANES:SUBLANES + tl, cols] = jnp.dot(h, wu_ref[:, cols], preferred_element_type=F32)
            carry_scr[:, cols] = up_scr[tl:tl + SUBLANES, cols]

    ahead = 2
    for f in range(min(ahead, nf)):
        up(f)
    out = x
    g0 = 0
    for f in range(nf):
        z = [cb_ref[:, col(part, f)]
             + cw_ref[0:1, col(part, f)] * up_scr[SUBLANES - 2:SUBLANES - 2 + tl, col(part, f)]
             + cw_ref[1:2, col(part, f)] * up_scr[SUBLANES - 1:SUBLANES - 1 + tl, col(part, f)]
             + cw_ref[2:3, col(part, f)] * up_scr[SUBLANES:SUBLANES + tl, col(part, f)]
             for part in range(2)]
        val, gate = z
        act_scr[:, f * tf:(f + 1) * tf] = (gate * _sigmoid(gate) * val).astype(BF16)
        if (f + 1) % dgrp == 0 or f + 1 == nf:
            g1 = (f + 1) * tf
            out = out + jnp.dot(act_scr[:, g0:g1], wd_ref[g0:g1, :], preferred_element_type=F32)
            g0 = g1
        if f + ahead < nf:
            up(f + ahead)
    if trim is None:
        o_ref[0] = out
    else:
        row0, nrows = trim
        end_tile, end_n = (row0 + nrows - 1) // tl, (row0 + nrows - 1) % tl + 1
        if end_tile == 0:
            o_ref[0] = out[row0:row0 + nrows]
        else:
            @pl.when(ti == 0)
            def _():
                o_ref[0, 0:tl - row0, :] = out[row0:tl]

            @pl.when((ti > 0) & (ti < end_tile))
            def _():
                o_ref[0, pl.ds(pl.multiple_of(ti * tl - row0, SUBLANES), tl), :] = out

            @pl.when(ti == end_tile)
            def _():
                o_ref[0, end_tile * tl - row0:end_tile * tl - row0 + end_n, :] = out[0:end_n]

    @pl.when(ti == last_tile)
    def _():
        st_ref[0] = up_scr[SUBLANES + last_row - 1:SUBLANES + last_row + 1, :]


def _ffn_streams_kernel(*refs, nb, tl, tf, mixer_out):
    if mixer_out:
        x_ref, ya_ref, yb_ref, wo_ref = refs[:4]
        refs = refs[4:]
    else:
        x_ref = refs[0]
        refs = refs[1:]
    g_ref, wu_ref, cw_ref, cb_ref, wd_ref, buf_ref, o_ref, st_ref, x3_scr, x_scr, up_scr, act_scr = refs
    m = nb * tl
    x = x_ref[...].reshape(m, D_MODEL)
    if mixer_out:
        y_ab = jnp.concatenate([ya_ref[...].reshape(m, W_MIX), yb_ref[...].reshape(m, W_MIX)], axis=1)
        x = x + jnp.dot(y_ab, wo_ref[...], preferred_element_type=F32)
    x3_scr[...] = x.reshape(nb, tl, D_MODEL)
    for t in range(tl):
        x_scr[t * nb:(t + 1) * nb, :] = x3_scr[:, t, :]
    x = x_scr[...]
    h = _rms(x, g_ref[...]).astype(BF16)
    for i in range(CONV_W - 1):
        up_scr[i * nb:(i + 1) * nb, :] = buf_ref[:, i, :]
    pre = (CONV_W - 1) * nb
    for c in range(2 * D_FF // tf):
        up_scr[pre:pre + m, c * tf:(c + 1) * tf] = jnp.dot(h, wu_ref[:, c * tf:(c + 1) * tf],
                                                            preferred_element_type=F32)
    for i in range(CONV_W - 1):
        st_ref[:, i, :] = up_scr[m + i * nb:m + (i + 1) * nb, :]
    for f in range(D_FF // tf):
        z = []
        for part in range(2):
            cols = slice(part * D_FF + f * tf, part * D_FF + (f + 1) * tf)
            z.append(cb_ref[:, cols] + cw_ref[0:1, cols] * up_scr[0:m, cols]
                     + cw_ref[1:2, cols] * up_scr[nb:nb + m, cols] + cw_ref[2:3, cols] * up_scr[pre:pre + m, cols])
        val, gate = z
        act_scr[:, f * tf:(f + 1) * tf] = (gate * _sigmoid(gate) * val).astype(BF16)
    x_scr[...] = x + jnp.dot(act_scr[...], wd_ref[...], preferred_element_type=F32)
    for t in range(tl):
        o_ref[:, t, :] = x_scr[t * nb:(t + 1) * nb, :]


def _ffn_streams(x, g, wu, cw, cb, wd, buf, mixer):
    b, tl, _ = x.shape
    nb, tf = S5_BATCH, 256
    m = nb * tl
    blk = lambda r, c: pl.BlockSpec((nb, r, c), lambda i: (i, 0, 0))
    mix_args, mix_specs = [], []
    if mixer is not None:
        mix_args = list(mixer)
        mix_specs = [blk(tl, W_MIX), blk(tl, W_MIX), _const_spec(mixer[2].shape)]
    return pl.pallas_call(
        functools.partial(_ffn_streams_kernel, nb=nb, tl=tl, tf=tf, mixer_out=mixer is not None),
        grid=(b // nb,),
        in_specs=[blk(tl, D_MODEL)] + mix_specs
                 + [_const_spec(g.shape), _const_spec(wu.shape), _const_spec(cw.shape), _const_spec(cb.shape),
                    _const_spec(wd.shape), blk(CONV_W - 1, 2 * D_FF)],
        out_specs=[blk(tl, D_MODEL), blk(CONV_W - 1, 2 * D_FF)],
        out_shape=[jax.ShapeDtypeStruct((b, tl, D_MODEL), F32),
                   jax.ShapeDtypeStruct((b, CONV_W - 1, 2 * D_FF), F32)],
        scratch_shapes=[pltpu.VMEM((nb, tl, D_MODEL), F32), pltpu.VMEM((m, D_MODEL), F32),
                        pltpu.VMEM((m + (CONV_W - 1) * nb, 2 * D_FF), F32), pltpu.VMEM((m, D_FF), BF16)],
        compiler_params=_cparams("parallel"),
        name="ffn_streams",
    )(x, *mix_args, g, wu, cw, cb, wd, buf)


def _ffn(x, g, wu, cw, cb, wd, buf, l_real, mixer=None, trim=None):
    b, lp, _ = x.shape
    if lp == l_real and lp <= WKV_CHUNK and lp % SUBLANES == 0 and b % S5_BATCH == 0 and trim is None:
        return _ffn_streams(x, g, wu, cw, cb, wd, buf, mixer)
    tl = _largest_tile(lp, FFN_TILE if trim is None else FFN_TILE_TRIM, SUBLANES)
    tf, dgrp = 256, 4
    assert l_real >= CONV_W - 1 and (l_real - 1) % tl >= 1
    row = lambda c: pl.BlockSpec((1, tl, c), lambda i, j: (i, j, 0))
    mix_args, mix_specs = [], []
    if mixer is not None:
        mix_args = list(mixer)
        mix_specs = [row(W_MIX), row(W_MIX), _const_spec(mixer[2].shape)]
    y_spec, y_rows = row(D_MODEL), lp
    if trim is not None:
        assert trim[0] % SUBLANES == 0 and trim[0] < tl and trim[0] + trim[1] <= l_real
        y_spec, y_rows = pl.BlockSpec((1, trim[1], D_MODEL), lambda i, j: (i, 0, 0)), trim[1]
    return pl.pallas_call(
        functools.partial(_ffn_kernel, tl=tl, tf=tf, dgrp=dgrp, l_real=l_real, mixer_out=mixer is not None,
                          trim=trim),
        grid=(b, lp // tl),
        in_specs=[row(D_MODEL)] + mix_specs
                 + [_const_spec(g.shape), _const_spec(wu.shape), _const_spec(cw.shape), _const_spec(cb.shape),
                    _const_spec(wd.shape),
                    pl.BlockSpec((1, CONV_W - 1, 2 * D_FF), lambda i, j: (i, 0, 0))],
        out_specs=[y_spec, pl.BlockSpec((1, CONV_W - 1, 2 * D_FF), lambda i, j: (i, 0, 0))],
        out_shape=[jax.ShapeDtypeStruct((b, y_rows, D_MODEL), F32),
                   jax.ShapeDtypeStruct((b, CONV_W - 1, 2 * D_FF), F32)],
        scratch_shapes=[pltpu.VMEM((tl + SUBLANES, 2 * D_FF), F32), pltpu.VMEM((tl, D_FF), BF16),
                        pltpu.VMEM((SUBLANES, 2 * D_FF), F32)],
        compiler_params=_cparams("parallel", "arbitrary"),
        name="ffn",
    )(x, *mix_args, g, wu, cw, cb, wd, buf)


def _s5_kernel(x_ref, g_ref, wbr_ref, wbi_ref, wcr_ref, wci_ref, are_ref, aim_ref, d_ref, wg_ref,
               s0r_ref, s0i_ref, o_ref, sor_ref, soi_ref, bu_scr, st_scr, x_scr, *, tt, l_real):
    ti = pl.program_id(1)
    nb = S5_BATCH
    m = nb * tt
    ntile = S5_STATE // LANES

    @pl.when(ti == 0)
    def _():
        for c in range(ntile):
            st_scr[c] = s0r_ref[:, c * LANES:(c + 1) * LANES]
            st_scr[ntile + c] = s0i_ref[:, c * LANES:(c + 1) * LANES]

    for t in range(tt):
        x_scr[t * nb:(t + 1) * nb, :] = x_ref[:, t, :]
    x = x_scr[...]
    u = _rms(x, g_ref[...])
    ub = u.astype(BF16)
    tpq = LANES // GC * P_C // LANES
    for q in range(D_MODEL // LANES):
        uq = ub[:, q * LANES:(q + 1) * LANES]
        br = jnp.dot(uq, wbr_ref[q], preferred_element_type=F32)
        bi = jnp.dot(uq, wbi_ref[q], preferred_element_type=F32)
        for c in range(tpq):
            bu_scr[q * tpq + c] = br[:, c * LANES:(c + 1) * LANES]
            bu_scr[ntile + q * tpq + c] = bi[:, c * LANES:(c + 1) * LANES]

    grp = 4
    for cb in range(ntile // grp):
        tiles = list(range(cb * grp, (cb + 1) * grp))
        a_re = [jnp.broadcast_to(are_ref[:, c * LANES:(c + 1) * LANES], (nb, LANES)) for c in tiles]
        a_im = [jnp.broadcast_to(aim_ref[:, c * LANES:(c + 1) * LANES], (nb, LANES)) for c in tiles]
        s_re = [st_scr[c] for c in tiles]
        s_im = [st_scr[ntile + c] for c in tiles]
        for t in range(tt):
            rows = slice(t * nb, (t + 1) * nb)
            for i, c in enumerate(tiles):
                n_re = a_re[i] * s_re[i] - a_im[i] * s_im[i] + bu_scr[c, rows, :]
                n_im = a_re[i] * s_im[i] + a_im[i] * s_re[i] + bu_scr[ntile + c, rows, :]
                bu_scr[c, rows, :] = n_re
                bu_scr[ntile + c, rows, :] = n_im
                s_re[i], s_im[i] = n_re, n_im
        for i, c in enumerate(tiles):
            st_scr[c] = s_re[i]
            st_scr[ntile + c] = s_im[i]

    ys = []
    for q in range(D_MODEL // LANES):
        sr = jnp.concatenate([bu_scr[q * tpq + c] for c in range(tpq)], axis=1).astype(BF16)
        si = jnp.concatenate([bu_scr[ntile + q * tpq + c] for c in range(tpq)], axis=1).astype(BF16)
        ys.append(jnp.dot(sr, wcr_ref[q], preferred_element_type=F32)
                  + jnp.dot(si, wci_ref[q], preferred_element_type=F32))
    yc = jnp.concatenate(ys, axis=1) + d_ref[...] * u
    z = jnp.dot(_gelu(yc).astype(BF16), wg_ref[...], preferred_element_type=F32)
    mix = z[:, 0:D_MODEL] * _sigmoid(z[:, D_MODEL:])
    x_scr[...] = x + mix
    for t in range(tt):
        o_ref[:, t, :] = x_scr[t * nb:(t + 1) * nb, :]
    last_tile, last_row = (l_real - 1) // tt, (l_real - 1) % tt

    @pl.when(ti == last_tile)
    def _():
        rows = slice(last_row * nb, (last_row + 1) * nb)
        for c in range(ntile):
            sor_ref[:, c * LANES:(c + 1) * LANES] = bu_scr[c, rows, :]
            soi_ref[:, c * LANES:(c + 1) * LANES] = bu_scr[ntile + c, rows, :]


def _s5(x, g, sp, s0r, s0i, l_real):
    b, lp, _ = x.shape
    assert b % S5_BATCH == 0
    tt = _largest_tile(lp, S5_TILE, SUBLANES)
    m = S5_BATCH * tt
    consts = [g, sp["wbr"], sp["wbi"], sp["wcr"], sp["wci"], sp["are"], sp["aim"], sp["d"], sp["wglu"]]
    st = pl.BlockSpec((S5_BATCH, S5_STATE), lambda i, j: (i, 0))
    return pl.pallas_call(
        functools.partial(_s5_kernel, tt=tt, l_real=l_real),
        grid=(b // S5_BATCH, lp // tt),
        in_specs=[pl.BlockSpec((S5_BATCH, tt, D_MODEL), lambda i, j: (i, j, 0))]
                 + [_const_spec(c.shape) for c in consts] + [st, st],
        out_specs=[pl.BlockSpec((S5_BATCH, tt, D_MODEL), lambda i, j: (i, j, 0)), st, st],
        out_shape=[jax.ShapeDtypeStruct((b, lp, D_MODEL), F32),
                   jax.ShapeDtypeStruct((b, S5_STATE), F32), jax.ShapeDtypeStruct((b, S5_STATE), F32)],
        scratch_shapes=[pltpu.VMEM((2 * S5_STATE // LANES, m, LANES), F32),
                        pltpu.VMEM((2 * S5_STATE // LANES, S5_BATCH, LANES), F32),
                        pltpu.VMEM((m, D_MODEL), F32)],
        compiler_params=_cparams("parallel", "arbitrary"),
        name="s5",
    )(x, *consts, s0r, s0i)


def _prep_params(p):
    j = 0
    head_of = jnp.arange(W_MIX) // HEAD_DIM
    e = (head_of[:, None] == head_of[None, :]).astype(BF16)
    w_in = p["w_in_e"][j]
    wf = jnp.zeros((D_MODEL, LANES), F32).at[:, :N_HEADS].set(w_in[:, A_COLS + 3 * W_MIX:])
    wb = jnp.concatenate([w_in[:, A_COLS:A_COLS + 3 * W_MIX], wf], axis=1).astype(BF16)
    fb = jnp.zeros((1, LANES), F32).at[0, :N_HEADS].set(p["b_fbias"][j])
    w2a = jnp.zeros((R_W + R_A, 2 * W_MIX), F32)
    w2a = w2a.at[:R_W, :W_MIX].set(p["a_w2"][j]).at[R_W:, W_MIX:].set(p["a_a2"][j])
    row = lambda a: a.reshape(1, -1).astype(F32)
    rw = dict(mu=row(p["a_mu"][j]), w0=row(p["a_w0"][j]), a0=row(p["a_a0"][j]), w2a=w2a.astype(BF16),
              g2=p["a_g2"][j].astype(BF16), kk=row(p["a_kk"][j]), ka=row(p["a_ka"][j]),
              rk=row(p["a_rk"][j]), lnw=row(p["a_ln_w"][j]), lnb=row(p["a_ln_b"][j]), e=e)
    hm = (jnp.arange(N_HEADS)[:, None] == head_of[None, :]).astype(BF16)
    l0 = dict(wa=w_in[:, :A_COLS].astype(BF16), wb=wb, fb=fb, e=e, hm=hm,
              qg=row(jnp.tile(p["b_qnorm"][j], N_HEADS)), kg=row(jnp.tile(p["b_knorm"][j], N_HEADS)),
              w_out=p["w_out_e"][j].astype(BF16))
    lam_re, lam_im = p["c_lam_re"][j].astype(F32), p["c_lam_im"][j].astype(F32)
    dt = jnp.exp(p["c_log_dt"][j].astype(F32))[:, None]
    mag = jnp.exp(lam_re * dt)
    ab_re, ab_im = mag * jnp.cos(lam_im * dt), mag * jnp.sin(lam_im * dt)
    den = lam_re * lam_re + lam_im * lam_im
    f_re = ((ab_re - 1.0) * lam_re + ab_im * lam_im) / den
    f_im = (ab_im * lam_re - (ab_re - 1.0) * lam_im) / den
    b_re, b_im = p["c_b_re"][j], p["c_b_im"][j]
    bb_re = f_re[..., None] * b_re - f_im[..., None] * b_im
    bb_im = f_re[..., None] * b_im + f_im[..., None] * b_re
    gpt = LANES // GC
    eye = jnp.eye(gpt, dtype=F32)

    def in_layout(bb):
        t = bb.reshape(G_C // gpt, gpt, P_C, GC).transpose(0, 1, 3, 2)
        return jnp.einsum("qgcp,gh->qgchp", t, eye).reshape(G_C // gpt, gpt * GC, gpt * P_C).astype(BF16)

    def out_layout(cc):
        t = cc.reshape(G_C // gpt, gpt, GC, P_C).transpose(0, 1, 3, 2)
        return jnp.einsum("qgpc,gh->qgphc", t, eye).reshape(G_C // gpt, gpt * P_C, gpt * GC).astype(BF16)

    s5 = dict(wbr=in_layout(bb_re), wbi=in_layout(bb_im), wcr=out_layout(p["c_c_re"][j]),
              wci=out_layout(-p["c_c_im"][j]), are=ab_re.reshape(1, S5_STATE), aim=ab_im.reshape(1, S5_STATE),
              d=row(p["c_d"][j]), wglu=p["w_glu"][j].astype(BF16))
    ffn = [dict(g=row(p["norm_ffn"][i]), wu=p["w_up"][i].astype(BF16), cw=p["conv_w"][i].astype(F32),
                cb=row(p["conv_b"][i]), wd=p["w_down"][i].astype(BF16)) for i in range(2)]
    return dict(l0=l0, rw=rw, s5=s5, ffn=ffn, g_mix=[row(p["norm_mix"][i]) for i in range(2)])


def _trunk(x, l_real, init, pp, past, y_trim=None):
    b, lp, _ = x.shape
    l0 = pp["l0"]
    flat = lp == l_real and lp < INPROJ_TILE
    xin = x.reshape(1, b * lp, D_MODEL) if flat else x
    outs = _inproj(xin, xin.shape[1] if flat else l_real, pp["g_mix"][0], l0["wa"], l0["wb"], l0["e"], l0["qg"],
                   l0["kg"], l0["fb"])
    ua, q, kf, kb, vf, vb, lf = [a.reshape((b, lp) + a.shape[2:]) for a in outs] if flat else outs
    y_a, a_shift, a_wkv = _rwkv(ua, init["a_shift"].reshape(b, 1, A_COLS), init["a_wkv"], pp["rw"], l_real)
    lf_row = jnp.transpose(lf[:, :, :N_HEADS], (0, 2, 1)).reshape(b * N_HEADS, lp)
    if past is None:
        lpad = -lp % LANES
        c_row = _cumsum_lanes(jnp.pad(lf_row, ((0, 0), (0, lpad))))[:, :lp].reshape(b, N_HEADS, lp)
        cq = jnp.pad(jnp.transpose(c_row, (0, 2, 1)), ((0, 0), (0, 0), (0, LANES - N_HEADS)))
        y_b = _attn_prompt(q, kb, vb, cq, c_row, min(ATTN_TILE, lp))
    else:
        pk, pv, plf = past
        plen = pk.shape[1]
        plf_row = jnp.transpose(plf, (0, 2, 1)).reshape(b * N_HEADS, plen)
        c_all = _cumsum_lanes(jnp.concatenate([plf_row, jnp.pad(lf_row, ((0, 0), (0, -lp % LANES)))], axis=1))
        c_past = c_all[:, :plen].reshape(b, N_HEADS, plen)
        c_new = c_all[:, plen:plen + lp].reshape(b, N_HEADS, lp)
        cq_stack = jnp.broadcast_to(c_new.reshape(b, N_HEADS * lp, 1), (b, N_HEADS * lp, LANES))
        y_b = _attn_sample(q, cq_stack, pk.reshape(b, plen, W_MIX), pv.reshape(b, plen, W_MIX), c_past,
                           kb, vb, c_new, l0["hm"])
    f0 = pp["ffn"][0]
    x2, conv0 = _ffn(x, f0["g"], f0["wu"], f0["cw"], f0["cb"], f0["wd"], init["ffn_conv"][0], l_real,
                     mixer=(y_a, y_b, l0["w_out"]))
    x3, c_re, c_im = _s5(x2, pp["g_mix"][1], pp["s5"], init["c_re"].reshape(b, S5_STATE),
                         init["c_im"].reshape(b, S5_STATE), l_real)
    f1 = pp["ffn"][1]
    x4, conv1 = _ffn(x3, f1["g"], f1["wu"], f1["cw"], f1["cb"], f1["wd"], init["ffn_conv"][1], l_real,
                     trim=y_trim)
    states = (a_shift.reshape(1, b, A_COLS), a_wkv[None], kf[None], vf[None],
              lf[None, :, :l_real, :N_HEADS], c_re.reshape(1, b, G_C, P_C), c_im.reshape(1, b, G_C, P_C),
              jnp.stack([conv0, conv1], axis=0))
    return x4, states


def kernel(x_prompt, x_sample, state_a_shift, state_a_wkv, cache_b_k, cache_b_v, cache_b_logf, state_c_re, state_c_im, state_ffn_conv, meta, norm_mix, norm_ffn, w_in_e, a_mu, a_w0, a_w2, a_a0, a_a2, a_g2, a_kk, a_ka, a_rk, a_ln_w, a_ln_b, b_fbias, b_qnorm, b_knorm, w_out_e, c_lam_re, c_lam_im, c_log_dt, c_b_re, c_b_im, c_c_re, c_c_im, c_d, w_glu, w_up, conv_w, conv_b, w_down):
    pp = _prep_params(dict(
        norm_mix=norm_mix, norm_ffn=norm_ffn, w_in_e=w_in_e, a_mu=a_mu, a_w0=a_w0, a_w2=a_w2, a_a0=a_a0,
        a_a2=a_a2, a_g2=a_g2, a_kk=a_kk, a_ka=a_ka, a_rk=a_rk, a_ln_w=a_ln_w, a_ln_b=a_ln_b,
        b_fbias=b_fbias, b_qnorm=b_qnorm, b_knorm=b_knorm, w_out_e=w_out_e, c_lam_re=c_lam_re,
        c_lam_im=c_lam_im, c_log_dt=c_log_dt, c_b_re=c_b_re, c_b_im=c_b_im, c_c_re=c_c_re, c_c_im=c_c_im,
        c_d=c_d, w_glu=w_glu, w_up=w_up, conv_w=conv_w, conv_b=conv_b, w_down=w_down))
    b, seq, _ = x_prompt.shape
    n_meta = meta.shape[0]
    lr = n_meta + seq
    lp = -(-lr // WKV_CHUNK) * WKV_CHUNK
    xp = jnp.concatenate([jnp.broadcast_to(meta.astype(F32)[None], (b, n_meta, D_MODEL)), x_prompt,
                          jnp.zeros((b, lp - lr, D_MODEL), F32)], axis=1)
    init_p = dict(a_shift=jnp.zeros((b, A_COLS), F32),
                  a_wkv=jnp.zeros((b, N_HEADS, HEAD_DIM, HEAD_DIM), F32),
                  c_re=jnp.zeros((b, S5_STATE), F32), c_im=jnp.zeros((b, S5_STATE), F32),
                  ffn_conv=jnp.zeros((2, b, CONV_W - 1, 2 * D_FF), F32))
    yp, st_p = _trunk(xp, lr, init_p, pp, None, y_trim=(n_meta, seq))
    bs, ls, _ = x_sample.shape
    init_s = dict(a_shift=state_a_shift[0], a_wkv=state_a_wkv[0], c_re=state_c_re[0], c_im=state_c_im[0],
                  ffn_conv=state_ffn_conv)
    past = (cache_b_k[0], cache_b_v[0], cache_b_logf[0])
    ys, st_s = _trunk(x_sample, ls, init_s, pp, past)
    return (yp, ys, *st_p, *st_s)
```
